```python
import math
import jax, jax.numpy as jnp
from jax import lax
import numpy as np

D_MODEL = 1024
BATCH = 8
SEQ = 2048
DEPTH = 1
DEC_BATCH = 128
DEC_SEQ = 8
PAST_LEN = 16384
PAGE_SIZE = 128

MIX_DIM = D_MODEL
SSM_DIM = MIX_DIM // 2
CONV_DIM = MIX_DIM - SSM_DIM
SSM_GROUP_CH = 16
SSM_GROUPS = SSM_DIM // SSM_GROUP_CH
SSM_STATE = 64
CONV_WIDTH = 31
N_MEM = 256
MEM_HEADS = 4
MEM_HEAD_DIM = D_MODEL // MEM_HEADS
D_FF = ((8 * D_MODEL // 3 + 127) // 128) * 128
EPS = 1e-6
DT_MIN = 1e-3
DT_MAX = 1e-1

kernel_name = "hymba_s5_conformer_macaron_memxattn_step"


def rms_norm(x, g):
    xf = x.astype(jnp.float32)
    y = xf * lax.rsqrt(jnp.mean(xf * xf, axis=-1, keepdims=True) + EPS)
    return (y * g.astype(jnp.float32)).astype(x.dtype)


def layer_norm(x, g, b):
    xf = x.astype(jnp.float32)
    mu = jnp.mean(xf, axis=-1, keepdims=True)
    var = jnp.mean(jnp.square(xf - mu), axis=-1, keepdims=True)
    y = (xf - mu) * lax.rsqrt(var + EPS) * g.astype(jnp.float32) + b.astype(jnp.float32)
    return y.astype(x.dtype)


def swiglu_ffn(x, w_gate, w_up, w_down):
    return (jax.nn.silu(x @ w_gate) * (x @ w_up)) @ w_down


def s5_mixer(u, h0_re, h0_im, a_re, a_im, log_dt, b_re, b_im, c_re, c_im, d_skip, w_glu):
    f32 = jnp.float32
    bsz, t_len, _ = u.shape
    dt = jnp.exp(log_dt.astype(f32))[:, None]
    lr, li = a_re.astype(f32), a_im.astype(f32)
    mag = jnp.exp(lr * dt)
    ab_re, ab_im = mag * jnp.cos(li * dt), mag * jnp.sin(li * dt)
    den = lr * lr + li * li
    nr, ni = ab_re - 1.0, ab_im
    coef_re = (nr * lr + ni * li) / den
    coef_im = (ni * lr - nr * li) / den
    uf = u.astype(f32)
    ug = uf.reshape(bsz, t_len, SSM_GROUPS, SSM_GROUP_CH)
    bu_re = jnp.einsum('btgc,gnc->btgn', ug, b_re.astype(f32))
    bu_im = jnp.einsum('btgc,gnc->btgn', ug, b_im.astype(f32))
    x_re = coef_re * bu_re - coef_im * bu_im
    x_im = coef_re * bu_im + coef_im * bu_re
    a_re_t = jnp.broadcast_to(ab_re, x_re.shape)
    a_im_t = jnp.broadcast_to(ab_im, x_re.shape)

    def combine(e1, e2):
        a1r, a1i, b1r, b1i = e1
        a2r, a2i, b2r, b2i = e2
        return (a1r * a2r - a1i * a2i,
                a1r * a2i + a1i * a2r,
                a2r * b1r - a2i * b1i + b2r,
                a2r * b1i + a2i * b1r + b2i)

    p_re, p_im, s_re, s_im = lax.associative_scan(combine, (a_re_t, a_im_t, x_re, x_im), axis=1)
    h0r = h0_re.astype(f32)[:, None]
    h0i = h0_im.astype(f32)[:, None]
    h_re = s_re + p_re * h0r - p_im * h0i
    h_im = s_im + p_re * h0i + p_im * h0r
    y = (jnp.einsum('btgn,gcn->btgc', h_re, c_re.astype(f32))
         - jnp.einsum('btgn,gcn->btgc', h_im, c_im.astype(f32)))
    y = y.reshape(bsz, t_len, SSM_DIM) + d_skip.astype(f32) * uf
    g = jax.nn.gelu(y)
    out = g * jax.nn.sigmoid(g @ w_glu.astype(f32))
    return out.astype(u.dtype), h_re[:, -1], h_im[:, -1]


def conformer_conv_mixer(val, gate, buf, w_dw, b_dw, ln_g, ln_b):
    v = val * jax.nn.sigmoid(gate)
    xp = jnp.concatenate([buf.astype(v.dtype), v], axis=1)
    y = lax.conv_general_dilated(xp, w_dw[:, None, :].astype(v.dtype), window_strides=(1,),
                                 padding='VALID', dimension_numbers=('NWC', 'WIO', 'NWC'),
                                 feature_group_count=CONV_DIM) + b_dw
    y = jax.nn.silu(layer_norm(y, ln_g, ln_b))
    return y, xp[:, -(CONV_WIDTH - 1):]


def mem_kv(mem, g_mem, w_k, w_v):
    bsz = mem.shape[0]
    m = rms_norm(mem, g_mem)
    k = (m @ w_k).reshape(bsz, N_MEM, MEM_HEADS, MEM_HEAD_DIM)
    v = (m @ w_v).reshape(bsz, N_MEM, MEM_HEADS, MEM_HEAD_DIM)
    return k, v


def mem_attend(h, k, v, w_q, w_o):
    bsz, t_len, _ = h.shape
    q = (h @ w_q).reshape(bsz, t_len, MEM_HEADS, MEM_HEAD_DIM)
    s = jnp.einsum('bthd,bmhd->bhtm', q, k.astype(q.dtype)).astype(jnp.float32) * (MEM_HEAD_DIM ** -0.5)
    p = jax.nn.softmax(s, axis=-1).astype(h.dtype)
    o = jnp.einsum('bhtm,bmhd->bthd', p, v.astype(h.dtype)).reshape(bsz, t_len, MEM_HEADS * MEM_HEAD_DIM)
    return o @ w_o


def decoder_layer(x, mk, mv, h0_re, h0_im, conv_buf,
                  g_ffn1, w_ffn1_gate, w_ffn1_up, w_ffn1_down,
                  g_mix, w_in,
                  ssm_a_re, ssm_a_im, ssm_log_dt, ssm_b_re, ssm_b_im, ssm_c_re, ssm_c_im, ssm_d, w_ssm_glu,
                  conv_w, conv_b, conv_ln_g, conv_ln_b,
                  w_out, g_xattn, w_mem_q, w_mem_o,
                  g_ffn2, w_ffn2_gate, w_ffn2_up, w_ffn2_down):
    x = x + 0.5 * swiglu_ffn(rms_norm(x, g_ffn1), w_ffn1_gate, w_ffn1_up, w_ffn1_down)
    h = rms_norm(x, g_mix)
    z = h @ w_in
    u_ssm = z[..., :SSM_DIM]
    c_val = z[..., SSM_DIM:SSM_DIM + CONV_DIM]
    c_gate = z[..., SSM_DIM + CONV_DIM:]
    s_out, h_re, h_im = s5_mixer(u_ssm, h0_re, h0_im, ssm_a_re, ssm_a_im, ssm_log_dt,
                                 ssm_b_re, ssm_b_im, ssm_c_re, ssm_c_im, ssm_d, w_ssm_glu)
    c_out, new_buf = conformer_conv_mixer(c_val, c_gate, conv_buf, conv_w, conv_b, conv_ln_g, conv_ln_b)
    x = x + jnp.concatenate([s_out, c_out], axis=-1) @ w_out
    x = x + mem_attend(rms_norm(x, g_xattn), mk, mv, w_mem_q, w_mem_o)
    x = x + 0.5 * swiglu_ffn(rms_norm(x, g_ffn2), w_ffn2_gate, w_ffn2_up, w_ffn2_down)
    return x, h_re.astype(x.dtype), h_im.astype(x.dtype), new_buf


def setup_inputs(seed: int = 0) -> dict:
    key = jax.random.key(seed)
    ks = iter(jax.random.split(key, 64))
    f32 = jnp.float32

    def nrm(shape, scale):
        return jax.random.normal(next(ks), shape, f32) * scale

    def gain(shape):
        return 1.0 + 0.02 * jax.random.normal(next(ks), shape, f32)

    L = DEPTH
    n_idx = jnp.arange(SSM_STATE, dtype=f32)
    inp = {}
    inp['x_prompt'] = nrm((BATCH, SEQ, D_MODEL), 1.0)
    inp['x_sample'] = nrm((DEC_BATCH, DEC_SEQ, D_MODEL), 1.0)
    inp['state_ssm_re'] = nrm((L, DEC_BATCH, SSM_GROUPS, SSM_STATE), 0.5)
    inp['state_ssm_im'] = nrm((L, DEC_BATCH, SSM_GROUPS, SSM_STATE), 0.5)
    inp['cache_conv'] = nrm((L, DEC_BATCH, CONV_WIDTH - 1, CONV_DIM), 0.5)
    inp['cache_mem_k'] = nrm((L, DEC_BATCH, N_MEM, MEM_HEADS, MEM_HEAD_DIM), 1.0)
    inp['cache_mem_v'] = nrm((L, DEC_BATCH, N_MEM, MEM_HEADS, MEM_HEAD_DIM), 1.0)
    inp['mem_prompt'] = nrm((BATCH, N_MEM, D_MODEL), 1.0)
    inp['g_mem'] = gain((L, D_MODEL))
    inp['w_mem_k'] = nrm((L, D_MODEL, MEM_HEADS * MEM_HEAD_DIM), D_MODEL ** -0.5)
    inp['w_mem_v'] = nrm((L, D_MODEL, MEM_HEADS * MEM_HEAD_DIM), D_MODEL ** -0.5)
    inp['g_ffn1'] = gain((L, D_MODEL))
    inp['w_ffn1_gate'] = nrm((L, D_MODEL, D_FF), D_MODEL ** -0.5)
    inp['w_ffn1_up'] = nrm((L, D_MODEL, D_FF), D_MODEL ** -0.5)
    inp['w_ffn1_down'] = nrm((L, D_FF, D_MODEL), D_FF ** -0.5)
    inp['g_mix'] = gain((L, D_MODEL))
    inp['w_in'] = nrm((L, D_MODEL, SSM_DIM + 2 * CONV_DIM), D_MODEL ** -0.5)
    inp['ssm_a_re'] = -0.5 + 0.01 * jax.random.normal(next(ks), (L, SSM_GROUPS, SSM_STATE), f32)
    inp['ssm_a_im'] = math.pi * n_idx + 0.01 * jax.random.normal(next(ks), (L, SSM_GROUPS, SSM_STATE), f32)
    inp['ssm_log_dt'] = jax.random.uniform(next(ks), (L, SSM_GROUPS), f32,
                                           minval=math.log(DT_MIN), maxval=math.log(DT_MAX))
    inp['ssm_b_re'] = nrm((L, SSM_GROUPS, SSM_STATE, SSM_GROUP_CH), (2 * SSM_GROUP_CH) ** -0.5)
    inp['ssm_b_im'] = nrm((L, SSM_GROUPS, SSM_STATE, SSM_GROUP_CH), (2 * SSM_GROUP_CH) ** -0.5)
    inp['ssm_c_re'] = nrm((L, SSM_GROUPS, SSM_GROUP_CH, SSM_STATE), (2 * SSM_STATE) ** -0.5)
    inp['ssm_c_im'] = nrm((L, SSM_GROUPS, SSM_GROUP_CH, SSM_STATE), (2 * SSM_STATE) ** -0.5)
    inp['ssm_d'] = nrm((L, SSM_DIM), 1.0)
    inp['w_ssm_glu'] = nrm((L, SSM_DIM, SSM_DIM), SSM_DIM ** -0.5)
    inp['conv_w'] = nrm((L, CONV_WIDTH, CONV_DIM), CONV_WIDTH ** -0.5)
    inp['conv_b'] = nrm((L, CONV_DIM), 0.02)
    inp['conv_ln_g'] = gain((L, CONV_DIM))
    inp['conv_ln_b'] = nrm((L, CONV_DIM), 0.02)
    inp['w_out'] = nrm((L, MIX_DIM, D_MODEL), MIX_DIM ** -0.5)
    inp['g_xattn'] = gain((L, D_MODEL))
    inp['w_mem_q'] = nrm((L, D_MODEL, MEM_HEADS * MEM_HEAD_DIM), D_MODEL ** -0.5)
    inp['w_mem_o'] = nrm((L, MEM_HEADS * MEM_HEAD_DIM, D_MODEL), D_MODEL ** -0.5)
    inp['g_ffn2'] = gain((L, D_MODEL))
    inp['w_ffn2_gate'] = nrm((L, D_MODEL, D_FF), D_MODEL ** -0.5)
    inp['w_ffn2_up'] = nrm((L, D_MODEL, D_FF), D_MODEL ** -0.5)
    inp['w_ffn2_down'] = nrm((L, D_FF, D_MODEL), D_FF ** -0.5)
    inp['g_final'] = gain((D_MODEL,))
    return inp


def reference(x_prompt, x_sample, state_ssm_re, state_ssm_im, cache_conv, cache_mem_k, cache_mem_v,
              mem_prompt, g_mem, w_mem_k, w_mem_v,
              g_ffn1, w_ffn1_gate, w_ffn1_up, w_ffn1_down,
              g_mix, w_in,
              ssm_a_re, ssm_a_im, ssm_log_dt, ssm_b_re, ssm_b_im, ssm_c_re, ssm_c_im, ssm_d, w_ssm_glu,
              conv_w, conv_b, conv_ln_g, conv_ln_b,
              w_out, g_xattn, w_mem_q, w_mem_o,
              g_ffn2, w_ffn2_gate, w_ffn2_up, w_ffn2_down,
              g_final):
    bp = x_prompt.shape[0]
    yp, ys = x_prompt, x_sample
    p_re, p_im, p_conv, p_mk, p_mv = [], [], [], [], []
    s_re, s_im, s_conv = [], [], []
    for l in range(DEPTH):
        lw = (g_ffn1[l], w_ffn1_gate[l], w_ffn1_up[l], w_ffn1_down[l],
              g_mix[l], w_in[l],
              ssm_a_re[l], ssm_a_im[l], ssm_log_dt[l], ssm_b_re[l], ssm_b_im[l],
              ssm_c_re[l], ssm_c_im[l], ssm_d[l], w_ssm_glu[l],
              conv_w[l], conv_b[l], conv_ln_g[l], conv_ln_b[l],
              w_out[l], g_xattn[l], w_mem_q[l], w_mem_o[l],
              g_ffn2[l], w_ffn2_gate[l], w_ffn2_up[l], w_ffn2_down[l])
        mk, mv = mem_kv(mem_prompt, g_mem[l], w_mem_k[l], w_mem_v[l])
        h0 = jnp.zeros((bp, SSM_GROUPS, SSM_STATE), jnp.float32)
        buf0 = jnp.zeros((bp, CONV_WIDTH - 1, CONV_DIM), yp.dtype)
        yp, hr, hi, nb = decoder_layer(yp, mk, mv, h0, h0, buf0, *lw)
        p_re.append(hr); p_im.append(hi); p_conv.append(nb); p_mk.append(mk); p_mv.append(mv)
        ys, hr, hi, nb = decoder_layer(ys, cache_mem_k[l], cache_mem_v[l], state_ssm_re[l],
                                       state_ssm_im[l], cache_conv[l], *lw)
        s_re.append(hr); s_im.append(hi); s_conv.append(nb)
    y_prompt = rms_norm(yp, g_final)
    y_sample = rms_norm(ys, g_final)
    return (y_prompt, y_sample,
            jnp.stack(p_re), jnp.stack(p_im), jnp.stack(p_conv), jnp.stack(p_mk), jnp.stack(p_mv),
            jnp.stack(s_re), jnp.stack(s_im), jnp.stack(s_conv))
```

```python
import functools
import math

import jax
import jax.numpy as jnp
from jax import lax
from jax.experimental import pallas as pl
from jax.experimental.pallas import tpu as pltpu

F32 = jnp.float32
BF16 = jnp.bfloat16

EPS = 1e-6
D_MODEL = 1024
D_FF = 2816
SSM_DIM = 512
CONV_DIM = 512
SSM_GROUPS = 32
SSM_GROUP_CH = 16
SSM_STATE = 64
N_STATE = SSM_GROUPS * SSM_STATE
CONV_WIDTH = 31
N_MEM = 256
MEM_HEADS = 4
MEM_HEAD_DIM = 256

LANES = 128
SUBLANES = 8
STATE_TILES = N_STATE // LANES
CH_TILES = SSM_DIM // LANES
MODEL_TILES = D_MODEL // LANES
GROUPS_PER_TILE = LANES // SSM_GROUP_CH
STATES_PER_CH_TILE = GROUPS_PER_TILE * SSM_STATE
HIST = 32
SCAN_TILE_GROUP = 4
CONV_ROWS = 64
VMEM_LIMIT = 56 * 1024 * 1024


def _dot(a, b):
    return jnp.dot(a, b, preferred_element_type=F32)


def _rms(x, g):
    return x * lax.rsqrt(jnp.mean(x * x, axis=-1, keepdims=True) + EPS) * g


def _const_spec(shape):
    zeros = (0,) * len(shape)
    return pl.BlockSpec(shape, lambda *_: zeros, pipeline_mode=pl.Buffered(1))


def _params(sem):
    return pltpu.CompilerParams(dimension_semantics=sem, vmem_limit_bytes=VMEM_LIMIT)


def _s5_params_kernel(ar_ref, ai_ref, ldt_ref, abr_ref, abi_ref, cfr_ref, cfi_ref):
    lr = ar_ref[...]
    li = ai_ref[...]
    dt = jnp.exp(ldt_ref[...])
    mag = jnp.exp(lr * dt)
    ab_re = mag * jnp.cos(li * dt)
    ab_im = mag * jnp.sin(li * dt)
    den = lr * lr + li * li
    nr = ab_re - 1.0
    ni = ab_im
    abr_ref[...] = ab_re
    abi_ref[...] = ab_im
    cfr_ref[...] = (nr * lr + ni * li) / den
    cfi_ref[...] = (ni * lr - nr * li) / den


def _s5_params(a_re, a_im, log_dt):
    shp = jax.ShapeDtypeStruct((SSM_GROUPS, SSM_STATE), F32)
    return pl.pallas_call(
        _s5_params_kernel,
        out_shape=(shp, shp, shp, shp),
        name="s5_params",
    )(a_re, a_im, log_dt.reshape(SSM_GROUPS, 1))


def _mem_kv_kernel(m_ref, g_ref, wk_ref, wv_ref, k_ref, v_ref):
    m = _rms(m_ref[...], g_ref[...]).astype(BF16)
    k_ref[...] = _dot(m, wk_ref[...])
    v_ref[...] = _dot(m, wv_ref[...])


def _mem_kv(mem2d, g, wk, wv, tm=512):
    rows = mem2d.shape[0]
    row_spec = pl.BlockSpec((tm, D_MODEL), lambda i: (i, 0))
    out = jax.ShapeDtypeStruct((rows, D_MODEL), F32)
    return pl.pallas_call(
        _mem_kv_kernel,
        grid=(rows // tm,),
        in_specs=[row_spec, _const_spec((1, D_MODEL)),
                  _const_spec((D_MODEL, D_MODEL)), _const_spec((D_MODEL, D_MODEL))],
        out_specs=(row_spec, row_spec),
        out_shape=(out, out),
        compiler_params=_params(("arbitrary",)),
        name="mem_kv",
    )(mem2d, g, wk, wv)


def _swiglu_half(x, g, wg_ref, wu_ref, wd_ref):
    hn = _rms(x, g).astype(BF16)
    gate = _dot(hn, wg_ref[...])
    up = _dot(hn, wu_ref[...])
    act = (gate * jax.nn.sigmoid(gate) * up).astype(BF16)
    return x + 0.5 * _dot(act, wd_ref[...])


def _ffn_in_kernel(x_ref, g1_ref, wg_ref, wu_ref, wd_ref, g2_ref, win_ref, x1_ref, z_ref):
    x1 = _swiglu_half(x_ref[...], g1_ref[...], wg_ref, wu_ref, wd_ref)
    x1_ref[...] = x1
    z_ref[...] = _dot(_rms(x1, g2_ref[...]).astype(BF16), win_ref[...])


def _ffn_in(x2d, g1, wg, wu, wd, g2, win, tm=512):
    rows = x2d.shape[0]
    zdim = win.shape[1]
    row_spec = pl.BlockSpec((tm, D_MODEL), lambda i: (i, 0))
    return pl.pallas_call(
        _ffn_in_kernel,
        grid=(rows // tm,),
        in_specs=[row_spec, _const_spec((1, D_MODEL)),
                  _const_spec((D_MODEL, D_FF)), _const_spec((D_MODEL, D_FF)), _const_spec((D_FF, D_MODEL)),
                  _const_spec((1, D_MODEL)), _const_spec((D_MODEL, zdim))],
        out_specs=(row_spec, pl.BlockSpec((tm, zdim), lambda i: (i, 0))),
        out_shape=(jax.ShapeDtypeStruct((rows, D_MODEL), F32), jax.ShapeDtypeStruct((rows, zdim), F32)),
        compiler_params=_params(("arbitrary",)),
        name="ffn_in",
    )(x2d, g1, wg, wu, wd, g2, win)


def _out_ffn_kernel(x2_ref, o_ref, wo_ref, g_ref, wg_ref, wu_ref, wd_ref, gf_ref, y_ref):
    x3 = x2_ref[...] + _dot(o_ref[...], wo_ref[...])
    x4 = _swiglu_half(x3, g_ref[...], wg_ref, wu_ref, wd_ref)
    y_ref[...] = _rms(x4, gf_ref[...])


def _out_ffn(x2, o, wo, g, wg, wu, wd, gf, tm=512):
    rows = x2.shape[0]
    row_spec = pl.BlockSpec((tm, D_MODEL), lambda i: (i, 0))
    return pl.pallas_call(
        _out_ffn_kernel,
        grid=(rows // tm,),
        in_specs=[row_spec, row_spec, _const_spec((D_MODEL, D_MODEL)), _const_spec((1, D_MODEL)),
                  _const_spec((D_MODEL, D_FF)), _const_spec((D_MODEL, D_FF)), _const_spec((D_FF, D_MODEL)),
                  _const_spec((1, D_MODEL))],
        out_specs=row_spec,
        out_shape=jax.ShapeDtypeStruct((rows, D_MODEL), F32),
        compiler_params=_params(("arbitrary",)),
        name="out_ffn",
    )(x2, o, wo, g, wg, wu, wd, gf)


def _attn_kernel(q_ref, k_ref, v_ref, o_ref, *, nseq, lq):
    scale = MEM_HEAD_DIM ** -0.5
    for s in range(nseq):
        rows = slice(s * lq, (s + 1) * lq)
        for h in range(MEM_HEADS):
            cols = slice(h * MEM_HEAD_DIM, (h + 1) * MEM_HEAD_DIM)
            q = q_ref[rows, cols]
            k = k_ref[s, :, cols].astype(BF16)
            v = v_ref[s, :, cols].astype(BF16)
            sc = lax.dot_general(q, k, (((1,), (1,)), ((), ())), preferred_element_type=F32) * scale
            e = jnp.exp(sc - jnp.max(sc, axis=-1, keepdims=True))
            p = e / jnp.sum(e, axis=-1, keepdims=True)
            o_ref[rows, cols] = _dot(p.astype(BF16), v).astype(BF16)


def _attn(q2d, k3d, v3d, *, nseq, lq):
    rows = q2d.shape[0]
    seq_len = rows // k3d.shape[0]
    q_per_seq = seq_len // lq if nseq == 1 else 1
    if nseq == 1:
        grid = (k3d.shape[0], q_per_seq)
        q_spec = pl.BlockSpec((lq, D_MODEL), lambda b, i: (b * q_per_seq + i, 0))
        kv_spec = pl.BlockSpec((1, N_MEM, D_MODEL), lambda b, i: (b, 0, 0))
        sem = ("arbitrary", "arbitrary")
    else:
        assert lq == seq_len
        grid = (k3d.shape[0] // nseq,)
        q_spec = pl.BlockSpec((nseq * lq, D_MODEL), lambda i: (i, 0))
        kv_spec = pl.BlockSpec((nseq, N_MEM, D_MODEL), lambda i: (i, 0, 0))
        sem = ("arbitrary",)
    return pl.pallas_call(
        functools.partial(_attn_kernel, nseq=nseq, lq=lq),
        grid=grid,
        in_specs=[q_spec, kv_spec, kv_spec],
        out_specs=q_spec,
        out_shape=jax.ShapeDtypeStruct((rows, D_MODEL), BF16),
        compiler_params=_params(sem),
        name="attn",
    )(q2d, k3d, v3d)


def _mixer_kernel(*refs, ns, tt, has_init):
    it = iter(refs)
    u_ref, val_ref, gate_ref, x1_ref = next(it), next(it), next(it), next(it)
    if has_init:
        h0r_ref, h0i_ref, cache_ref = next(it), next(it), next(it)
    (bblk_ref, cre_ref, cim_ref, cfr_ref, cfi_ref, ar_ref, ai_ref, d_ref, wglu_ref,
     cw_ref, cb_ref, lng_ref, lnb_ref, wout_ref, gx_ref, wq_ref) = [next(it) for _ in range(16)]
    x2_ref, q_ref, hre_out, him_out, buf_out = [next(it) for _ in range(5)]
    utm, xr, xi, hcr, hci, vext, ytm, dtm, xn = [next(it) for _ in range(9)]

    rows = ns * tt
    hist_rows = HIST * ns
    chunk = pl.program_id(0)

    if has_init:
        for j in range(STATE_TILES):
            hcr[j] = h0r_ref[:, j * LANES:(j + 1) * LANES]
            hci[j] = h0i_ref[:, j * LANES:(j + 1) * LANES]
        for s in range(ns):
            for j in range(CH_TILES):
                vext[j, pl.ds((HIST - (CONV_WIDTH - 1)) * ns + s, CONV_WIDTH - 1, stride=ns), :] = (
                    cache_ref[s, :, j * LANES:(j + 1) * LANES])
    else:
        @pl.when(chunk == 0)
        def _():
            hcr[...] = jnp.zeros_like(hcr)
            hci[...] = jnp.zeros_like(hci)
            vext[:, :hist_rows, :] = jnp.zeros((CH_TILES, hist_rows, LANES), F32)

    for s in range(ns):
        u_s = u_ref[s]
        v_s = val_ref[s] * jax.nn.sigmoid(gate_ref[s])
        for j in range(CH_TILES):
            utm[j, pl.ds(s, tt, stride=ns), :] = u_s[:, j * LANES:(j + 1) * LANES]
            vext[j, pl.ds(hist_rows + s, tt, stride=ns), :] = v_s[:, j * LANES:(j + 1) * LANES]

    for kb in range(CH_TILES):
        r = _dot(utm[kb].astype(BF16), bblk_ref[kb])
        re = r[:, :STATES_PER_CH_TILE]
        im = r[:, STATES_PER_CH_TILE:]
        st = slice(kb * STATES_PER_CH_TILE, (kb + 1) * STATES_PER_CH_TILE)
        cr = cfr_ref[:, st]
        ci = cfi_ref[:, st]
        x_re = cr * re - ci * im
        x_im = cr * im + ci * re
        for i in range(STATES_PER_CH_TILE // LANES):
            j = kb * (STATES_PER_CH_TILE // LANES) + i
            xr[j] = x_re[:, i * LANES:(i + 1) * LANES]
            xi[j] = x_im[:, i * LANES:(i + 1) * LANES]

    for i in range(ns // SUBLANES):
        srow = slice(i * SUBLANES, (i + 1) * SUBLANES)
        for jg in range(STATE_TILES // SCAN_TILE_GROUP):
            tiles = tuple(range(jg * SCAN_TILE_GROUP, (jg + 1) * SCAN_TILE_GROUP))
            a_r = [ar_ref[j] for j in tiles]
            a_i = [ai_ref[j] for j in tiles]
            init = tuple(hcr[j, srow, :] for j in tiles) + tuple(hci[j, srow, :] for j in tiles)

            def body(t, carry, tiles=tiles, a_r=a_r, a_i=a_i, i=i):
                row = pl.ds(pl.multiple_of(t * ns + i * SUBLANES, SUBLANES), SUBLANES)
                new_r, new_i = [], []
                for idx, j in enumerate(tiles):
                    h_r, h_i = carry[idx], carry[SCAN_TILE_GROUP + idx]
                    n_r = a_r[idx] * h_r - a_i[idx] * h_i + xr[j, row, :]
                    n_i = a_r[idx] * h_i + a_i[idx] * h_r + xi[j, row, :]
                    xr[j, row, :] = n_r
                    xi[j, row, :] = n_i
                    new_r.append(n_r)
                    new_i.append(n_i)
                return tuple(new_r) + tuple(new_i)

            fin = lax.fori_loop(0, tt, body, init, unroll=8)
            for idx, j in enumerate(tiles):
                hcr[j, srow, :] = fin[idx]
                hci[j, srow, :] = fin[SCAN_TILE_GROUP + idx]

    tiles_per_ch = STATES_PER_CH_TILE // LANES
    ys = []
    for ob in range(CH_TILES):
        h_re = jnp.concatenate([xr[ob * tiles_per_ch + i] for i in range(tiles_per_ch)], axis=1).astype(BF16)
        h_im = jnp.concatenate([xi[ob * tiles_per_ch + i] for i in range(tiles_per_ch)], axis=1).astype(BF16)
        ys.append(_dot(h_re, cre_ref[ob]) + _dot(h_im, cim_ref[ob]))
    u_all = jnp.concatenate([utm[j] for j in range(CH_TILES)], axis=1)
    y = jnp.concatenate(ys, axis=1) + d_ref[...] * u_all
    g = jax.nn.gelu(y)
    s_out = g * jax.nn.sigmoid(_dot(g.astype(BF16), wglu_ref[...]))

    first_tap = HIST - (CONV_WIDTH - 1)
    for j in range(CH_TILES):
        def conv_body(b, _, j=j):
            base = pl.multiple_of(b * CONV_ROWS, CONV_ROWS)
            acc = jnp.zeros((CONV_ROWS, LANES), F32)
            for k in range(CONV_WIDTH):
                acc = acc + cw_ref[j, k:k + 1, :] * vext[j, pl.ds(base + (first_tap + k) * ns, CONV_ROWS), :]
            ytm[j, pl.ds(base, CONV_ROWS), :] = acc
            return 0

        lax.fori_loop(0, rows // CONV_ROWS, conv_body, 0)
    yc = jnp.concatenate([ytm[j] for j in range(CH_TILES)], axis=1) + cb_ref[...]
    mu = jnp.mean(yc, axis=-1, keepdims=True)
    var = jnp.mean(jnp.square(yc - mu), axis=-1, keepdims=True)
    yn = (yc - mu) * lax.rsqrt(var + EPS) * lng_ref[...] + lnb_ref[...]
    c_out = yn * jax.nn.sigmoid(yn)

    mix = jnp.concatenate([s_out, c_out], axis=1).astype(BF16)
    delta = _dot(mix, wout_ref[...])
    for j in range(MODEL_TILES):
        dtm[j] = delta[:, j * LANES:(j + 1) * LANES]
    for s in range(ns):
        d_s = jnp.concatenate([dtm[j, pl.ds(s, tt, stride=ns), :] for j in range(MODEL_TILES)], axis=1)
        x2_s = x1_ref[s] + d_s
        x2_ref[s] = x2_s
        xn[s * tt:(s + 1) * tt, :] = _rms(x2_s, gx_ref[...])
    qv = _dot(xn[...].astype(BF16), wq_ref[...]).astype(BF16)
    for s in range(ns):
        q_ref[s] = qv[s * tt:(s + 1) * tt, :]

    hre_out[...] = jnp.concatenate([hcr[j] for j in range(STATE_TILES)], axis=1)
    him_out[...] = jnp.concatenate([hci[j] for j in range(STATE_TILES)], axis=1)

    def write_buf():
        for s in range(ns):
            for j in range(CH_TILES):
                buf_out[s, :, j * LANES:(j + 1) * LANES] = (
                    vext[j, pl.ds((tt + first_tap) * ns + s, CONV_WIDTH - 1, stride=ns), :])

    if has_init:
        write_buf()
    else:
        pl.when(chunk == pl.num_programs(0) - 1)(write_buf)
        vext[:, :hist_rows, :] = vext[:, rows:rows + hist_rows, :]


def _mixer(z3, x13, init, w, *, ns, tt):
    nseq, seq_len, _ = z3.shape
    has_init = init is not None
    if has_init:
        assert tt == seq_len and nseq % ns == 0
        grid = (nseq // ns,)
        blk = lambda i: (i, 0, 0)
        col = lambda c: (lambda i: (i, 0, c))
        st_map = lambda i: (i, 0)
    else:
        assert ns == nseq and seq_len % tt == 0
        grid = (seq_len // tt,)
        blk = lambda i: (0, i, 0)
        col = lambda c: (lambda i: (0, i, c))
        st_map = lambda i: (0, 0)
    rows = ns * tt
    assert rows % CONV_ROWS == 0 and ns % SUBLANES == 0 and (ns == SUBLANES or tt == SUBLANES)

    in_specs = [pl.BlockSpec((ns, tt, SSM_DIM), col(c)) for c in range(3)]
    in_specs.append(pl.BlockSpec((ns, tt, D_MODEL), blk))
    args = [z3, z3, z3, x13]
    if has_init:
        in_specs += [pl.BlockSpec((ns, N_STATE), st_map), pl.BlockSpec((ns, N_STATE), st_map),
                     pl.BlockSpec((ns, CONV_WIDTH - 1, CONV_DIM), blk)]
        args += list(init)
    weights = [w["bblk"], w["cre"], w["cim"], w["coef_re"], w["coef_im"], w["ab_re"], w["ab_im"], w["d"],
               w["w_glu"], w["conv_w"], w["conv_b"], w["ln_g"], w["ln_b"], w["w_out"], w["g_x"], w["w_q"]]
    in_specs += [_const_spec(a.shape) for a in weights]
    args += weights

    out_specs = (pl.BlockSpec((ns, tt, D_MODEL), blk), pl.BlockSpec((ns, tt, D_MODEL), blk),
                 pl.BlockSpec((ns, N_STATE), st_map), pl.BlockSpec((ns, N_STATE), st_map),
                 pl.BlockSpec((ns, CONV_WIDTH - 1, CONV_DIM), (blk if has_init else (lambda i: (0, 0, 0)))))
    out_shape = (jax.ShapeDtypeStruct((nseq, seq_len, D_MODEL), F32),
                 jax.ShapeDtypeStruct((nseq, seq_len, D_MODEL), BF16),
                 jax.ShapeDtypeStruct((nseq, N_STATE), F32), jax.ShapeDtypeStruct((nseq, N_STATE), F32),
                 jax.ShapeDtypeStruct((nseq, CONV_WIDTH - 1, CONV_DIM), F32))
    scratch = [pltpu.VMEM((CH_TILES, rows, LANES), F32),
               pltpu.VMEM((STATE_TILES, rows, LANES), F32),
               pltpu.VMEM((STATE_TILES, rows, LANES), F32),
               pltpu.VMEM((STATE_TILES, ns, LANES), F32),
               pltpu.VMEM((STATE_TILES, ns, LANES), F32),
               pltpu.VMEM((CH_TILES, rows + HIST * ns, LANES), F32),
               pltpu.VMEM((CH_TILES, rows, LANES), F32),
               pltpu.VMEM((MODEL_TILES, rows, LANES), F32),
               pltpu.VMEM((rows, D_MODEL), F32)]
    return pl.pallas_call(
        functools.partial(_mixer_kernel, ns=ns, tt=tt, has_init=has_init),
        grid=grid,
        in_specs=in_specs,
        out_specs=out_specs,
        out_shape=out_shape,
        scratch_shapes=scratch,
        compiler_params=_params(("arbitrary",)),
        name="mixer_init" if has_init else "mixer_zero",
    )(*args)


def _block_diag_tiles(x):
    _, a, b = x.shape
    x4 = x.reshape(CH_TILES, GROUPS_PER_TILE, a, b)
    eye = jnp.eye(GROUPS_PER_TILE, dtype=x.dtype)
    out = x4[:, :, :, None, :] * eye[None, :, None, :, None]
    return out.reshape(CH_TILES, GROUPS_PER_TILE * a, GROUPS_PER_TILE * b)


def _row(x):
    return x.reshape(1, -1)


def _state_slabs(x):
    return jnp.broadcast_to(x.reshape(STATE_TILES, 1, LANES), (STATE_TILES, SUBLANES, LANES))


def kernel(x_prompt, x_sample, state_ssm_re, state_ssm_im, cache_conv, cache_mem_k, cache_mem_v, mem_prompt,
           g_mem, w_mem_k, w_mem_v, g_ffn1, w_ffn1_gate, w_ffn1_up, w_ffn1_down, g_mix, w_in,
           ssm_a_re, ssm_a_im, ssm_log_dt, ssm_b_re, ssm_b_im, ssm_c_re, ssm_c_im, ssm_d, w_ssm_glu,
           conv_w, conv_b, conv_ln_g, conv_ln_b, w_out, g_xattn, w_mem_q, w_mem_o,
           g_ffn2, w_ffn2_gate, w_ffn2_up, w_ffn2_down, g_final):
    depth = g_ffn1.shape[0]
    assert depth == 1
    l = 0
    bp, seq, _ = x_prompt.shape
    bs, dseq, _ = x_sample.shape
    bf = lambda a: a.astype(BF16)

    ab_re, ab_im, coef_re, coef_im = _s5_params(ssm_a_re[l], ssm_a_im[l], ssm_log_dt[l])
    b_re_t = _block_diag_tiles(jnp.swapaxes(ssm_b_re[l], 1, 2))
    b_im_t = _block_diag_tiles(jnp.swapaxes(ssm_b_im[l], 1, 2))
    mixer_w = dict(
        bblk=bf(jnp.concatenate([b_re_t, b_im_t], axis=-1)),
        cre=bf(_block_diag_tiles(jnp.swapaxes(ssm_c_re[l], 1, 2))),
        cim=bf(-_block_diag_tiles(jnp.swapaxes(ssm_c_im[l], 1, 2))),
        coef_re=_row(coef_re), coef_im=_row(coef_im),
        ab_re=_state_slabs(ab_re), ab_im=_state_slabs(ab_im),
        d=_row(ssm_d[l]), w_glu=bf(w_ssm_glu[l]),
        conv_w=jnp.swapaxes(conv_w[l].reshape(CONV_WIDTH, CH_TILES, LANES), 0, 1),
        conv_b=_row(conv_b[l]), ln_g=_row(conv_ln_g[l]), ln_b=_row(conv_ln_b[l]),
        w_out=bf(w_out[l]), g_x=_row(g_xattn[l]), w_q=bf(w_mem_q[l]),
    )
    ffn1 = (_row(g_ffn1[l]), bf(w_ffn1_gate[l]), bf(w_ffn1_up[l]), bf(w_ffn1_down[l]))
    ffn2 = (_row(g_ffn2[l]), bf(w_ffn2_gate[l]), bf(w_ffn2_up[l]), bf(w_ffn2_down[l]))
    g_mix_r, w_in_b, w_o_b, g_final_r = _row(g_mix[l]), bf(w_in[l]), bf(w_mem_o[l]), _row(g_final)

    mk, mv = _mem_kv(mem_prompt.reshape(bp * N_MEM, D_MODEL), _row(g_mem[l]), bf(w_mem_k[l]), bf(w_mem_v[l]))

    def trunk(x3d, k3d, v3d, init, *, ns, tt, attn_nseq, attn_lq):
        nseq, slen, _ = x3d.shape
        x1, z = _ffn_in(x3d.reshape(nseq * slen, D_MODEL), *ffn1, g_mix_r, w_in_b)
        x2, q, h_re, h_im, buf = _mixer(z.reshape(nseq, slen, -1), x1.reshape(nseq, slen, D_MODEL), init,
                                        mixer_w, ns=ns, tt=tt)
        o = _attn(q.reshape(nseq * slen, D_MODEL), k3d, v3d, nseq=attn_nseq, lq=attn_lq)
        y = _out_ffn(x2.reshape(nseq * slen, D_MODEL), o, w_o_b, *ffn2, g_final_r)
        return y.reshape(nseq, slen, D_MODEL), h_re, h_im, buf

    yp, p_re, p_im, p_buf = trunk(x_prompt, mk.reshape(bp, N_MEM, D_MODEL), mv.reshape(bp, N_MEM, D_MODEL), None,
                                  ns=bp, tt=64, attn_nseq=1, attn_lq=512)
    init_s = (state_ssm_re[l].reshape(bs, N_STATE), state_ssm_im[l].reshape(bs, N_STATE), cache_conv[l])
    ys, s_re, s_im, s_buf = trunk(x_sample, cache_mem_k[l].reshape(bs, N_MEM, D_MODEL),
                                  cache_mem_v[l].reshape(bs, N_MEM, D_MODEL), init_s,
                                  ns=32, tt=dseq, attn_nseq=4, attn_lq=dseq)

    st = lambda a, n: a.reshape(1, n, SSM_GROUPS, SSM_STATE)
    kv = lambda a: a.reshape(1, bp, N_MEM, MEM_HEADS, MEM_HEAD_DIM)
    return (yp, ys, st(p_re, bp), st(p_im, bp), p_buf[None], kv(mk), kv(mv),
            st(s_re, bs), st(s_im, bs), s_buf[None])
```

```python
import functools
import math

import jax
import jax.numpy as jnp
from jax import lax
from jax.experimental import pallas as pl
from jax.experimental.pallas import tpu as pltpu

F32 = jnp.float32
BF16 = jnp.bfloat16

EPS = 1e-6
D_MODEL = 1024
D_FF = 2816
SSM_DIM = 512
CONV_DIM = 512
SSM_GROUPS = 32
SSM_GROUP_CH = 16
SSM_STATE = 64
N_STATE = SSM_GROUPS * SSM_STATE
CONV_WIDTH = 31
N_MEM = 256
MEM_HEADS = 4
MEM_HEAD_DIM = 256

LANES = 128
SUBLANES = 8
STATE_TILES = N_STATE // LANES
CH_TILES = SSM_DIM // LANES
MODEL_TILES = D_MODEL // LANES
GROUPS_PER_TILE = LANES // SSM_GROUP_CH
STATES_PER_CH_TILE = GROUPS_PER_TILE * SSM_STATE
HIST = 32
SCAN_TILE_GROUP = 4
CONV_ROWS = 64
VMEM_LIMIT = 56 * 1024 * 1024


def _dot(a, b):
    return jnp.dot(a, b, preferred_element_type=F32)


def _rms(x, g):
    return x * lax.rsqrt(jnp.mean(x * x, axis=-1, keepdims=True) + EPS) * g


def _const_spec(shape):
    zeros = (0,) * len(shape)
    return pl.BlockSpec(shape, lambda *_: zeros, pipeline_mode=pl.Buffered(1))


def _params(sem):
    return pltpu.CompilerParams(dimension_semantics=sem, vmem_limit_bytes=VMEM_LIMIT)


def _s5_params_kernel(ar_ref, ai_ref, ldt_ref, abr_ref, abi_ref, cfr_ref, cfi_ref):
    lr = ar_ref[...]
    li = ai_ref[...]
    dt = jnp.exp(ldt_ref[...])
    mag = jnp.exp(lr * dt)
    ab_re = mag * jnp.cos(li * dt)
    ab_im = mag * jnp.sin(li * dt)
    den = lr * lr + li * li
    nr = ab_re - 1.0
    ni = ab_im
    abr_ref[...] = ab_re
    abi_ref[...] = ab_im
    cfr_ref[...] = (nr * lr + ni * li) / den
    cfi_ref[...] = (ni * lr - nr * li) / den


def _s5_params(a_re, a_im, log_dt):
    shp = jax.ShapeDtypeStruct((SSM_GROUPS, SSM_STATE), F32)
    return pl.pallas_call(
        _s5_params_kernel,
        out_shape=(shp, shp, shp, shp),
        name="s5_params",
    )(a_re, a_im, log_dt.reshape(SSM_GROUPS, 1))


def _mem_kv_kernel(m_ref, g_ref, wk_ref, wv_ref, k_ref, v_ref):
    m = _rms(m_ref[...], g_ref[...]).astype(BF16)
    k = _dot(m, wk_ref[...])
    v = _dot(m, wv_ref[...])
    for h in range(MEM_HEADS):
        cols = slice(h * MEM_HEAD_DIM, (h + 1) * MEM_HEAD_DIM)
        k_ref[:, h, :] = k[:, cols]
        v_ref[:, h, :] = v[:, cols]


def _mem_kv(mem2d, g, wk, wv, tm=512):
    rows = mem2d.shape[0]
    row_spec = pl.BlockSpec((tm, D_MODEL), lambda i: (i, 0))
    head_spec = pl.BlockSpec((tm, MEM_HEADS, MEM_HEAD_DIM), lambda i: (i, 0, 0))
    out = jax.ShapeDtypeStruct((rows, MEM_HEADS, MEM_HEAD_DIM), F32)
    return pl.pallas_call(
        _mem_kv_kernel,
        grid=(rows // tm,),
        in_specs=[row_spec, _const_spec((1, D_MODEL)),
                  _const_spec((D_MODEL, D_MODEL)), _const_spec((D_MODEL, D_MODEL))],
        out_specs=(head_spec, head_spec),
        out_shape=(out, out),
        compiler_params=_params(("arbitrary",)),
        name="mem_kv",
    )(mem2d, g, wk, wv)


def _swiglu_half(x, g, wg_ref, wu_ref, wd_ref):
    hn = _rms(x, g).astype(BF16)
    gate = _dot(hn, wg_ref[...])
    up = _dot(hn, wu_ref[...])
    act = (gate * jax.nn.sigmoid(gate) * up).astype(BF16)
    return x + 0.5 * _dot(act, wd_ref[...])


def _ffn_in_kernel(x_ref, g1_ref, wg_ref, wu_ref, wd_ref, g2_ref, win_ref, x1_ref, z_ref):
    x1 = _swiglu_half(x_ref[...], g1_ref[...], wg_ref, wu_ref, wd_ref)
    x1_ref[...] = x1
    z_ref[...] = _dot(_rms(x1, g2_ref[...]).astype(BF16), win_ref[...])


def _ffn_in(x2d, g1, wg, wu, wd, g2, win, tm=512):
    rows = x2d.shape[0]
    zdim = win.shape[1]
    row_spec = pl.BlockSpec((tm, D_MODEL), lambda i: (i, 0))
    return pl.pallas_call(
        _ffn_in_kernel,
        grid=(rows // tm,),
        in_specs=[row_spec, _const_spec((1, D_MODEL)),
                  _const_spec((D_MODEL, D_FF)), _const_spec((D_MODEL, D_FF)), _const_spec((D_FF, D_MODEL)),
                  _const_spec((1, D_MODEL)), _const_spec((D_MODEL, zdim))],
        out_specs=(row_spec, pl.BlockSpec((tm, zdim), lambda i: (i, 0))),
        out_shape=(jax.ShapeDtypeStruct((rows, D_MODEL), F32), jax.ShapeDtypeStruct((rows, zdim), F32)),
        compiler_params=_params(("arbitrary",)),
        name="ffn_in",
    )(x2d, g1, wg, wu, wd, g2, win)


def _out_ffn_kernel(x2_ref, o_ref, wo_ref, g_ref, wg_ref, wu_ref, wd_ref, gf_ref, y_ref):
    x3 = x2_ref[...] + _dot(o_ref[...], wo_ref[...])
    x4 = _swiglu_half(x3, g_ref[...], wg_ref, wu_ref, wd_ref)
    y_ref[...] = _rms(x4, gf_ref[...])


def _out_ffn(x2, o, wo, g, wg, wu, wd, gf, tm=512):
    rows = x2.shape[0]
    row_spec = pl.BlockSpec((tm, D_MODEL), lambda i: (i, 0))
    return pl.pallas_call(
        _out_ffn_kernel,
        grid=(rows // tm,),
        in_specs=[row_spec, row_spec, _const_spec((D_MODEL, D_MODEL)), _const_spec((1, D_MODEL)),
                  _const_spec((D_MODEL, D_FF)), _const_spec((D_MODEL, D_FF)), _const_spec((D_FF, D_MODEL)),
                  _const_spec((1, D_MODEL))],
        out_specs=row_spec,
        out_shape=jax.ShapeDtypeStruct((rows, D_MODEL), F32),
        compiler_params=_params(("arbitrary",)),
        name="out_ffn",
    )(x2, o, wo, g, wg, wu, wd, gf)


def _attn_kernel(q_ref, k_ref, v_ref, o_ref, *, nseq, lq):
    scale = MEM_HEAD_DIM ** -0.5
    for s in range(nseq):
        rows = slice(s * lq, (s + 1) * lq)
        for h in range(MEM_HEADS):
            cols = slice(h * MEM_HEAD_DIM, (h + 1) * MEM_HEAD_DIM)
            q = q_ref[rows, cols]
            k = k_ref[s, :, h, :].astype(BF16)
            v = v_ref[s, :, h, :].astype(BF16)
            sc = lax.dot_general(q, k, (((1,), (1,)), ((), ())), preferred_element_type=F32) * scale
            e = jnp.exp(sc - jnp.max(sc, axis=-1, keepdims=True))
            p = e / jnp.sum(e, axis=-1, keepdims=True)
            o_ref[rows, cols] = _dot(p.astype(BF16), v).astype(BF16)


def _attn(q2d, k4d, v4d, *, nseq, lq):
    rows = q2d.shape[0]
    n_total = k4d.shape[0]
    seq_len = rows // n_total
    kv_block = (nseq, N_MEM, MEM_HEADS, MEM_HEAD_DIM)
    if nseq == 1:
        q_per_seq = seq_len // lq
        grid = (n_total, q_per_seq)
        q_spec = pl.BlockSpec((lq, D_MODEL), lambda b, i: (b * q_per_seq + i, 0))
        kv_spec = pl.BlockSpec(kv_block, lambda b, i: (b, 0, 0, 0))
        sem = ("arbitrary", "arbitrary")
    else:
        assert lq == seq_len
        grid = (n_total // nseq,)
        q_spec = pl.BlockSpec((nseq * lq, D_MODEL), lambda i: (i, 0))
        kv_spec = pl.BlockSpec(kv_block, lambda i: (i, 0, 0, 0))
        sem = ("arbitrary",)
    return pl.pallas_call(
        functools.partial(_attn_kernel, nseq=nseq, lq=lq),
        grid=grid,
        in_specs=[q_spec, kv_spec, kv_spec],
        out_specs=q_spec,
        out_shape=jax.ShapeDtypeStruct((rows, D_MODEL), BF16),
        compiler_params=_params(sem),
        name="attn",
    )(q2d, k4d, v4d)


def _mixer_kernel(*refs, ns, tt, has_init):
    it = iter(refs)
    u_ref, val_ref, gate_ref, x1_ref = next(it), next(it), next(it), next(it)
    if has_init:
        h0r_ref, h0i_ref, cache_ref = next(it), next(it), next(it)
    (bblk_ref, cre_ref, cim_ref, cfr_ref, cfi_ref, ar_ref, ai_ref, d_ref, wglu_ref,
     cw_ref, cb_ref, lng_ref, lnb_ref, wout_ref, gx_ref, wq_ref) = [next(it) for _ in range(16)]
    x2_ref, q_ref, hre_out, him_out, buf_out = [next(it) for _ in range(5)]
    utm, xr, xi, hcr, hci, vext, ytm, dtm, xn = [next(it) for _ in range(9)]

    rows = ns * tt
    hist_rows = HIST * ns
    chunk = pl.program_id(0)

    if has_init:
        for j in range(STATE_TILES):
            hcr[j] = h0r_ref[:, j * LANES:(j + 1) * LANES]
            hci[j] = h0i_ref[:, j * LANES:(j + 1) * LANES]
        for j in range(CH_TILES):
            vext[j, (HIST - (CONV_WIDTH - 1)) * ns:hist_rows, :] = (
                cache_ref[:, :, j * LANES:(j + 1) * LANES].reshape((CONV_WIDTH - 1) * ns, LANES))
    else:
        @pl.when(chunk == 0)
        def _():
            hcr[...] = jnp.zeros_like(hcr)
            hci[...] = jnp.zeros_like(hci)
            vext[:, :hist_rows, :] = jnp.zeros((CH_TILES, hist_rows, LANES), F32)

    for s in range(ns):
        u_s = u_ref[s]
        v_s = val_ref[s] * jax.nn.sigmoid(gate_ref[s])
        for j in range(CH_TILES):
            utm[j, pl.ds(s, tt, stride=ns), :] = u_s[:, j * LANES:(j + 1) * LANES]
            vext[j, pl.ds(hist_rows + s, tt, stride=ns), :] = v_s[:, j * LANES:(j + 1) * LANES]

    for kb in range(CH_TILES):
        r = _dot(utm[kb].astype(BF16), bblk_ref[kb])
        re = r[:, :STATES_PER_CH_TILE]
        im = r[:, STATES_PER_CH_TILE:]
        st = slice(kb * STATES_PER_CH_TILE, (kb + 1) * STATES_PER_CH_TILE)
        cr = cfr_ref[:, st]
        ci = cfi_ref[:, st]
        x_re = cr * re - ci * im
        x_im = cr * im + ci * re
        for i in range(STATES_PER_CH_TILE // LANES):
            j = kb * (STATES_PER_CH_TILE // LANES) + i
            xr[j] = x_re[:, i * LANES:(i + 1) * LANES]
            xi[j] = x_im[:, i * LANES:(i + 1) * LANES]

    for i in range(ns // SUBLANES):
        srow = slice(i * SUBLANES, (i + 1) * SUBLANES)
        for jg in range(STATE_TILES // SCAN_TILE_GROUP):
            tiles = tuple(range(jg * SCAN_TILE_GROUP, (jg + 1) * SCAN_TILE_GROUP))
            a_r = [ar_ref[j] for j in tiles]
            a_i = [ai_ref[j] for j in tiles]
            init = tuple(hcr[j, srow, :] for j in tiles) + tuple(hci[j, srow, :] for j in tiles)

            def body(t, carry, tiles=tiles, a_r=a_r, a_i=a_i, i=i):
                row = pl.ds(pl.multiple_of(t * ns + i * SUBLANES, SUBLANES), SUBLANES)
                new_r, new_i = [], []
                for idx, j in enumerate(tiles):
                    h_r, h_i = carry[idx], carry[SCAN_TILE_GROUP + idx]
                    n_r = a_r[idx] * h_r - a_i[idx] * h_i + xr[j, row, :]
                    n_i = a_r[idx] * h_i + a_i[idx] * h_r + xi[j, row, :]
                    xr[j, row, :] = n_r
                    xi[j, row, :] = n_i
                    new_r.append(n_r)
                    new_i.append(n_i)
                return tuple(new_r) + tuple(new_i)

            fin = lax.fori_loop(0, tt, body, init, unroll=8)
            for idx, j in enumerate(tiles):
                hcr[j, srow, :] = fin[idx]
                hci[j, srow, :] = fin[SCAN_TILE_GROUP + idx]

    tiles_per_ch = STATES_PER_CH_TILE // LANES
    ys = []
    for ob in range(CH_TILES):
        h_re = jnp.concatenate([xr[ob * tiles_per_ch + i] for i in range(tiles_per_ch)], axis=1).astype(BF16)
        h_im = jnp.concatenate([xi[ob * tiles_per_ch + i] for i in range(tiles_per_ch)], axis=1).astype(BF16)
        ys.append(_dot(h_re, cre_ref[ob]) + _dot(h_im, cim_ref[ob]))
    u_all = jnp.concatenate([utm[j] for j in range(CH_TILES)], axis=1)
    y = jnp.concatenate(ys, axis=1) + d_ref[...] * u_all
    g = jax.nn.gelu(y)
    s_out = g * jax.nn.sigmoid(_dot(g.astype(BF16), wglu_ref[...]))

    first_tap = HIST - (CONV_WIDTH - 1)
    for j in range(CH_TILES):
        def conv_body(b, _, j=j):
            base = pl.multiple_of(b * CONV_ROWS, CONV_ROWS)
            acc = jnp.zeros((CONV_ROWS, LANES), F32)
            for k in range(CONV_WIDTH):
                acc = acc + cw_ref[j, k:k + 1, :] * vext[j, pl.ds(base + (first_tap + k) * ns, CONV_ROWS), :]
            ytm[j, pl.ds(base, CONV_ROWS), :] = acc
            return 0

        lax.fori_loop(0, rows // CONV_ROWS, conv_body, 0)
    yc = jnp.concatenate([ytm[j] for j in range(CH_TILES)], axis=1) + cb_ref[...]
    mu = jnp.mean(yc, axis=-1, keepdims=True)
    var = jnp.mean(jnp.square(yc - mu), axis=-1, keepdims=True)
    yn = (yc - mu) * lax.rsqrt(var + EPS) * lng_ref[...] + lnb_ref[...]
    c_out = yn * jax.nn.sigmoid(yn)

    mix = jnp.concatenate([s_out, c_out], axis=1).astype(BF16)
    delta = _dot(mix, wout_ref[...])
    for j in range(MODEL_TILES):
        dtm[j] = delta[:, j * LANES:(j + 1) * LANES]
    for s in range(ns):
        d_s = jnp.concatenate([dtm[j, pl.ds(s, tt, stride=ns), :] for j in range(MODEL_TILES)], axis=1)
        x2_s = x1_ref[s] + d_s
        x2_ref[s] = x2_s
        xn[s * tt:(s + 1) * tt, :] = _rms(x2_s, gx_ref[...])
    qv = _dot(xn[...].astype(BF16), wq_ref[...]).astype(BF16)
    for s in range(ns):
        q_ref[s] = qv[s * tt:(s + 1) * tt, :]

    hre_out[...] = jnp.concatenate([hcr[j] for j in range(STATE_TILES)], axis=1)
    him_out[...] = jnp.concatenate([hci[j] for j in range(STATE_TILES)], axis=1)

    def write_buf():
        for j in range(CH_TILES):
            buf_out[:, :, j * LANES:(j + 1) * LANES] = (
                vext[j, (tt + first_tap) * ns:(tt + HIST) * ns, :].reshape(CONV_WIDTH - 1, ns, LANES))

    if has_init:
        write_buf()
    else:
        pl.when(chunk == pl.num_programs(0) - 1)(write_buf)
        vext[:, :hist_rows, :] = vext[:, rows:rows + hist_rows, :]


def _mixer(z3, x13, init, w, *, ns, tt):
    nseq, seq_len, _ = z3.shape
    has_init = init is not None
    if has_init:
        assert tt == seq_len and nseq % ns == 0
        grid = (nseq // ns,)
        blk = lambda i: (i, 0, 0)
        col = lambda c: (lambda i: (i, 0, c))
        st_map = lambda i: (i, 0)
    else:
        assert ns == nseq and seq_len % tt == 0
        grid = (seq_len // tt,)
        blk = lambda i: (0, i, 0)
        col = lambda c: (lambda i: (0, i, c))
        st_map = lambda i: (0, 0)
    rows = ns * tt
    assert rows % CONV_ROWS == 0 and ns % SUBLANES == 0 and (ns == SUBLANES or tt == SUBLANES)

    in_specs = [pl.BlockSpec((ns, tt, SSM_DIM), col(c)) for c in range(3)]
    in_specs.append(pl.BlockSpec((ns, tt, D_MODEL), blk))
    args = [z3, z3, z3, x13]
    if has_init:
        in_specs += [pl.BlockSpec((ns, N_STATE), st_map), pl.BlockSpec((ns, N_STATE), st_map),
                     pl.BlockSpec((CONV_WIDTH - 1, ns, CONV_DIM), lambda i: (0, i, 0))]
        args += list(init)
    weights = [w["bblk"], w["cre"], w["cim"], w["coef_re"], w["coef_im"], w["ab_re"], w["ab_im"], w["d"],
               w["w_glu"], w["conv_w"], w["conv_b"], w["ln_g"], w["ln_b"], w["w_out"], w["g_x"], w["w_q"]]
    in_specs += [_const_spec(a.shape) for a in weights]
    args += weights

    out_specs = (pl.BlockSpec((ns, tt, D_MODEL), blk), pl.BlockSpec((ns, tt, D_MODEL), blk),
                 pl.BlockSpec((ns, N_STATE), st_map), pl.BlockSpec((ns, N_STATE), st_map),
                 pl.BlockSpec((CONV_WIDTH - 1, ns, CONV_DIM), lambda i: (0, i if has_init else 0, 0)))
    out_shape = (jax.ShapeDtypeStruct((nseq, seq_len, D_MODEL), F32),
                 jax.ShapeDtypeStruct((nseq, seq_len, D_MODEL), BF16),
                 jax.ShapeDtypeStruct((nseq, N_STATE), F32), jax.ShapeDtypeStruct((nseq, N_STATE), F32),
                 jax.ShapeDtypeStruct((CONV_WIDTH - 1, nseq, CONV_DIM), F32))
    scratch = [pltpu.VMEM((CH_TILES, rows, LANES), F32),
               pltpu.VMEM((STATE_TILES, rows, LANES), F32),
               pltpu.VMEM((STATE_TILES, rows, LANES), F32),
               pltpu.VMEM((STATE_TILES, ns, LANES), F32),
               pltpu.VMEM((STATE_TILES, ns, LANES), F32),
               pltpu.VMEM((CH_TILES, rows + HIST * ns, LANES), F32),
               pltpu.VMEM((CH_TILES, rows, LANES), F32),
               pltpu.VMEM((MODEL_TILES, rows, LANES), F32),
               pltpu.VMEM((rows, D_MODEL), F32)]
    return pl.pallas_call(
        functools.partial(_mixer_kernel, ns=ns, tt=tt, has_init=has_init),
        grid=grid,
        in_specs=in_specs,
        out_specs=out_specs,
        out_shape=out_shape,
        scratch_shapes=scratch,
        compiler_params=_params(("arbitrary",)),
        name="mixer_init" if has_init else "mixer_zero",
    )(*args)


def _block_diag_tiles(x):
    _, a, b = x.shape
    x4 = x.reshape(CH_TILES, GROUPS_PER_TILE, a, b)
    eye = jnp.eye(GROUPS_PER_TILE, dtype=x.dtype)
    out = x4[:, :, :, None, :] * eye[None, :, None, :, None]
    return out.reshape(CH_TILES, GROUPS_PER_TILE * a, GROUPS_PER_TILE * b)


def _row(x):
    return x.reshape(1, -1)


def _state_slabs(x):
    return jnp.broadcast_to(x.reshape(STATE_TILES, 1, LANES), (STATE_TILES, SUBLANES, LANES))


def kernel(x_prompt, x_sample, state_ssm_re, state_ssm_im, cache_conv, cache_mem_k, cache_mem_v, mem_prompt,
           g_mem, w_mem_k, w_mem_v, g_ffn1, w_ffn1_gate, w_ffn1_up, w_ffn1_down, g_mix, w_in,
           ssm_a_re, ssm_a_im, ssm_log_dt, ssm_b_re, ssm_b_im, ssm_c_re, ssm_c_im, ssm_d, w_ssm_glu,
           conv_w, conv_b, conv_ln_g, conv_ln_b, w_out, g_xattn, w_mem_q, w_mem_o,
           g_ffn2, w_ffn2_gate, w_ffn2_up, w_ffn2_down, g_final):
    depth = g_ffn1.shape[0]
    assert depth == 1
    l = 0
    bp, seq, _ = x_prompt.shape
    bs, dseq, _ = x_sample.shape
    bf = lambda a: a.astype(BF16)

    ab_re, ab_im, coef_re, coef_im = _s5_params(ssm_a_re[l], ssm_a_im[l], ssm_log_dt[l])
    b_re_t = _block_diag_tiles(jnp.swapaxes(ssm_b_re[l], 1, 2))
    b_im_t = _block_diag_tiles(jnp.swapaxes(ssm_b_im[l], 1, 2))
    mixer_w = dict(
        bblk=bf(jnp.concatenate([b_re_t, b_im_t], axis=-1)),
        cre=bf(_block_diag_tiles(jnp.swapaxes(ssm_c_re[l], 1, 2))),
        cim=bf(-_block_diag_tiles(jnp.swapaxes(ssm_c_im[l], 1, 2))),
        coef_re=_row(coef_re), coef_im=_row(coef_im),
        ab_re=_state_slabs(ab_re), ab_im=_state_slabs(ab_im),
        d=_row(ssm_d[l]), w_glu=bf(w_ssm_glu[l]),
        conv_w=jnp.swapaxes(conv_w[l].reshape(CONV_WIDTH, CH_TILES, LANES), 0, 1),
        conv_b=_row(conv_b[l]), ln_g=_row(conv_ln_g[l]), ln_b=_row(conv_ln_b[l]),
        w_out=bf(w_out[l]), g_x=_row(g_xattn[l]), w_q=bf(w_mem_q[l]),
    )
    ffn1 = (_row(g_ffn1[l]), bf(w_ffn1_gate[l]), bf(w_ffn1_up[l]), bf(w_ffn1_down[l]))
    ffn2 = (_row(g_ffn2[l]), bf(w_ffn2_gate[l]), bf(w_ffn2_up[l]), bf(w_ffn2_down[l]))
    g_mix_r, w_in_b, w_o_b, g_final_r = _row(g_mix[l]), bf(w_in[l]), bf(w_mem_o[l]), _row(g_final)

    mk, mv = _mem_kv(mem_prompt.reshape(bp * N_MEM, D_MODEL), _row(g_mem[l]), bf(w_mem_k[l]), bf(w_mem_v[l]))

    kv_heads = lambda a, n: a.reshape(n, N_MEM, MEM_HEADS, MEM_HEAD_DIM)

    def trunk(x3d, k4d, v4d, init, *, ns, tt, attn_nseq, attn_lq):
        nseq, slen, _ = x3d.shape
        x1, z = _ffn_in(x3d.reshape(nseq * slen, D_MODEL), *ffn1, g_mix_r, w_in_b)
        x2, q, h_re, h_im, buf = _mixer(z.reshape(nseq, slen, -1), x1.reshape(nseq, slen, D_MODEL), init,
                                        mixer_w, ns=ns, tt=tt)
        o = _attn(q.reshape(nseq * slen, D_MODEL), k4d, v4d, nseq=attn_nseq, lq=attn_lq)
        y = _out_ffn(x2.reshape(nseq * slen, D_MODEL), o, w_o_b, *ffn2, g_final_r)
        return y.reshape(nseq, slen, D_MODEL), h_re, h_im, buf

    yp, p_re, p_im, p_buf = trunk(x_prompt, kv_heads(mk, bp), kv_heads(mv, bp), None,
                                  ns=bp, tt=64, attn_nseq=1, attn_lq=512)
    init_s = (state_ssm_re[l].reshape(bs, N_STATE), state_ssm_im[l].reshape(bs, N_STATE),
              jnp.swapaxes(cache_conv[l], 0, 1))
    ys, s_re, s_im, s_buf = trunk(x_sample, kv_heads(cache_mem_k, bs), kv_heads(cache_mem_v, bs), init_s,
                                  ns=32, tt=dseq, attn_nseq=4, attn_lq=dseq)

    st = lambda a, n: a.reshape(1, n, SSM_GROUPS, SSM_STATE)
    kv = lambda a: a.reshape(1, bp, N_MEM, MEM_HEADS, MEM_HEAD_DIM)
    buf = lambda a: jnp.swapaxes(a, 0, 1)[None]
    return (yp, ys, st(p_re, bp), st(p_im, bp), buf(p_buf), kv(mk), kv(mv),
            st(s_re, bs), st(s_im, bs), buf(s_buf))
```

```python
import functools
import math

import jax
import jax.numpy as jnp
from jax import lax
from jax.experimental import pallas as pl
from jax.experimental.pallas import tpu as pltpu

F32 = jnp.float32
BF16 = jnp.bfloat16

EPS = 1e-6
D_MODEL = 1024
D_FF = 2816
SSM_DIM = 512
CONV_DIM = 512
SSM_GROUPS = 32
SSM_GROUP_CH = 16
SSM_STATE = 64
N_STATE = SSM_GROUPS * SSM_STATE
CONV_WIDTH = 31
N_MEM = 256
MEM_HEADS = 4
MEM_HEAD_DIM = 256

LANES = 128
SUBLANES = 8
STATE_TILES = N_STATE // LANES
CH_TILES = SSM_DIM // LANES
MODEL_TILES = D_MODEL // LANES
GROUPS_PER_TILE = LANES // SSM_GROUP_CH
STATES_PER_CH_TILE = GROUPS_PER_TILE * SSM_STATE
HIST = 32
SCAN_TILE_GROUP = 4
CONV_ROWS = 64
VMEM_LIMIT = 56 * 1024 * 1024


def _dot(a, b):
    return jnp.dot(a, b, preferred_element_type=F32)


def _rms(x, g):
    return x * lax.rsqrt(jnp.mean(x * x, axis=-1, keepdims=True) + EPS) * g


def _const_spec(shape):
    zeros = (0,) * len(shape)
    return pl.BlockSpec(shape, lambda *_: zeros, pipeline_mode=pl.Buffered(1))


def _params(sem):
    return pltpu.CompilerParams(dimension_semantics=sem, vmem_limit_bytes=VMEM_LIMIT)


def _s5_params_kernel(ar_ref, ai_ref, ldt_ref, abr_ref, abi_ref, cfr_ref, cfi_ref):
    lr = ar_ref[...]
    li = ai_ref[...]
    dt = jnp.exp(ldt_ref[...])
    mag = jnp.exp(lr * dt)
    ab_re = mag * jnp.cos(li * dt)
    ab_im = mag * jnp.sin(li * dt)
    den = lr * lr + li * li
    nr = ab_re - 1.0
    ni = ab_im
    abr_ref[...] = ab_re
    abi_ref[...] = ab_im
    cfr_ref[...] = (nr * lr + ni * li) / den
    cfi_ref[...] = (ni * lr - nr * li) / den


def _s5_params(a_re, a_im, log_dt):
    shp = jax.ShapeDtypeStruct((SSM_GROUPS, SSM_STATE), F32)
    return pl.pallas_call(
        _s5_params_kernel,
        out_shape=(shp, shp, shp, shp),
        name="s5_params",
    )(a_re, a_im, log_dt.reshape(SSM_GROUPS, 1))


def _mem_kv_kernel(m_ref, g_ref, wk_ref, wv_ref, k_ref, v_ref, kh_ref, vh_ref):
    m = _rms(m_ref[...], g_ref[...]).astype(BF16)
    k = _dot(m, wk_ref[...])
    v = _dot(m, wv_ref[...])
    for h in range(MEM_HEADS):
        cols = slice(h * MEM_HEAD_DIM, (h + 1) * MEM_HEAD_DIM)
        k_ref[:, h, :] = k[:, cols]
        v_ref[:, h, :] = v[:, cols]
        kh_ref[0, h] = k[:, cols].astype(BF16)
        vh_ref[0, h] = v[:, cols].astype(BF16)


def _mem_kv(mem2d, g, wk, wv):
    rows = mem2d.shape[0]
    nb = rows // N_MEM
    row_spec = pl.BlockSpec((N_MEM, D_MODEL), lambda i: (i, 0))
    head_spec = pl.BlockSpec((N_MEM, MEM_HEADS, MEM_HEAD_DIM), lambda i: (i, 0, 0))
    hm_spec = pl.BlockSpec((1, MEM_HEADS, N_MEM, MEM_HEAD_DIM), lambda i: (i, 0, 0, 0))
    out = jax.ShapeDtypeStruct((rows, MEM_HEADS, MEM_HEAD_DIM), F32)
    out_hm = jax.ShapeDtypeStruct((nb, MEM_HEADS, N_MEM, MEM_HEAD_DIM), BF16)
    return pl.pallas_call(
        _mem_kv_kernel,
        grid=(nb,),
        in_specs=[row_spec, _const_spec((1, D_MODEL)),
                  _const_spec((D_MODEL, D_MODEL)), _const_spec((D_MODEL, D_MODEL))],
        out_specs=(head_spec, head_spec, hm_spec, hm_spec),
        out_shape=(out, out, out_hm, out_hm),
        compiler_params=_params(("arbitrary",)),
        name="mem_kv",
    )(mem2d, g, wk, wv)


def _swiglu_half(x, g, wg_ref, wu_ref, wd_ref):
    hn = _rms(x, g).astype(BF16)
    gate = _dot(hn, wg_ref[...])
    up = _dot(hn, wu_ref[...])
    act = (gate * jax.nn.sigmoid(gate) * up).astype(BF16)
    return x + 0.5 * _dot(act, wd_ref[...])


def _ffn_in_kernel(x_ref, g1_ref, wg_ref, wu_ref, wd_ref, g2_ref, win_ref, x1_ref, z_ref):
    x1 = _swiglu_half(x_ref[...], g1_ref[...], wg_ref, wu_ref, wd_ref)
    x1_ref[...] = x1
    z_ref[...] = _dot(_rms(x1, g2_ref[...]).astype(BF16), win_ref[...])


def _ffn_in(x2d, g1, wg, wu, wd, g2, win, tm=512):
    rows = x2d.shape[0]
    zdim = win.shape[1]
    row_spec = pl.BlockSpec((tm, D_MODEL), lambda i: (i, 0))
    return pl.pallas_call(
        _ffn_in_kernel,
        grid=(rows // tm,),
        in_specs=[row_spec, _const_spec((1, D_MODEL)),
                  _const_spec((D_MODEL, D_FF)), _const_spec((D_MODEL, D_FF)), _const_spec((D_FF, D_MODEL)),
                  _const_spec((1, D_MODEL)), _const_spec((D_MODEL, zdim))],
        out_specs=(row_spec, pl.BlockSpec((tm, zdim), lambda i: (i, 0))),
        out_shape=(jax.ShapeDtypeStruct((rows, D_MODEL), F32), jax.ShapeDtypeStruct((rows, zdim), F32)),
        compiler_params=_params(("arbitrary",)),
        name="ffn_in",
    )(x2d, g1, wg, wu, wd, g2, win)


def _out_ffn_kernel(x2_ref, o_ref, wo_ref, g_ref, wg_ref, wu_ref, wd_ref, gf_ref, y_ref):
    x3 = x2_ref[...] + _dot(o_ref[...], wo_ref[...])
    x4 = _swiglu_half(x3, g_ref[...], wg_ref, wu_ref, wd_ref)
    y_ref[...] = _rms(x4, gf_ref[...])


def _out_ffn(x2, o, wo, g, wg, wu, wd, gf, tm=512):
    rows = x2.shape[0]
    row_spec = pl.BlockSpec((tm, D_MODEL), lambda i: (i, 0))
    return pl.pallas_call(
        _out_ffn_kernel,
        grid=(rows // tm,),
        in_specs=[row_spec, row_spec, _const_spec((D_MODEL, D_MODEL)), _const_spec((1, D_MODEL)),
                  _const_spec((D_MODEL, D_FF)), _const_spec((D_MODEL, D_FF)), _const_spec((D_FF, D_MODEL)),
                  _const_spec((1, D_MODEL))],
        out_specs=row_spec,
        out_shape=jax.ShapeDtypeStruct((rows, D_MODEL), F32),
        compiler_params=_params(("arbitrary",)),
        name="out_ffn",
    )(x2, o, wo, g, wg, wu, wd, gf)


def _softmax_rows(sc):
    e = jnp.exp(sc - jnp.max(sc, axis=-1, keepdims=True))
    return e / jnp.sum(e, axis=-1, keepdims=True)


def _attn_long_kernel(q_ref, k_ref, v_ref, o_ref):
    scale = MEM_HEAD_DIM ** -0.5
    for h in range(MEM_HEADS):
        cols = slice(h * MEM_HEAD_DIM, (h + 1) * MEM_HEAD_DIM)
        sc = lax.dot_general(q_ref[:, cols], k_ref[0, h], (((1,), (1,)), ((), ())),
                             preferred_element_type=F32) * scale
        o_ref[:, cols] = _dot(_softmax_rows(sc).astype(BF16), v_ref[0, h]).astype(BF16)


def _attn_long(q2d, k_hm, v_hm, *, lq):
    rows = q2d.shape[0]
    nb = k_hm.shape[0]
    q_per_seq = rows // nb // lq
    q_spec = pl.BlockSpec((lq, D_MODEL), lambda b, i: (b * q_per_seq + i, 0))
    kv_spec = pl.BlockSpec((1, MEM_HEADS, N_MEM, MEM_HEAD_DIM), lambda b, i: (b, 0, 0, 0))
    return pl.pallas_call(
        _attn_long_kernel,
        grid=(nb, q_per_seq),
        in_specs=[q_spec, kv_spec, kv_spec],
        out_specs=q_spec,
        out_shape=jax.ShapeDtypeStruct((rows, D_MODEL), BF16),
        compiler_params=_params(("arbitrary", "arbitrary")),
        name="attn_long",
    )(q2d, k_hm, v_hm)


def _attn_short_kernel(q_ref, k_ref, v_ref, o_ref, *, nseq, lq):
    scale = MEM_HEAD_DIM ** -0.5
    kv_rows = N_MEM * MEM_HEADS
    k2d = k_ref.reshape(nseq * kv_rows, MEM_HEAD_DIM)
    v2d = v_ref.reshape(nseq * kv_rows, MEM_HEAD_DIM)
    shape = (MEM_HEADS * lq, kv_rows)
    same_head = (lax.broadcasted_iota(jnp.int32, shape, 0) // lq
                 == lax.broadcasted_iota(jnp.int32, shape, 1) % MEM_HEADS)
    for s in range(nseq):
        rows = slice(s * lq, (s + 1) * lq)
        q = jnp.concatenate([q_ref[rows, h * MEM_HEAD_DIM:(h + 1) * MEM_HEAD_DIM] for h in range(MEM_HEADS)],
                            axis=0)
        k = k2d[s * kv_rows:(s + 1) * kv_rows, :].astype(BF16)
        v = v2d[s * kv_rows:(s + 1) * kv_rows, :].astype(BF16)
        sc = lax.dot_general(q, k, (((1,), (1,)), ((), ())), preferred_element_type=F32) * scale
        p = _softmax_rows(jnp.where(same_head, sc, -jnp.inf))
        o = _dot(p.astype(BF16), v).astype(BF16)
        for h in range(MEM_HEADS):
            o_ref[rows, h * MEM_HEAD_DIM:(h + 1) * MEM_HEAD_DIM] = o[h * lq:(h + 1) * lq, :]


def _attn_short(q2d, k3d, v3d, *, nseq):
    rows = q2d.shape[0]
    n_total = k3d.shape[0] // N_MEM
    lq = rows // n_total
    q_spec = pl.BlockSpec((nseq * lq, D_MODEL), lambda i: (i, 0))
    kv_spec = pl.BlockSpec((nseq * N_MEM, MEM_HEADS, MEM_HEAD_DIM), lambda i: (i, 0, 0))
    return pl.pallas_call(
        functools.partial(_attn_short_kernel, nseq=nseq, lq=lq),
        grid=(n_total // nseq,),
        in_specs=[q_spec, kv_spec, kv_spec],
        out_specs=q_spec,
        out_shape=jax.ShapeDtypeStruct((rows, D_MODEL), BF16),
        compiler_params=_params(("arbitrary",)),
        name="attn_short",
    )(q2d, k3d, v3d)


def _mixer_kernel(*refs, ns, tt, has_init):
    it = iter(refs)
    u_ref, val_ref, gate_ref, x1_ref = next(it), next(it), next(it), next(it)
    if has_init:
        h0r_ref, h0i_ref, cache_ref = next(it), next(it), next(it)
    (bblk_ref, cre_ref, cim_ref, cfr_ref, cfi_ref, ar_ref, ai_ref, d_ref, wglu_ref,
     cw_ref, cb_ref, lng_ref, lnb_ref, wout_ref, gx_ref, wq_ref) = [next(it) for _ in range(16)]
    x2_ref, q_ref, hre_out, him_out, buf_out = [next(it) for _ in range(5)]
    utm, xr, xi, hcr, hci, vext, ytm, dtm, xn = [next(it) for _ in range(9)]

    rows = ns * tt
    hist_rows = HIST * ns
    chunk = pl.program_id(0)

    if has_init:
        for j in range(STATE_TILES):
            hcr[j] = h0r_ref[:, j * LANES:(j + 1) * LANES]
            hci[j] = h0i_ref[:, j * LANES:(j + 1) * LANES]
        for j in range(CH_TILES):
            vext[j, (HIST - (CONV_WIDTH - 1)) * ns:hist_rows, :] = (
                cache_ref[:, :, j * LANES:(j + 1) * LANES].reshape((CONV_WIDTH - 1) * ns, LANES))
    else:
        @pl.when(chunk == 0)
        def _():
            hcr[...] = jnp.zeros_like(hcr)
            hci[...] = jnp.zeros_like(hci)
            vext[:, :hist_rows, :] = jnp.zeros((CH_TILES, hist_rows, LANES), F32)

    for s in range(ns):
        u_s = u_ref[s]
        v_s = val_ref[s] * jax.nn.sigmoid(gate_ref[s])
        for j in range(CH_TILES):
            utm[j, pl.ds(s, tt, stride=ns), :] = u_s[:, j * LANES:(j + 1) * LANES]
            vext[j, pl.ds(hist_rows + s, tt, stride=ns), :] = v_s[:, j * LANES:(j + 1) * LANES]

    for kb in range(CH_TILES):
        r = _dot(utm[kb].astype(BF16), bblk_ref[kb])
        re = r[:, :STATES_PER_CH_TILE]
        im = r[:, STATES_PER_CH_TILE:]
        st = slice(kb * STATES_PER_CH_TILE, (kb + 1) * STATES_PER_CH_TILE)
        cr = cfr_ref[:, st]
        ci = cfi_ref[:, st]
        x_re = cr * re - ci * im
        x_im = cr * im + ci * re
        for i in range(STATES_PER_CH_TILE // LANES):
            j = kb * (STATES_PER_CH_TILE // LANES) + i
            xr[j] = x_re[:, i * LANES:(i + 1) * LANES]
            xi[j] = x_im[:, i * LANES:(i + 1) * LANES]

    for i in range(ns // SUBLANES):
        srow = slice(i * SUBLANES, (i + 1) * SUBLANES)
        for jg in range(STATE_TILES // SCAN_TILE_GROUP):
            tiles = tuple(range(jg * SCAN_TILE_GROUP, (jg + 1) * SCAN_TILE_GROUP))
            a_r = [ar_ref[j] for j in tiles]
            a_i = [ai_ref[j] for j in tiles]
            init = tuple(hcr[j, srow, :] for j in tiles) + tuple(hci[j, srow, :] for j in tiles)

            def body(t, carry, tiles=tiles, a_r=a_r, a_i=a_i, i=i):
                row = pl.ds(pl.multiple_of(t * ns + i * SUBLANES, SUBLANES), SUBLANES)
                new_r, new_i = [], []
                for idx, j in enumerate(tiles):
                    h_r, h_i = carry[idx], carry[SCAN_TILE_GROUP + idx]
                    n_r = a_r[idx] * h_r - a_i[idx] * h_i + xr[j, row, :]
                    n_i = a_r[idx] * h_i + a_i[idx] * h_r + xi[j, row, :]
                    xr[j, row, :] = n_r
                    xi[j, row, :] = n_i
                    new_r.append(n_r)
                    new_i.append(n_i)
                return tuple(new_r) + tuple(new_i)

            fin = lax.fori_loop(0, tt, body, init, unroll=8)
            for idx, j in enumerate(tiles):
                hcr[j, srow, :] = fin[idx]
                hci[j, srow, :] = fin[SCAN_TILE_GROUP + idx]

    tiles_per_ch = STATES_PER_CH_TILE // LANES
    ys = []
    for ob in range(CH_TILES):
        h_re = jnp.concatenate([xr[ob * tiles_per_ch + i] for i in range(tiles_per_ch)], axis=1).astype(BF16)
        h_im = jnp.concatenate([xi[ob * tiles_per_ch + i] for i in range(tiles_per_ch)], axis=1).astype(BF16)
        ys.append(_dot(h_re, cre_ref[ob]) + _dot(h_im, cim_ref[ob]))
    u_all = jnp.concatenate([utm[j] for j in range(CH_TILES)], axis=1)
    y = jnp.concatenate(ys, axis=1) + d_ref[...] * u_all
    g = jax.nn.gelu(y)
    s_out = g * jax.nn.sigmoid(_dot(g.astype(BF16), wglu_ref[...]))

    first_tap = HIST - (CONV_WIDTH - 1)
    for j in range(CH_TILES):
        def conv_body(b, _, j=j):
            base = pl.multiple_of(b * CONV_ROWS, CONV_ROWS)
            acc = jnp.zeros((CONV_ROWS, LANES), F32)
            for k in range(CONV_WIDTH):
                acc = acc + cw_ref[j, k:k + 1, :] * vext[j, pl.ds(base + (first_tap + k) * ns, CONV_ROWS), :]
            ytm[j, pl.ds(base, CONV_ROWS), :] = acc
            return 0

        lax.fori_loop(0, rows // CONV_ROWS, conv_body, 0)
    yc = jnp.concatenate([ytm[j] for j in range(CH_TILES)], axis=1) + cb_ref[...]
    mu = jnp.mean(yc, axis=-1, keepdims=True)
    var = jnp.mean(jnp.square(yc - mu), axis=-1, keepdims=True)
    yn = (yc - mu) * lax.rsqrt(var + EPS) * lng_ref[...] + lnb_ref[...]
    c_out = yn * jax.nn.sigmoid(yn)

    mix = jnp.concatenate([s_out, c_out], axis=1).astype(BF16)
    delta = _dot(mix, wout_ref[...])
    for j in range(MODEL_TILES):
        dtm[j] = delta[:, j * LANES:(j + 1) * LANES]
    for s in range(ns):
        d_s = jnp.concatenate([dtm[j, pl.ds(s, tt, stride=ns), :] for j in range(MODEL_TILES)], axis=1)
        x2_s = x1_ref[s] + d_s
        x2_ref[s] = x2_s
        xn[s * tt:(s + 1) * tt, :] = _rms(x2_s, gx_ref[...])
    qv = _dot(xn[...].astype(BF16), wq_ref[...]).astype(BF16)
    for s in range(ns):
        q_ref[s] = qv[s * tt:(s + 1) * tt, :]

    hre_out[...] = jnp.concatenate([hcr[j] for j in range(STATE_TILES)], axis=1)
    him_out[...] = jnp.concatenate([hci[j] for j in range(STATE_TILES)], axis=1)

    def write_buf():
        for j in range(CH_TILES):
            buf_out[:, :, j * LANES:(j + 1) * LANES] = (
                vext[j, (tt + first_tap) * ns:(tt + HIST) * ns, :].reshape(CONV_WIDTH - 1, ns, LANES))

    if has_init:
        write_buf()
    else:
        pl.when(chunk == pl.num_programs(0) - 1)(write_buf)
        vext[:, :hist_rows, :] = vext[:, rows:rows + hist_rows, :]


def _mixer(z3, x13, init, w, *, ns, tt):
    nseq, seq_len, _ = z3.shape
    has_init = init is not None
    if has_init:
        assert tt == seq_len and nseq % ns == 0
        grid = (nseq // ns,)
        blk = lambda i: (i, 0, 0)
        col = lambda c: (lambda i: (i, 0, c))
        st_map = lambda i: (i, 0)
    else:
        assert ns == nseq and seq_len % tt == 0
        grid = (seq_len // tt,)
        blk = lambda i: (0, i, 0)
        col = lambda c: (lambda i: (0, i, c))
        st_map = lambda i: (0, 0)
    rows = ns * tt
    assert rows % CONV_ROWS == 0 and ns % SUBLANES == 0 and (ns == SUBLANES or tt == SUBLANES)

    in_specs = [pl.BlockSpec((ns, tt, SSM_DIM), col(c)) for c in range(3)]
    in_specs.append(pl.BlockSpec((ns, tt, D_MODEL), blk))
    args = [z3, z3, z3, x13]
    if has_init:
        in_specs += [pl.BlockSpec((ns, N_STATE), st_map), pl.BlockSpec((ns, N_STATE), st_map),
                     pl.BlockSpec((CONV_WIDTH - 1, ns, CONV_DIM), lambda i: (0, i, 0))]
        args += list(init)
    weights = [w["bblk"], w["cre"], w["cim"], w["coef_re"], w["coef_im"], w["ab_re"], w["ab_im"], w["d"],
               w["w_glu"], w["conv_w"], w["conv_b"], w["ln_g"], w["ln_b"], w["w_out"], w["g_x"], w["w_q"]]
    in_specs += [_const_spec(a.shape) for a in weights]
    args += weights

    out_specs = (pl.BlockSpec((ns, tt, D_MODEL), blk), pl.BlockSpec((ns, tt, D_MODEL), blk),
                 pl.BlockSpec((ns, N_STATE), st_map), pl.BlockSpec((ns, N_STATE), st_map),
                 pl.BlockSpec((CONV_WIDTH - 1, ns, CONV_DIM), lambda i: (0, i if has_init else 0, 0)))
    out_shape = (jax.ShapeDtypeStruct((nseq, seq_len, D_MODEL), F32),
                 jax.ShapeDtypeStruct((nseq, seq_len, D_MODEL), BF16),
                 jax.ShapeDtypeStruct((nseq, N_STATE), F32), jax.ShapeDtypeStruct((nseq, N_STATE), F32),
                 jax.ShapeDtypeStruct((CONV_WIDTH - 1, nseq, CONV_DIM), F32))
    scratch = [pltpu.VMEM((CH_TILES, rows, LANES), F32),
               pltpu.VMEM((STATE_TILES, rows, LANES), F32),
               pltpu.VMEM((STATE_TILES, rows, LANES), F32),
               pltpu.VMEM((STATE_TILES, ns, LANES), F32),
               pltpu.VMEM((STATE_TILES, ns, LANES), F32),
               pltpu.VMEM((CH_TILES, rows + HIST * ns, LANES), F32),
               pltpu.VMEM((CH_TILES, rows, LANES), F32),
               pltpu.VMEM((MODEL_TILES, rows, LANES), F32),
               pltpu.VMEM((rows, D_MODEL), F32)]
    return pl.pallas_call(
        functools.partial(_mixer_kernel, ns=ns, tt=tt, has_init=has_init),
        grid=grid,
        in_specs=in_specs,
        out_specs=out_specs,
        out_shape=out_shape,
        scratch_shapes=scratch,
        compiler_params=_params(("arbitrary",)),
        name="mixer_init" if has_init else "mixer_zero",
    )(*args)


def _block_diag_tiles(x):
    _, a, b = x.shape
    x4 = x.reshape(CH_TILES, GROUPS_PER_TILE, a, b)
    eye = jnp.eye(GROUPS_PER_TILE, dtype=x.dtype)
    out = x4[:, :, :, None, :] * eye[None, :, None, :, None]
    return out.reshape(CH_TILES, GROUPS_PER_TILE * a, GROUPS_PER_TILE * b)


def _row(x):
    return x.reshape(1, -1)


def _state_slabs(x):
    return jnp.broadcast_to(x.reshape(STATE_TILES, 1, LANES), (STATE_TILES, SUBLANES, LANES))


def kernel(x_prompt, x_sample, state_ssm_re, state_ssm_im, cache_conv, cache_mem_k, cache_mem_v, mem_prompt,
           g_mem, w_mem_k, w_mem_v, g_ffn1, w_ffn1_gate, w_ffn1_up, w_ffn1_down, g_mix, w_in,
           ssm_a_re, ssm_a_im, ssm_log_dt, ssm_b_re, ssm_b_im, ssm_c_re, ssm_c_im, ssm_d, w_ssm_glu,
           conv_w, conv_b, conv_ln_g, conv_ln_b, w_out, g_xattn, w_mem_q, w_mem_o,
           g_ffn2, w_ffn2_gate, w_ffn2_up, w_ffn2_down, g_final):
    depth = g_ffn1.shape[0]
    assert depth == 1
    l = 0
    bp, seq, _ = x_prompt.shape
    bs, dseq, _ = x_sample.shape
    bf = lambda a: a.astype(BF16)

    ab_re, ab_im, coef_re, coef_im = _s5_params(ssm_a_re[l], ssm_a_im[l], ssm_log_dt[l])
    b_re_t = _block_diag_tiles(jnp.swapaxes(ssm_b_re[l], 1, 2))
    b_im_t = _block_diag_tiles(jnp.swapaxes(ssm_b_im[l], 1, 2))
    mixer_w = dict(
        bblk=bf(jnp.concatenate([b_re_t, b_im_t], axis=-1)),
        cre=bf(_block_diag_tiles(jnp.swapaxes(ssm_c_re[l], 1, 2))),
        cim=bf(-_block_diag_tiles(jnp.swapaxes(ssm_c_im[l], 1, 2))),
        coef_re=_row(coef_re), coef_im=_row(coef_im),
        ab_re=_state_slabs(ab_re), ab_im=_state_slabs(ab_im),
        d=_row(ssm_d[l]), w_glu=bf(w_ssm_glu[l]),
        conv_w=jnp.swapaxes(conv_w[l].reshape(CONV_WIDTH, CH_TILES, LANES), 0, 1),
        conv_b=_row(conv_b[l]), ln_g=_row(conv_ln_g[l]), ln_b=_row(conv_ln_b[l]),
        w_out=bf(w_out[l]), g_x=_row(g_xattn[l]), w_q=bf(w_mem_q[l]),
    )
    ffn1 = (_row(g_ffn1[l]), bf(w_ffn1_gate[l]), bf(w_ffn1_up[l]), bf(w_ffn1_down[l]))
    ffn2 = (_row(g_ffn2[l]), bf(w_ffn2_gate[l]), bf(w_ffn2_up[l]), bf(w_ffn2_down[l]))
    g_mix_r, w_in_b, w_o_b, g_final_r = _row(g_mix[l]), bf(w_in[l]), bf(w_mem_o[l]), _row(g_final)

    mk, mv, mk_hm, mv_hm = _mem_kv(mem_prompt.reshape(bp * N_MEM, D_MODEL), _row(g_mem[l]),
                                   bf(w_mem_k[l]), bf(w_mem_v[l]))

    def trunk(x3d, attn, init, *, ns, tt):
        nseq, slen, _ = x3d.shape
        x1, z = _ffn_in(x3d.reshape(nseq * slen, D_MODEL), *ffn1, g_mix_r, w_in_b)
        x2, q, h_re, h_im, buf = _mixer(z.reshape(nseq, slen, -1), x1.reshape(nseq, slen, D_MODEL), init,
                                        mixer_w, ns=ns, tt=tt)
        o = attn(q.reshape(nseq * slen, D_MODEL))
        y = _out_ffn(x2.reshape(nseq * slen, D_MODEL), o, w_o_b, *ffn2, g_final_r)
        return y.reshape(nseq, slen, D_MODEL), h_re, h_im, buf

    yp, p_re, p_im, p_buf = trunk(x_prompt, functools.partial(_attn_long, k_hm=mk_hm, v_hm=mv_hm, lq=512), None,
                                  ns=bp, tt=64)
    init_s = (state_ssm_re[l].reshape(bs, N_STATE), state_ssm_im[l].reshape(bs, N_STATE),
              jnp.swapaxes(cache_conv[l], 0, 1))
    kv_rows = lambda a: a.reshape(bs * N_MEM, MEM_HEADS, MEM_HEAD_DIM)
    ys, s_re, s_im, s_buf = trunk(x_sample, functools.partial(_attn_short, k3d=kv_rows(cache_mem_k),
                                                              v3d=kv_rows(cache_mem_v), nseq=4), init_s,
                                  ns=32, tt=dseq)

    st = lambda a, n: a.reshape(1, n, SSM_GROUPS, SSM_STATE)
    kv = lambda a: a.reshape(1, bp, N_MEM, MEM_HEADS, MEM_HEAD_DIM)
    buf = lambda a: jnp.swapaxes(a, 0, 1)[None]
    return (yp, ys, st(p_re, bp), st(p_im, bp), buf(p_buf), kv(mk), kv(mv),
            st(s_re, bs), st(s_im, bs), buf(s_buf))
```

```python
import functools
import math

import jax
import jax.numpy as jnp
from jax import lax
from jax.experimental import pallas as pl
from jax.experimental.pallas import tpu as pltpu

F32 = jnp.float32
BF16 = jnp.bfloat16

EPS = 1e-6
D_MODEL = 1024
D_FF = 2816
SSM_DIM = 512
CONV_DIM = 512
SSM_GROUPS = 32
SSM_GROUP_CH = 16
SSM_STATE = 64
N_STATE = SSM_GROUPS * SSM_STATE
CONV_WIDTH = 31
N_MEM = 256
MEM_HEADS = 4
MEM_HEAD_DIM = 256

LANES = 128
SUBLANES = 8
STATE_TILES = N_STATE // LANES
CH_TILES = SSM_DIM // LANES
MODEL_TILES = D_MODEL // LANES
GROUPS_PER_TILE = LANES // SSM_GROUP_CH
STATES_PER_CH_TILE = GROUPS_PER_TILE * SSM_STATE
HIST = 32
SCAN_TILE_GROUP = 4
CONV_ROWS = 64
VMEM_LIMIT = 56 * 1024 * 1024


def _dot(a, b):
    return jnp.dot(a, b, preferred_element_type=F32)


def _rms(x, g):
    return x * lax.rsqrt(jnp.mean(x * x, axis=-1, keepdims=True) + EPS) * g


def _const_spec(shape):
    zeros = (0,) * len(shape)
    return pl.BlockSpec(shape, lambda *_: zeros, pipeline_mode=pl.Buffered(1))


def _params(sem):
    return pltpu.CompilerParams(dimension_semantics=sem, vmem_limit_bytes=VMEM_LIMIT)


def _s5_params_kernel(ar_ref, ai_ref, ldt_ref, abr_ref, abi_ref, cfr_ref, cfi_ref):
    lr = ar_ref[...]
    li = ai_ref[...]
    dt = jnp.exp(ldt_ref[...])
    mag = jnp.exp(lr * dt)
    ab_re = mag * jnp.cos(li * dt)
    ab_im = mag * jnp.sin(li * dt)
    den = lr * lr + li * li
    nr = ab_re - 1.0
    ni = ab_im
    abr_ref[...] = ab_re
    abi_ref[...] = ab_im
    cfr_ref[...] = (nr * lr + ni * li) / den
    cfi_ref[...] = (ni * lr - nr * li) / den


def _s5_params(a_re, a_im, log_dt):
    shp = jax.ShapeDtypeStruct((SSM_GROUPS, SSM_STATE), F32)
    return pl.pallas_call(
        _s5_params_kernel,
        out_shape=(shp, shp, shp, shp),
        name="s5_params",
    )(a_re, a_im, log_dt.reshape(SSM_GROUPS, 1))


def _mem_kv_kernel(m_ref, g_ref, wk_ref, wv_ref, k_ref, v_ref, kh_ref, vh_ref):
    m = _rms(m_ref[...], g_ref[...]).astype(BF16)
    k = _dot(m, wk_ref[...])
    v = _dot(m, wv_ref[...])
    for h in range(MEM_HEADS):
        cols = slice(h * MEM_HEAD_DIM, (h + 1) * MEM_HEAD_DIM)
        k_ref[:, h, :] = k[:, cols]
        v_ref[:, h, :] = v[:, cols]
        kh_ref[0, h] = k[:, cols].astype(BF16)
        vh_ref[0, h] = v[:, cols].astype(BF16)


def _mem_kv(mem2d, g, wk, wv):
    rows = mem2d.shape[0]
    nb = rows // N_MEM
    row_spec = pl.BlockSpec((N_MEM, D_MODEL), lambda i: (i, 0))
    head_spec = pl.BlockSpec((N_MEM, MEM_HEADS, MEM_HEAD_DIM), lambda i: (i, 0, 0))
    hm_spec = pl.BlockSpec((1, MEM_HEADS, N_MEM, MEM_HEAD_DIM), lambda i: (i, 0, 0, 0))
    out = jax.ShapeDtypeStruct((rows, MEM_HEADS, MEM_HEAD_DIM), F32)
    out_hm = jax.ShapeDtypeStruct((nb, MEM_HEADS, N_MEM, MEM_HEAD_DIM), BF16)
    return pl.pallas_call(
        _mem_kv_kernel,
        grid=(nb,),
        in_specs=[row_spec, _const_spec((1, D_MODEL)),
                  _const_spec((D_MODEL, D_MODEL)), _const_spec((D_MODEL, D_MODEL))],
        out_specs=(head_spec, head_spec, hm_spec, hm_spec),
        out_shape=(out, out, out_hm, out_hm),
        compiler_params=_params(("arbitrary",)),
        name="mem_kv",
    )(mem2d, g, wk, wv)


def _swiglu_half(x, g, wg_ref, wu_ref, wd_ref):
    hn = _rms(x, g).astype(BF16)
    gate = _dot(hn, wg_ref[...])
    up = _dot(hn, wu_ref[...])
    act = (gate * jax.nn.sigmoid(gate) * up).astype(BF16)
    return x + 0.5 * _dot(act, wd_ref[...])


def _ffn_in_kernel(x_ref, g1_ref, wg_ref, wu_ref, wd_ref, g2_ref, win_ref, x1_ref, z_ref):
    x1 = _swiglu_half(x_ref[...], g1_ref[...], wg_ref, wu_ref, wd_ref)
    x1_ref[...] = x1
    z_ref[...] = _dot(_rms(x1, g2_ref[...]).astype(BF16), win_ref[...])


def _ffn_in(x2d, g1, wg, wu, wd, g2, win, tm=512):
    rows = x2d.shape[0]
    zdim = win.shape[1]
    row_spec = pl.BlockSpec((tm, D_MODEL), lambda i: (i, 0))
    return pl.pallas_call(
        _ffn_in_kernel,
        grid=(rows // tm,),
        in_specs=[row_spec, _const_spec((1, D_MODEL)),
                  _const_spec((D_MODEL, D_FF)), _const_spec((D_MODEL, D_FF)), _const_spec((D_FF, D_MODEL)),
                  _const_spec((1, D_MODEL)), _const_spec((D_MODEL, zdim))],
        out_specs=(row_spec, pl.BlockSpec((tm, zdim), lambda i: (i, 0))),
        out_shape=(jax.ShapeDtypeStruct((rows, D_MODEL), F32), jax.ShapeDtypeStruct((rows, zdim), F32)),
        compiler_params=_params(("arbitrary",)),
        name="ffn_in",
    )(x2d, g1, wg, wu, wd, g2, win)


def _softmax_rows(sc):
    e = jnp.exp(sc - jnp.max(sc, axis=-1, keepdims=True))
    return e / jnp.sum(e, axis=-1, keepdims=True)


def _attn_long_scores(q_ref, k_ref):
    scale = MEM_HEAD_DIM ** -0.5
    return [lax.dot_general(q_ref[:, h * MEM_HEAD_DIM:(h + 1) * MEM_HEAD_DIM], k_ref[0, h],
                            (((1,), (1,)), ((), ())), preferred_element_type=F32) * scale
            for h in range(MEM_HEADS)]


def _attn_long_values(scores, v_ref):
    return jnp.concatenate([_dot(_softmax_rows(sc).astype(BF16), v_ref[0, h]).astype(BF16)
                            for h, sc in enumerate(scores)], axis=1)


def _attn_short_scores(q_ref, k_ref, *, nseq, lq):
    scale = MEM_HEAD_DIM ** -0.5
    kv_rows = N_MEM * MEM_HEADS
    k2d = k_ref.reshape(nseq * kv_rows, MEM_HEAD_DIM)
    shape = (MEM_HEADS * lq, kv_rows)
    same_head = (lax.broadcasted_iota(jnp.int32, shape, 0) // lq
                 == lax.broadcasted_iota(jnp.int32, shape, 1) % MEM_HEADS)
    scores = []
    for s in range(nseq):
        rows = slice(s * lq, (s + 1) * lq)
        q = jnp.concatenate([q_ref[rows, h * MEM_HEAD_DIM:(h + 1) * MEM_HEAD_DIM] for h in range(MEM_HEADS)],
                            axis=0)
        k = k2d[s * kv_rows:(s + 1) * kv_rows, :].astype(BF16)
        sc = lax.dot_general(q, k, (((1,), (1,)), ((), ())), preferred_element_type=F32) * scale
        scores.append(jnp.where(same_head, sc, -jnp.inf))
    return scores


def _attn_short_values(scores, v_ref, *, lq):
    kv_rows = N_MEM * MEM_HEADS
    v2d = v_ref.reshape(len(scores) * kv_rows, MEM_HEAD_DIM)
    out = []
    for s, sc in enumerate(scores):
        v = v2d[s * kv_rows:(s + 1) * kv_rows, :].astype(BF16)
        o = _dot(_softmax_rows(sc).astype(BF16), v)
        out.append(jnp.concatenate([o[h * lq:(h + 1) * lq, :] for h in range(MEM_HEADS)], axis=1))
    return jnp.concatenate(out, axis=0)


def _tail_kernel(x2a_ref, qa_ref, ka_ref, va_ref, x2b_ref, qb_ref, kb_ref, vb_ref,
                 wo_ref, g_ref, wg_ref, wu_ref, wd_ref, gf_ref, ya_ref, yb_ref, oa_scr, ob_scr,
                 *, a_steps, tm, nseq, lq):
    j = pl.program_id(0)
    in_b = j > a_steps

    @pl.when(j == 0)
    def _():
        oa_scr[...] = jnp.zeros_like(oa_scr)
        ob_scr[...] = jnp.zeros_like(ob_scr)

    b_tile = pl.ds(pl.multiple_of(jnp.maximum(j - a_steps - 1, 0) * tm, tm), tm)
    o = jnp.where(in_b, ob_scr[b_tile, :], oa_scr[...])
    x2 = jnp.where(in_b, x2b_ref[...], x2a_ref[...])

    sc_a = _attn_long_scores(qa_ref, ka_ref)
    sc_b = _attn_short_scores(qb_ref, kb_ref, nseq=nseq, lq=lq)

    x3 = x2 + _dot(o, wo_ref[...])
    hn = _rms(x3, g_ref[...]).astype(BF16)
    gate = _dot(hn, wg_ref[...])
    up = _dot(hn, wu_ref[...])
    act = (gate * jax.nn.sigmoid(gate) * up).astype(BF16)
    x4 = x3 + 0.5 * _dot(act, wd_ref[...])

    b_rows = nseq * lq
    blk = jnp.minimum(j, a_steps - 1)
    ob_scr[pl.ds(pl.multiple_of(blk * b_rows, b_rows), b_rows), :] = (
        _attn_short_values(sc_b, vb_ref, lq=lq).astype(BF16))
    oa_scr[...] = _attn_long_values(sc_a, va_ref)

    y = _rms(x4, gf_ref[...])

    @pl.when(jnp.logical_and(j >= 1, j <= a_steps))
    def _():
        ya_ref[...] = y

    @pl.when(in_b)
    def _():
        yb_ref[...] = y


def _tail(x2a, qa, ka_hm, va_hm, x2b, qb, kb3d, vb3d, wo, g, wg, wu, wd, gf, *, tm):
    a_rows, b_rows = x2a.shape[0], x2b.shape[0]
    a_steps, b_steps = a_rows // tm, b_rows // tm
    n_b = kb3d.shape[0] // N_MEM
    lq = b_rows // n_b
    assert n_b % a_steps == 0
    nseq = n_b // a_steps
    tiles_per_seq = a_rows // ka_hm.shape[0] // tm
    attn_blk = lambda j: jnp.minimum(j, a_steps - 1)
    a_fin = lambda j: (jnp.clip(j - 1, 0, a_steps - 1), 0)
    b_fin = lambda j: (jnp.maximum(j - a_steps - 1, 0), 0)
    row = lambda m: pl.BlockSpec((tm, D_MODEL), m)
    ka_spec = pl.BlockSpec((1, MEM_HEADS, N_MEM, MEM_HEAD_DIM),
                           lambda j: (attn_blk(j) // tiles_per_seq, 0, 0, 0))
    kb_spec = pl.BlockSpec((nseq * N_MEM, MEM_HEADS, MEM_HEAD_DIM), lambda j: (attn_blk(j), 0, 0))
    return pl.pallas_call(
        functools.partial(_tail_kernel, a_steps=a_steps, tm=tm, nseq=nseq, lq=lq),
        grid=(a_steps + 1 + b_steps,),
        in_specs=[row(a_fin), row(lambda j: (attn_blk(j), 0)), ka_spec, ka_spec,
                  row(b_fin), pl.BlockSpec((nseq * lq, D_MODEL), lambda j: (attn_blk(j), 0)), kb_spec, kb_spec,
                  _const_spec((D_MODEL, D_MODEL)), _const_spec((1, D_MODEL)),
                  _const_spec((D_MODEL, D_FF)), _const_spec((D_MODEL, D_FF)), _const_spec((D_FF, D_MODEL)),
                  _const_spec((1, D_MODEL))],
        out_specs=(row(a_fin), row(b_fin)),
        out_shape=(jax.ShapeDtypeStruct((a_rows, D_MODEL), F32), jax.ShapeDtypeStruct((b_rows, D_MODEL), F32)),
        scratch_shapes=[pltpu.VMEM((tm, D_MODEL), BF16), pltpu.VMEM((b_rows, D_MODEL), BF16)],
        compiler_params=_params(("arbitrary",)),
        name="tail",
    )(x2a, qa, ka_hm, va_hm, x2b, qb, kb3d, vb3d, wo, g, wg, wu, wd, gf)


def _mixer_kernel(*refs, ns, tt, has_init):
    it = iter(refs)
    u_ref, val_ref, gate_ref, x1_ref = next(it), next(it), next(it), next(it)
    if has_init:
        h0r_ref, h0i_ref, cache_ref = next(it), next(it), next(it)
    (bblk_ref, cre_ref, cim_ref, cfr_ref, cfi_ref, ar_ref, ai_ref, d_ref, wglu_ref,
     cw_ref, cb_ref, lng_ref, lnb_ref, wout_ref, gx_ref, wq_ref) = [next(it) for _ in range(16)]
    x2_ref, q_ref, hre_out, him_out, buf_out = [next(it) for _ in range(5)]
    utm, xr, xi, hcr, hci, vext, ytm, dtm, xn = [next(it) for _ in range(9)]

    rows = ns * tt
    hist_rows = HIST * ns
    chunk = pl.program_id(0)

    if has_init:
        for j in range(STATE_TILES):
            hcr[j] = h0r_ref[:, j * LANES:(j + 1) * LANES]
            hci[j] = h0i_ref[:, j * LANES:(j + 1) * LANES]
        for j in range(CH_TILES):
            vext[j, (HIST - (CONV_WIDTH - 1)) * ns:hist_rows, :] = (
                cache_ref[:, :, j * LANES:(j + 1) * LANES].reshape((CONV_WIDTH - 1) * ns, LANES))
    else:
        @pl.when(chunk == 0)
        def _():
            hcr[...] = jnp.zeros_like(hcr)
            hci[...] = jnp.zeros_like(hci)
            vext[:, :hist_rows, :] = jnp.zeros((CH_TILES, hist_rows, LANES), F32)

    for s in range(ns):
        u_s = u_ref[s]
        v_s = val_ref[s] * jax.nn.sigmoid(gate_ref[s])
        for j in range(CH_TILES):
            utm[j, pl.ds(s, tt, stride=ns), :] = u_s[:, j * LANES:(j + 1) * LANES]
            vext[j, pl.ds(hist_rows + s, tt, stride=ns), :] = v_s[:, j * LANES:(j + 1) * LANES]

    for kb in range(CH_TILES):
        r = _dot(utm[kb].astype(BF16), bblk_ref[kb])
        re = r[:, :STATES_PER_CH_TILE]
        im = r[:, STATES_PER_CH_TILE:]
        st = slice(kb * STATES_PER_CH_TILE, (kb + 1) * STATES_PER_CH_TILE)
        cr = cfr_ref[:, st]
        ci = cfi_ref[:, st]
        x_re = cr * re - ci * im
        x_im = cr * im + ci * re
        for i in range(STATES_PER_CH_TILE // LANES):
            j = kb * (STATES_PER_CH_TILE // LANES) + i
            xr[j] = x_re[:, i * LANES:(i + 1) * LANES]
            xi[j] = x_im[:, i * LANES:(i + 1) * LANES]

    for i in range(ns // SUBLANES):
        srow = slice(i * SUBLANES, (i + 1) * SUBLANES)
        for jg in range(STATE_TILES // SCAN_TILE_GROUP):
            tiles = tuple(range(jg * SCAN_TILE_GROUP, (jg + 1) * SCAN_TILE_GROUP))
            a_r = [ar_ref[j] for j in tiles]
            a_i = [ai_ref[j] for j in tiles]
            init = tuple(hcr[j, srow, :] for j in tiles) + tuple(hci[j, srow, :] for j in tiles)

            def body(t, carry, tiles=tiles, a_r=a_r, a_i=a_i, i=i):
                row = pl.ds(pl.multiple_of(t * ns + i * SUBLANES, SUBLANES), SUBLANES)
                new_r, new_i = [], []
                for idx, j in enumerate(tiles):
                    h_r, h_i = carry[idx], carry[SCAN_TILE_GROUP + idx]
                    n_r = a_r[idx] * h_r - a_i[idx] * h_i + xr[j, row, :]
                    n_i = a_r[idx] * h_i + a_i[idx] * h_r + xi[j, row, :]
                    xr[j, row, :] = n_r
                    xi[j, row, :] = n_i
                    new_r.append(n_r)
                    new_i.append(n_i)
                return tuple(new_r) + tuple(new_i)

            fin = lax.fori_loop(0, tt, body, init, unroll=8)
            for idx, j in enumerate(tiles):
                hcr[j, srow, :] = fin[idx]
                hci[j, srow, :] = fin[SCAN_TILE_GROUP + idx]

    tiles_per_ch = STATES_PER_CH_TILE // LANES
    ys = []
    for ob in range(CH_TILES):
        h_re = jnp.concatenate([xr[ob * tiles_per_ch + i] for i in range(tiles_per_ch)], axis=1).astype(BF16)
        h_im = jnp.concatenate([xi[ob * tiles_per_ch + i] for i in range(tiles_per_ch)], axis=1).astype(BF16)
        ys.append(_dot(h_re, cre_ref[ob]) + _dot(h_im, cim_ref[ob]))
    u_all = jnp.concatenate([utm[j] for j in range(CH_TILES)], axis=1)
    y = jnp.concatenate(ys, axis=1) + d_ref[...] * u_all
    g = jax.nn.gelu(y)
    s_out = g * jax.nn.sigmoid(_dot(g.astype(BF16), wglu_ref[...]))

    first_tap = HIST - (CONV_WIDTH - 1)
    for j in range(CH_TILES):
        def conv_body(b, _, j=j):
            base = pl.multiple_of(b * CONV_ROWS, CONV_ROWS)
            acc = jnp.zeros((CONV_ROWS, LANES), F32)
            for k in range(CONV_WIDTH):
                acc = acc + cw_ref[j, k:k + 1, :] * vext[j, pl.ds(base + (first_tap + k) * ns, CONV_ROWS), :]
            ytm[j, pl.ds(base, CONV_ROWS), :] = acc
            return 0

        lax.fori_loop(0, rows // CONV_ROWS, conv_body, 0)
    yc = jnp.concatenate([ytm[j] for j in range(CH_TILES)], axis=1) + cb_ref[...]
    mu = jnp.mean(yc, axis=-1, keepdims=True)
    var = jnp.mean(jnp.square(yc - mu), axis=-1, keepdims=True)
    yn = (yc - mu) * lax.rsqrt(var + EPS) * lng_ref[...] + lnb_ref[...]
    c_out = yn * jax.nn.sigmoid(yn)

    mix = jnp.concatenate([s_out, c_out], axis=1).astype(BF16)
    delta = _dot(mix, wout_ref[...])
    for j in range(MODEL_TILES):
        dtm[j] = delta[:, j * LANES:(j + 1) * LANES]
    for s in range(ns):
        d_s = jnp.concatenate([dtm[j, pl.ds(s, tt, stride=ns), :] for j in range(MODEL_TILES)], axis=1)
        x2_s = x1_ref[s] + d_s
        x2_ref[s] = x2_s
        xn[s * tt:(s + 1) * tt, :] = _rms(x2_s, gx_ref[...])
    qv = _dot(xn[...].astype(BF16), wq_ref[...]).astype(BF16)
    for s in range(ns):
        q_ref[s] = qv[s * tt:(s + 1) * tt, :]

    hre_out[...] = jnp.concatenate([hcr[j] for j in range(STATE_TILES)], axis=1)
    him_out[...] = jnp.concatenate([hci[j] for j in range(STATE_TILES)], axis=1)

    def write_buf():
        for j in range(CH_TILES):
            buf_out[:, :, j * LANES:(j + 1) * LANES] = (
                vext[j, (tt + first_tap) * ns:(tt + HIST) * ns, :].reshape(CONV_WIDTH - 1, ns, LANES))

    if has_init:
        write_buf()
    else:
        pl.when(chunk == pl.num_programs(0) - 1)(write_buf)
        vext[:, :hist_rows, :] = vext[:, rows:rows + hist_rows, :]


def _mixer(z3, x13, init, w, *, ns, tt):
    nseq, seq_len, _ = z3.shape
    has_init = init is not None
    if has_init:
        assert tt == seq_len and nseq % ns == 0
        grid = (nseq // ns,)
        blk = lambda i: (i, 0, 0)
        col = lambda c: (lambda i: (i, 0, c))
        st_map = lambda i: (i, 0)
    else:
        assert ns == nseq and seq_len % tt == 0
        grid = (seq_len // tt,)
        blk = lambda i: (0, i, 0)
        col = lambda c: (lambda i: (0, i, c))
        st_map = lambda i: (0, 0)
    rows = ns * tt
    assert rows % CONV_ROWS == 0 and ns % SUBLANES == 0 and (ns == SUBLANES or tt == SUBLANES)

    in_specs = [pl.BlockSpec((ns, tt, SSM_DIM), col(c)) for c in range(3)]
    in_specs.append(pl.BlockSpec((ns, tt, D_MODEL), blk))
    args = [z3, z3, z3, x13]
    if has_init:
        in_specs += [pl.BlockSpec((ns, N_STATE), st_map), pl.BlockSpec((ns, N_STATE), st_map),
                     pl.BlockSpec((CONV_WIDTH - 1, ns, CONV_DIM), lambda i: (0, i, 0))]
        args += list(init)
    weights = [w["bblk"], w["cre"], w["cim"], w["coef_re"], w["coef_im"], w["ab_re"], w["ab_im"], w["d"],
               w["w_glu"], w["conv_w"], w["conv_b"], w["ln_g"], w["ln_b"], w["w_out"], w["g_x"], w["w_q"]]
    in_specs += [_const_spec(a.shape) for a in weights]
    args += weights

    out_specs = (pl.BlockSpec((ns, tt, D_MODEL), blk), pl.BlockSpec((ns, tt, D_MODEL), blk),
                 pl.BlockSpec((ns, N_STATE), st_map), pl.BlockSpec((ns, N_STATE), st_map),
                 pl.BlockSpec((CONV_WIDTH - 1, ns, CONV_DIM), lambda i: (0, i if has_init else 0, 0)))
    out_shape = (jax.ShapeDtypeStruct((nseq, seq_len, D_MODEL), F32),
                 jax.ShapeDtypeStruct((nseq, seq_len, D_MODEL), BF16),
                 jax.ShapeDtypeStruct((nseq, N_STATE), F32), jax.ShapeDtypeStruct((nseq, N_STATE), F32),
                 jax.ShapeDtypeStruct((CONV_WIDTH - 1, nseq, CONV_DIM), F32))
    scratch = [pltpu.VMEM((CH_TILES, rows, LANES), F32),
               pltpu.VMEM((STATE_TILES, rows, LANES), F32),
               pltpu.VMEM((STATE_TILES, rows, LANES), F32),
               pltpu.VMEM((STATE_TILES, ns, LANES), F32),
               pltpu.VMEM((STATE_TILES, ns, LANES), F32),
               pltpu.VMEM((CH_TILES, rows + HIST * ns, LANES), F32),
               pltpu.VMEM((CH_TILES, rows, LANES), F32),
               pltpu.VMEM((MODEL_TILES, rows, LANES), F32),
               pltpu.VMEM((rows, D_MODEL), F32)]
    return pl.pallas_call(
        functools.partial(_mixer_kernel, ns=ns, tt=tt, has_init=has_init),
        grid=grid,
        in_specs=in_specs,
        out_specs=out_specs,
        out_shape=out_shape,
        scratch_shapes=scratch,
        compiler_params=_params(("arbitrary",)),
        name="mixer_init" if has_init else "mixer_zero",
    )(*args)


def _block_diag_tiles(x):
    _, a, b = x.shape
    x4 = x.reshape(CH_TILES, GROUPS_PER_TILE, a, b)
    eye = jnp.eye(GROUPS_PER_TILE, dtype=x.dtype)
    out = x4[:, :, :, None, :] * eye[None, :, None, :, None]
    return out.reshape(CH_TILES, GROUPS_PER_TILE * a, GROUPS_PER_TILE * b)


def _row(x):
    return x.reshape(1, -1)


def _state_slabs(x):
    return jnp.broadcast_to(x.reshape(STATE_TILES, 1, LANES), (STATE_TILES, SUBLANES, LANES))


def kernel(x_prompt, x_sample, state_ssm_re, state_ssm_im, cache_conv, cache_mem_k, cache_mem_v, mem_prompt,
           g_mem, w_mem_k, w_mem_v, g_ffn1, w_ffn1_gate, w_ffn1_up, w_ffn1_down, g_mix, w_in,
           ssm_a_re, ssm_a_im, ssm_log_dt, ssm_b_re, ssm_b_im, ssm_c_re, ssm_c_im, ssm_d, w_ssm_glu,
           conv_w, conv_b, conv_ln_g, conv_ln_b, w_out, g_xattn, w_mem_q, w_mem_o,
           g_ffn2, w_ffn2_gate, w_ffn2_up, w_ffn2_down, g_final):
    depth = g_ffn1.shape[0]
    assert depth == 1
    l = 0
    bp, seq, _ = x_prompt.shape
    bs, dseq, _ = x_sample.shape
    bf = lambda a: a.astype(BF16)

    ab_re, ab_im, coef_re, coef_im = _s5_params(ssm_a_re[l], ssm_a_im[l], ssm_log_dt[l])
    b_re_t = _block_diag_tiles(jnp.swapaxes(ssm_b_re[l], 1, 2))
    b_im_t = _block_diag_tiles(jnp.swapaxes(ssm_b_im[l], 1, 2))
    mixer_w = dict(
        bblk=bf(jnp.concatenate([b_re_t, b_im_t], axis=-1)),
        cre=bf(_block_diag_tiles(jnp.swapaxes(ssm_c_re[l], 1, 2))),
        cim=bf(-_block_diag_tiles(jnp.swapaxes(ssm_c_im[l], 1, 2))),
        coef_re=_row(coef_re), coef_im=_row(coef_im),
        ab_re=_state_slabs(ab_re), ab_im=_state_slabs(ab_im),
        d=_row(ssm_d[l]), w_glu=bf(w_ssm_glu[l]),
        conv_w=jnp.swapaxes(conv_w[l].reshape(CONV_WIDTH, CH_TILES, LANES), 0, 1),
        conv_b=_row(conv_b[l]), ln_g=_row(conv_ln_g[l]), ln_b=_row(conv_ln_b[l]),
        w_out=bf(w_out[l]), g_x=_row(g_xattn[l]), w_q=bf(w_mem_q[l]),
    )
    ffn1 = (_row(g_ffn1[l]), bf(w_ffn1_gate[l]), bf(w_ffn1_up[l]), bf(w_ffn1_down[l]))
    ffn2 = (_row(g_ffn2[l]), bf(w_ffn2_gate[l]), bf(w_ffn2_up[l]), bf(w_ffn2_down[l]))
    g_mix_r, w_in_b, w_o_b, g_final_r = _row(g_mix[l]), bf(w_in[l]), bf(w_mem_o[l]), _row(g_final)

    mk, mv, mk_hm, mv_hm = _mem_kv(mem_prompt.reshape(bp * N_MEM, D_MODEL), _row(g_mem[l]),
                                   bf(w_mem_k[l]), bf(w_mem_v[l]))

    p_rows, s_rows = bp * seq, bs * dseq
    x1p, zp = _ffn_in(x_prompt.reshape(p_rows, D_MODEL), *ffn1, g_mix_r, w_in_b)
    x1s, zs = _ffn_in(x_sample.reshape(s_rows, D_MODEL), *ffn1, g_mix_r, w_in_b)
    x2p, qp, p_re, p_im, p_buf = _mixer(zp.reshape(bp, seq, -1), x1p.reshape(bp, seq, D_MODEL), None,
                                        mixer_w, ns=bp, tt=64)
    init_s = (state_ssm_re[l].reshape(bs, N_STATE), state_ssm_im[l].reshape(bs, N_STATE),
              jnp.swapaxes(cache_conv[l], 0, 1))
    x2s, qs, s_re, s_im, s_buf = _mixer(zs.reshape(bs, dseq, -1), x1s.reshape(bs, dseq, D_MODEL), init_s,
                                        mixer_w, ns=32, tt=dseq)
    kv_rows = lambda a: a.reshape(bs * N_MEM, MEM_HEADS, MEM_HEAD_DIM)
    yp, ys = _tail(x2p.reshape(p_rows, D_MODEL), qp.reshape(p_rows, D_MODEL), mk_hm, mv_hm,
                   x2s.reshape(s_rows, D_MODEL), qs.reshape(s_rows, D_MODEL),
                   kv_rows(cache_mem_k), kv_rows(cache_mem_v), w_o_b, *ffn2, g_final_r, tm=256)
    yp = yp.reshape(bp, seq, D_MODEL)
    ys = ys.reshape(bs, dseq, D_MODEL)

    st = lambda a, n: a.reshape(1, n, SSM_GROUPS, SSM_STATE)
    kv = lambda a: a.reshape(1, bp, N_MEM, MEM_HEADS, MEM_HEAD_DIM)
    buf = lambda a: jnp.swapaxes(a, 0, 1)[None]
    return (yp, ys, st(p_re, bp), st(p_im, bp), buf(p_buf), kv(mk), kv(mv),
            st(s_re, bs), st(s_im, bs), buf(s_buf))
```

```python
import functools

import jax
import jax.numpy as jnp
from jax import lax
from jax.experimental import pallas as pl
from jax.experimental.pallas import tpu as pltpu

F32 = jnp.float32
BF16 = jnp.bfloat16

EPS = 1e-6
D_MODEL = 1024
D_FF = 2816
SSM_DIM = 512
CONV_DIM = 512
SSM_GROUPS = 32
SSM_GROUP_CH = 16
SSM_STATE = 64
N_STATE = SSM_GROUPS * SSM_STATE
CONV_WIDTH = 31
N_MEM = 256
MEM_HEADS = 4
MEM_HEAD_DIM = 256

LANES = 128
SUBLANES = 8
STATE_TILES = N_STATE // LANES
CH_TILES = SSM_DIM // LANES
MODEL_TILES = D_MODEL // LANES
GROUPS_PER_TILE = LANES // SSM_GROUP_CH
STATES_PER_CH_TILE = GROUPS_PER_TILE * SSM_STATE
TILES_PER_CH = STATES_PER_CH_TILE // LANES
HIST = 32
FIRST_TAP = HIST - (CONV_WIDTH - 1)
SCAN_TILE_GROUP = 4
CONV_ROWS = 64
VMEM_LIMIT = 56 * 1024 * 1024


def _dot(a, b):
    return jnp.dot(a, b, preferred_element_type=F32)


def _rms(x, g):
    return x * lax.rsqrt(jnp.mean(x * x, axis=-1, keepdims=True) + EPS) * g


def _lane_tile(j):
    return slice(j * LANES, (j + 1) * LANES)


def _const_spec(shape):
    zeros = (0,) * len(shape)
    return pl.BlockSpec(shape, lambda *_: zeros, pipeline_mode=pl.Buffered(1))


def _params(sem):
    return pltpu.CompilerParams(dimension_semantics=sem, vmem_limit_bytes=VMEM_LIMIT)


def _s5_params_kernel(ar_ref, ai_ref, ldt_ref, bre_ref, bim_ref, abr_ref, abi_ref, bbr_ref, bbi_ref):
    lr = ar_ref[...]
    li = ai_ref[...]
    dt = jnp.exp(ldt_ref[...])
    mag = jnp.exp(lr * dt)
    ab_re = mag * jnp.cos(li * dt)
    ab_im = mag * jnp.sin(li * dt)
    den = lr * lr + li * li
    nr = ab_re - 1.0
    ni = ab_im
    coef_re = (nr * lr + ni * li) / den
    coef_im = (ni * lr - nr * li) / den
    abr_ref[...] = ab_re
    abi_ref[...] = ab_im
    b_re = bre_ref[...]
    b_im = bim_ref[...]
    bbr_ref[...] = coef_re * b_re - coef_im * b_im
    bbi_ref[...] = coef_re * b_im + coef_im * b_re


def _s5_params(a_re, a_im, log_dt, b_re, b_im):
    a_shape = jax.ShapeDtypeStruct((SSM_GROUPS, 1, SSM_STATE), F32)
    b_shape = jax.ShapeDtypeStruct(b_re.shape, F32)
    return pl.pallas_call(
        _s5_params_kernel,
        out_shape=(a_shape, a_shape, b_shape, b_shape),
        name="s5_params",
    )(a_re.reshape(a_shape.shape), a_im.reshape(a_shape.shape), log_dt.reshape(SSM_GROUPS, 1, 1), b_re, b_im)


def _mem_kv_kernel(m_ref, g_ref, wk_ref, wv_ref, k_ref, v_ref, kh_ref, vh_ref):
    m = _rms(m_ref[...], g_ref[...]).astype(BF16)
    k = _dot(m, wk_ref[...])
    v = _dot(m, wv_ref[...])
    for h in range(MEM_HEADS):
        cols = slice(h * MEM_HEAD_DIM, (h + 1) * MEM_HEAD_DIM)
        k_ref[:, h, :] = k[:, cols]
        v_ref[:, h, :] = v[:, cols]
        kh_ref[0, h] = k[:, cols].astype(BF16)
        vh_ref[0, h] = v[:, cols].astype(BF16)


def _mem_kv(mem2d, g, wk, wv):
    rows = mem2d.shape[0]
    nb = rows // N_MEM
    row_spec = pl.BlockSpec((N_MEM, D_MODEL), lambda i: (i, 0))
    head_spec = pl.BlockSpec((N_MEM, MEM_HEADS, MEM_HEAD_DIM), lambda i: (i, 0, 0))
    hm_spec = pl.BlockSpec((1, MEM_HEADS, N_MEM, MEM_HEAD_DIM), lambda i: (i, 0, 0, 0))
    out = jax.ShapeDtypeStruct((rows, MEM_HEADS, MEM_HEAD_DIM), F32)
    out_hm = jax.ShapeDtypeStruct((nb, MEM_HEADS, N_MEM, MEM_HEAD_DIM), BF16)
    return pl.pallas_call(
        _mem_kv_kernel,
        grid=(nb,),
        in_specs=[row_spec, _const_spec((1, D_MODEL)),
                  _const_spec((D_MODEL, D_MODEL)), _const_spec((D_MODEL, D_MODEL))],
        out_specs=(head_spec, head_spec, hm_spec, hm_spec),
        out_shape=(out, out, out_hm, out_hm),
        compiler_params=_params(("arbitrary",)),
        name="mem_kv",
    )(mem2d, g, wk, wv)


def _conv_rows(vext, ytm, cw_ref, j, base, *, ns):
    acc = jnp.zeros((CONV_ROWS, LANES), F32)
    for k in range(CONV_WIDTH):
        acc = acc + cw_ref[j, k:k + 1, :] * vext[j, pl.ds(base + (FIRST_TAP + k) * ns, CONV_ROWS), :]
    ytm[j, pl.ds(base, CONV_ROWS), :] = acc


def _ln_swish(ytm, cb_ref, lng_ref, lnb_ref):
    yc = jnp.concatenate([ytm[j] for j in range(CH_TILES)], axis=1) + cb_ref[...]
    mu = jnp.mean(yc, axis=-1, keepdims=True)
    var = jnp.mean(jnp.square(yc - mu), axis=-1, keepdims=True)
    yn = (yc - mu) * lax.rsqrt(var + EPS) * lng_ref[...] + lnb_ref[...]
    return yn * jax.nn.sigmoid(yn)


def _swiglu_half(x, g, wg_ref, wu_ref, wd_ref):
    hn = _rms(x, g).astype(BF16)
    gate = _dot(hn, wg_ref[...])
    up = _dot(hn, wu_ref[...])
    act = (gate * jax.nn.sigmoid(gate) * up).astype(BF16)
    return x + 0.5 * _dot(act, wd_ref[...])


def _ffn_in_kernel(x_ref, g1_ref, wg_ref, wu_ref, wd_ref, g2_ref, win_ref, x1_ref, z_ref):
    x1 = _swiglu_half(x_ref[...], g1_ref[...], wg_ref, wu_ref, wd_ref)
    x1_ref[...] = x1
    z_ref[...] = _dot(_rms(x1, g2_ref[...]).astype(BF16), win_ref[...])


def _ffn_in(x2d, g1, wg, wu, wd, g2, win, tm=512):
    rows = x2d.shape[0]
    zdim = win.shape[1]
    row_spec = pl.BlockSpec((tm, D_MODEL), lambda i: (i, 0))
    return pl.pallas_call(
        _ffn_in_kernel,
        grid=(rows // tm,),
        in_specs=[row_spec, _const_spec((1, D_MODEL)),
                  _const_spec((D_MODEL, D_FF)), _const_spec((D_MODEL, D_FF)), _const_spec((D_FF, D_MODEL)),
                  _const_spec((1, D_MODEL)), _const_spec((D_MODEL, zdim))],
        out_specs=(row_spec, pl.BlockSpec((tm, zdim), lambda i: (i, 0))),
        out_shape=(jax.ShapeDtypeStruct((rows, D_MODEL), F32), jax.ShapeDtypeStruct((rows, zdim), F32)),
        compiler_params=_params(("arbitrary",)),
        name="ffn_in",
    )(x2d, g1, wg, wu, wd, g2, win)


def _softmax_rows(sc):
    e = jnp.exp(sc - jnp.max(sc, axis=-1, keepdims=True))
    return e / jnp.sum(e, axis=-1, keepdims=True)


def _attn_long_scores(q_ref, k_ref):
    scale = MEM_HEAD_DIM ** -0.5
    return [lax.dot_general(q_ref[:, h * MEM_HEAD_DIM:(h + 1) * MEM_HEAD_DIM], k_ref[0, h],
                            (((1,), (1,)), ((), ())), preferred_element_type=F32) * scale
            for h in range(MEM_HEADS)]


def _attn_long_values(scores, v_ref):
    return jnp.concatenate([_dot(_softmax_rows(sc).astype(BF16), v_ref[0, h]).astype(BF16)
                            for h, sc in enumerate(scores)], axis=1)


def _attn_short_scores(q_ref, k_ref, *, nseq, lq):
    scale = MEM_HEAD_DIM ** -0.5
    kv_rows = N_MEM * MEM_HEADS
    k2d = k_ref.reshape(nseq * kv_rows, MEM_HEAD_DIM)
    shape = (MEM_HEADS * lq, kv_rows)
    same_head = (lax.broadcasted_iota(jnp.int32, shape, 0) // lq
                 == lax.broadcasted_iota(jnp.int32, shape, 1) % MEM_HEADS)
    scores = []
    for s in range(nseq):
        rows = slice(s * lq, (s + 1) * lq)
        q = jnp.concatenate([q_ref[rows, h * MEM_HEAD_DIM:(h + 1) * MEM_HEAD_DIM] for h in range(MEM_HEADS)],
                            axis=0)
        k = k2d[s * kv_rows:(s + 1) * kv_rows, :].astype(BF16)
        sc = lax.dot_general(q, k, (((1,), (1,)), ((), ())), preferred_element_type=F32) * scale
        scores.append(jnp.where(same_head, sc, -jnp.inf))
    return scores


def _attn_short_values(scores, v_ref, *, lq):
    kv_rows = N_MEM * MEM_HEADS
    v2d = v_ref.reshape(len(scores) * kv_rows, MEM_HEAD_DIM)
    out = []
    for s, sc in enumerate(scores):
        v = v2d[s * kv_rows:(s + 1) * kv_rows, :].astype(BF16)
        o = _dot(_softmax_rows(sc).astype(BF16), v)
        out.append(jnp.concatenate([o[h * lq:(h + 1) * lq, :] for h in range(MEM_HEADS)], axis=1))
    return jnp.concatenate(out, axis=0)


def _tail_kernel(x2a_ref, qa_ref, ka_ref, va_ref, x2b_ref, qb_ref, kb_ref, vb_ref,
                 wo_ref, g_ref, wg_ref, wu_ref, wd_ref, gf_ref, ya_ref, yb_ref, oa_scr, ob_scr,
                 *, a_steps, tm, nseq, lq):
    j = pl.program_id(0)
    in_b = j > a_steps

    @pl.when(j == 0)
    def _():
        oa_scr[...] = jnp.zeros_like(oa_scr)
        ob_scr[...] = jnp.zeros_like(ob_scr)

    b_tile = pl.ds(pl.multiple_of(jnp.maximum(j - a_steps - 1, 0) * tm, tm), tm)
    o = jnp.where(in_b, ob_scr[b_tile, :], oa_scr[...])
    x2 = jnp.where(in_b, x2b_ref[...], x2a_ref[...])

    sc_a = _attn_long_scores(qa_ref, ka_ref)
    sc_b = _attn_short_scores(qb_ref, kb_ref, nseq=nseq, lq=lq)

    x3 = x2 + _dot(o, wo_ref[...])
    hn = _rms(x3, g_ref[...]).astype(BF16)
    gate = _dot(hn, wg_ref[...])
    up = _dot(hn, wu_ref[...])
    act = (gate * jax.nn.sigmoid(gate) * up).astype(BF16)
    x4 = x3 + 0.5 * _dot(act, wd_ref[...])

    b_rows = nseq * lq
    blk = jnp.minimum(j, a_steps - 1)
    ob_scr[pl.ds(pl.multiple_of(blk * b_rows, b_rows), b_rows), :] = (
        _attn_short_values(sc_b, vb_ref, lq=lq).astype(BF16))
    oa_scr[...] = _attn_long_values(sc_a, va_ref)

    y = _rms(x4, gf_ref[...])

    @pl.when(jnp.logical_and(j >= 1, j <= a_steps))
    def _():
        ya_ref[...] = y

    @pl.when(in_b)
    def _():
        yb_ref[...] = y


def _tail(x2a, qa, ka_hm, va_hm, x2b, qb, kb3d, vb3d, wo, g, wg, wu, wd, gf, *, tm):
    a_rows, b_rows = x2a.shape[0], x2b.shape[0]
    a_steps, b_steps = a_rows // tm, b_rows // tm
    n_b = kb3d.shape[0] // N_MEM
    lq = b_rows // n_b
    assert n_b % a_steps == 0
    nseq = n_b // a_steps
    tiles_per_seq = a_rows // ka_hm.shape[0] // tm
    attn_blk = lambda j: jnp.minimum(j, a_steps - 1)
    a_fin = lambda j: (jnp.clip(j - 1, 0, a_steps - 1), 0)
    b_fin = lambda j: (jnp.maximum(j - a_steps - 1, 0), 0)
    row = lambda m: pl.BlockSpec((tm, D_MODEL), m)
    ka_spec = pl.BlockSpec((1, MEM_HEADS, N_MEM, MEM_HEAD_DIM),
                           lambda j: (attn_blk(j) // tiles_per_seq, 0, 0, 0))
    kb_spec = pl.BlockSpec((nseq * N_MEM, MEM_HEADS, MEM_HEAD_DIM), lambda j: (attn_blk(j), 0, 0))
    return pl.pallas_call(
        functools.partial(_tail_kernel, a_steps=a_steps, tm=tm, nseq=nseq, lq=lq),
        grid=(a_steps + 1 + b_steps,),
        in_specs=[row(a_fin), row(lambda j: (attn_blk(j), 0)), ka_spec, ka_spec,
                  row(b_fin), pl.BlockSpec((nseq * lq, D_MODEL), lambda j: (attn_blk(j), 0)), kb_spec, kb_spec,
                  _const_spec((D_MODEL, D_MODEL)), _const_spec((1, D_MODEL)),
                  _const_spec((D_MODEL, D_FF)), _const_spec((D_MODEL, D_FF)), _const_spec((D_FF, D_MODEL)),
                  _const_spec((1, D_MODEL))],
        out_specs=(row(a_fin), row(b_fin)),
        out_shape=(jax.ShapeDtypeStruct((a_rows, D_MODEL), F32), jax.ShapeDtypeStruct((b_rows, D_MODEL), F32)),
        scratch_shapes=[pltpu.VMEM((tm, D_MODEL), BF16), pltpu.VMEM((b_rows, D_MODEL), BF16)],
        compiler_params=_params(("arbitrary",)),
        name="tail",
    )(x2a, qa, ka_hm, va_hm, x2b, qb, kb3d, vb3d, wo, g, wg, wu, wd, gf)


def _s5_input_and_conv(utm, vext, ytm, xr, xi, bblk_ref, cw_ref, *, ns, rows):
    def body(kb, carry):
        x = _dot(utm[kb].astype(BF16), bblk_ref[kb])
        for i in range(TILES_PER_CH):
            xr[kb * TILES_PER_CH + i] = x[:, _lane_tile(i)]
            xi[kb * TILES_PER_CH + i] = x[:, _lane_tile(TILES_PER_CH + i)]
        for b in range(rows // CONV_ROWS):
            _conv_rows(vext, ytm, cw_ref, kb, b * CONV_ROWS, ns=ns)
        return carry

    lax.fori_loop(0, CH_TILES, body, 0)


def _s5_scan_readout(utm, xr, xi, hcr, hci, cre_ref, cim_ref, ar_ref, ai_ref, d_ref, wglu_ref, *, ns, tt):
    for i in range(ns // SUBLANES):
        srow = slice(i * SUBLANES, (i + 1) * SUBLANES)
        for jg in range(STATE_TILES // SCAN_TILE_GROUP):
            tiles = tuple(range(jg * SCAN_TILE_GROUP, (jg + 1) * SCAN_TILE_GROUP))
            a_r = [ar_ref[j] for j in tiles]
            a_i = [ai_ref[j] for j in tiles]
            init = tuple(hcr[j, srow, :] for j in tiles) + tuple(hci[j, srow, :] for j in tiles)

            def body(t, carry, tiles=tiles, a_r=a_r, a_i=a_i, i=i):
                row = pl.ds(pl.multiple_of(t * ns + i * SUBLANES, SUBLANES), SUBLANES)
                new_r, new_i = [], []
                for idx, j in enumerate(tiles):
                    h_r, h_i = carry[idx], carry[SCAN_TILE_GROUP + idx]
                    n_r = a_r[idx] * h_r - a_i[idx] * h_i + xr[j, row, :]
                    n_i = a_r[idx] * h_i + a_i[idx] * h_r + xi[j, row, :]
                    xr[j, row, :] = n_r
                    xi[j, row, :] = n_i
                    new_r.append(n_r)
                    new_i.append(n_i)
                return tuple(new_r) + tuple(new_i)

            fin = lax.fori_loop(0, tt, body, init, unroll=8)
            for idx, j in enumerate(tiles):
                hcr[j, srow, :] = fin[idx]
                hci[j, srow, :] = fin[SCAN_TILE_GROUP + idx]

    ys = []
    for ob in range(CH_TILES):
        h_re = jnp.concatenate([xr[ob * TILES_PER_CH + i] for i in range(TILES_PER_CH)], axis=1).astype(BF16)
        h_im = jnp.concatenate([xi[ob * TILES_PER_CH + i] for i in range(TILES_PER_CH)], axis=1).astype(BF16)
        ys.append(_dot(h_re, cre_ref[ob]) + _dot(h_im, cim_ref[ob]))
    y = jnp.concatenate(ys, axis=1) + d_ref[...] * jnp.concatenate([utm[j] for j in range(CH_TILES)], axis=1)
    g = jax.nn.gelu(y)
    return g * jax.nn.sigmoid(_dot(g.astype(BF16), wglu_ref[...]))


def _project_out(s_out, c_out, x1_ref, wout_ref, gx_ref, wq_ref, x2_ref, q_ref, dtm, xn, *, ns, tt):
    mix = jnp.concatenate([s_out, c_out], axis=1).astype(BF16)
    delta = _dot(mix, wout_ref[...])
    for j in range(MODEL_TILES):
        dtm[j] = delta[:, _lane_tile(j)]
    for s in range(ns):
        d_s = jnp.concatenate([dtm[j, pl.ds(s, tt, stride=ns), :] for j in range(MODEL_TILES)], axis=1)
        x2_s = x1_ref[s] + d_s
        x2_ref[s] = x2_s
        xn[s * tt:(s + 1) * tt, :] = _rms(x2_s, gx_ref[...])
    qv = _dot(xn[...].astype(BF16), wq_ref[...]).astype(BF16)
    for s in range(ns):
        q_ref[s] = qv[s * tt:(s + 1) * tt, :]


def _state_out(hre_out, him_out, hcr, hci):
    hre_out[...] = jnp.concatenate([hcr[j] for j in range(STATE_TILES)], axis=1)
    him_out[...] = jnp.concatenate([hci[j] for j in range(STATE_TILES)], axis=1)


_S5_WEIGHTS = ("bblk", "cre", "cim", "ab_re", "ab_im", "d", "w_glu")
_CONV_WEIGHTS = ("conv_w", "conv_b", "ln_g", "ln_b")
_OUT_WEIGHTS = ("w_out", "g_x", "w_q")


def _mixer_scratch(ns, rows):
    return [pltpu.VMEM((STATE_TILES, rows, LANES), F32),
            pltpu.VMEM((STATE_TILES, rows, LANES), F32),
            pltpu.VMEM((STATE_TILES, ns, LANES), F32),
            pltpu.VMEM((STATE_TILES, ns, LANES), F32),
            pltpu.VMEM((MODEL_TILES, rows, LANES), F32),
            pltpu.VMEM((rows, D_MODEL), F32)]


def _mixer_kernel(*refs, ns, tt, has_init):
    n_in = 7 if has_init else 4
    u_ref, val_ref, gate_ref, x1_ref = refs[:4]
    (bblk_ref, cre_ref, cim_ref, ar_ref, ai_ref, d_ref, wglu_ref,
     cw_ref, cb_ref, lng_ref, lnb_ref, wout_ref, gx_ref, wq_ref) = refs[n_in:n_in + 14]
    x2_ref, q_ref, hre_out, him_out, buf_out = refs[n_in + 14:n_in + 19]
    xr, xi, hcr, hci, dtm, xn, utm, vext, ytm = refs[n_in + 19:]
    rows = ns * tt
    hist_rows = HIST * ns

    if has_init:
        h0r_ref, h0i_ref, cache_ref = refs[4:7]
        for j in range(STATE_TILES):
            hcr[j] = h0r_ref[:, _lane_tile(j)]
            hci[j] = h0i_ref[:, _lane_tile(j)]
        for j in range(CH_TILES):
            vext[j, FIRST_TAP * ns:hist_rows, :] = (
                cache_ref[:, :, _lane_tile(j)].reshape((CONV_WIDTH - 1) * ns, LANES))
    else:
        @pl.when(pl.program_id(0) == 0)
        def _():
            hcr[...] = jnp.zeros_like(hcr)
            hci[...] = jnp.zeros_like(hci)
            vext[:, :hist_rows, :] = jnp.zeros((CH_TILES, hist_rows, LANES), F32)

    for s in range(ns):
        u_s = u_ref[s]
        v_s = val_ref[s] * jax.nn.sigmoid(gate_ref[s])
        for j in range(CH_TILES):
            utm[j, pl.ds(s, tt, stride=ns), :] = u_s[:, _lane_tile(j)]
            vext[j, pl.ds(hist_rows + s, tt, stride=ns), :] = v_s[:, _lane_tile(j)]

    _s5_input_and_conv(utm, vext, ytm, xr, xi, bblk_ref, cw_ref, ns=ns, rows=rows)
    s_out = _s5_scan_readout(utm, xr, xi, hcr, hci, cre_ref, cim_ref, ar_ref, ai_ref, d_ref, wglu_ref,
                             ns=ns, tt=tt)
    c_out = _ln_swish(ytm, cb_ref, lng_ref, lnb_ref)
    _project_out(s_out, c_out, x1_ref, wout_ref, gx_ref, wq_ref, x2_ref, q_ref, dtm, xn, ns=ns, tt=tt)
    _state_out(hre_out, him_out, hcr, hci)

    def write_hist():
        for j in range(CH_TILES):
            buf_out[:, :, _lane_tile(j)] = (
                vext[j, (tt + FIRST_TAP) * ns:(tt + HIST) * ns, :].reshape(CONV_WIDTH - 1, ns, LANES))

    if has_init:
        write_hist()
    else:
        pl.when(pl.program_id(0) == pl.num_programs(0) - 1)(write_hist)
        vext[:, :hist_rows, :] = vext[:, rows:rows + hist_rows, :]


def _mixer(z3, x13, init, w, *, ns, tt):
    nseq, seq_len, _ = z3.shape
    has_init = init is not None
    if has_init:
        assert tt == seq_len and nseq % ns == 0
        grid = (nseq // ns,)
        blk = lambda i: (i, 0, 0)
        col = lambda c: (lambda i: (i, 0, c))
        st_map = lambda i: (i, 0)
        hist_map = lambda i: (0, i, 0)
    else:
        assert ns == nseq and seq_len % tt == 0
        grid = (seq_len // tt,)
        blk = lambda i: (0, i, 0)
        col = lambda c: (lambda i: (0, i, c))
        st_map = lambda i: (0, 0)
        hist_map = lambda i: (0, 0, 0)
    rows = ns * tt
    assert rows % CONV_ROWS == 0 and ns % SUBLANES == 0
    st_spec = pl.BlockSpec((ns, N_STATE), st_map)
    hist_spec = pl.BlockSpec((CONV_WIDTH - 1, ns, CONV_DIM), hist_map)
    row_spec = pl.BlockSpec((ns, tt, D_MODEL), blk)
    weights = [w[k] for k in _S5_WEIGHTS + _CONV_WEIGHTS + _OUT_WEIGHTS]
    in_specs = [pl.BlockSpec((ns, tt, SSM_DIM), col(c)) for c in range(3)] + [row_spec]
    args = [z3, z3, z3, x13]
    if has_init:
        in_specs += [st_spec, st_spec, hist_spec]
        args += list(init)
    in_specs += [_const_spec(a.shape) for a in weights]
    return pl.pallas_call(
        functools.partial(_mixer_kernel, ns=ns, tt=tt, has_init=has_init),
        grid=grid,
        in_specs=in_specs,
        out_specs=(row_spec, row_spec, st_spec, st_spec, hist_spec),
        out_shape=(jax.ShapeDtypeStruct((nseq, seq_len, D_MODEL), F32),
                   jax.ShapeDtypeStruct((nseq, seq_len, D_MODEL), BF16),
                   jax.ShapeDtypeStruct((nseq, N_STATE), F32), jax.ShapeDtypeStruct((nseq, N_STATE), F32),
                   jax.ShapeDtypeStruct((CONV_WIDTH - 1, nseq, CONV_DIM), F32)),
        scratch_shapes=_mixer_scratch(ns, rows) + [
            pltpu.VMEM((CH_TILES, rows, LANES), F32),
            pltpu.VMEM((CH_TILES, rows + HIST * ns, LANES), F32),
            pltpu.VMEM((CH_TILES, rows, LANES), F32)],
        compiler_params=_params(("arbitrary",)),
        name="mixer_init" if has_init else "mixer_zero",
    )(*args, *weights)


def _block_diag_tiles(x):
    _, a, b = x.shape
    x4 = x.reshape(CH_TILES, GROUPS_PER_TILE, a, b)
    eye = jnp.eye(GROUPS_PER_TILE, dtype=x.dtype)
    out = x4[:, :, :, None, :] * eye[None, :, None, :, None]
    return out.reshape(CH_TILES, GROUPS_PER_TILE * a, GROUPS_PER_TILE * b)


def _row(x):
    return x.reshape(1, -1)


def _state_slabs(x):
    return jnp.broadcast_to(x.reshape(STATE_TILES, 1, LANES), (STATE_TILES, SUBLANES, LANES))


def kernel(x_prompt, x_sample, state_ssm_re, state_ssm_im, cache_conv, cache_mem_k, cache_mem_v, mem_prompt,
           g_mem, w_mem_k, w_mem_v, g_ffn1, w_ffn1_gate, w_ffn1_up, w_ffn1_down, g_mix, w_in,
           ssm_a_re, ssm_a_im, ssm_log_dt, ssm_b_re, ssm_b_im, ssm_c_re, ssm_c_im, ssm_d, w_ssm_glu,
           conv_w, conv_b, conv_ln_g, conv_ln_b, w_out, g_xattn, w_mem_q, w_mem_o,
           g_ffn2, w_ffn2_gate, w_ffn2_up, w_ffn2_down, g_final):
    depth = g_ffn1.shape[0]
    assert depth == 1
    l = 0
    bp, seq, _ = x_prompt.shape
    bs, dseq, _ = x_sample.shape
    bf = lambda a: a.astype(BF16)

    ab_re, ab_im, bbar_re, bbar_im = _s5_params(ssm_a_re[l], ssm_a_im[l], ssm_log_dt[l],
                                                jnp.swapaxes(ssm_b_re[l], 1, 2), jnp.swapaxes(ssm_b_im[l], 1, 2))
    mixer_w = dict(
        bblk=bf(jnp.concatenate([_block_diag_tiles(bbar_re), _block_diag_tiles(bbar_im)], axis=-1)),
        cre=bf(_block_diag_tiles(jnp.swapaxes(ssm_c_re[l], 1, 2))),
        cim=bf(-_block_diag_tiles(jnp.swapaxes(ssm_c_im[l], 1, 2))),
        ab_re=_state_slabs(ab_re), ab_im=_state_slabs(ab_im),
        d=_row(ssm_d[l]), w_glu=bf(w_ssm_glu[l]),
        conv_w=jnp.swapaxes(conv_w[l].reshape(CONV_WIDTH, CH_TILES, LANES), 0, 1),
        conv_b=_row(conv_b[l]), ln_g=_row(conv_ln_g[l]), ln_b=_row(conv_ln_b[l]),
        w_out=bf(w_out[l]), g_x=_row(g_xattn[l]), w_q=bf(w_mem_q[l]),
    )
    ffn1 = (_row(g_ffn1[l]), bf(w_ffn1_gate[l]), bf(w_ffn1_up[l]), bf(w_ffn1_down[l]))
    ffn2 = (_row(g_ffn2[l]), bf(w_ffn2_gate[l]), bf(w_ffn2_up[l]), bf(w_ffn2_down[l]))
    g_mix_r, w_in_b, w_o_b, g_final_r = _row(g_mix[l]), bf(w_in[l]), bf(w_mem_o[l]), _row(g_final)

    mk, mv, mk_hm, mv_hm = _mem_kv(mem_prompt.reshape(bp * N_MEM, D_MODEL), _row(g_mem[l]),
                                   bf(w_mem_k[l]), bf(w_mem_v[l]))

    p_rows, s_rows = bp * seq, bs * dseq
    x1p, zp = _ffn_in(x_prompt.reshape(p_rows, D_MODEL), *ffn1, g_mix_r, w_in_b)
    x2p, qp, p_re, p_im, p_buf = _mixer(zp.reshape(bp, seq, -1), x1p.reshape(bp, seq, D_MODEL), None,
                                        mixer_w, ns=bp, tt=64)

    x1s, zs = _ffn_in(x_sample.reshape(s_rows, D_MODEL), *ffn1, g_mix_r, w_in_b)
    init_s = (state_ssm_re[l].reshape(bs, N_STATE), state_ssm_im[l].reshape(bs, N_STATE),
              jnp.swapaxes(cache_conv[l], 0, 1))
    x2s, qs, s_re, s_im, s_buf = _mixer(zs.reshape(bs, dseq, -1), x1s.reshape(bs, dseq, D_MODEL), init_s,
                                        mixer_w, ns=32, tt=dseq)

    kv_rows = lambda a: a.reshape(bs * N_MEM, MEM_HEADS, MEM_HEAD_DIM)
    yp, ys = _tail(x2p.reshape(p_rows, D_MODEL), qp.reshape(p_rows, D_MODEL), mk_hm, mv_hm,
                   x2s.reshape(s_rows, D_MODEL), qs.reshape(s_rows, D_MODEL),
                   kv_rows(cache_mem_k), kv_rows(cache_mem_v), w_o_b, *ffn2, g_final_r, tm=256)
    yp = yp.reshape(bp, seq, D_MODEL)
    ys = ys.reshape(bs, dseq, D_MODEL)

    st = lambda a, n: a.reshape(1, n, SSM_GROUPS, SSM_STATE)
    kv = lambda a: a.reshape(1, bp, N_MEM, MEM_HEADS, MEM_HEAD_DIM)
    buf = lambda a: jnp.swapaxes(a, 0, 1)[None]
    return (yp, ys, st(p_re, bp), st(p_im, bp), buf(p_buf), kv(mk), kv(mv),
            st(s_re, bs), st(s_im, bs), buf(s_buf))
```

```python
import functools

import jax
import jax.numpy as jnp
from jax import lax
from jax.experimental import pallas as pl
from jax.experimental.pallas import tpu as pltpu

F32 = jnp.float32
BF16 = jnp.bfloat16

EPS = 1e-6
D_MODEL = 1024
D_FF = 2816
SSM_DIM = 512
CONV_DIM = 512
SSM_GROUPS = 32
SSM_GROUP_CH = 16
SSM_STATE = 64
N_STATE = SSM_GROUPS * SSM_STATE
CONV_WIDTH = 31
N_MEM = 256
MEM_HEADS = 4
MEM_HEAD_DIM = 256

LANES = 128
SUBLANES = 8
STATE_TILES = N_STATE // LANES
CH_TILES = SSM_DIM // LANES
MODEL_TILES = D_MODEL // LANES
GROUPS_PER_TILE = LANES // SSM_GROUP_CH
STATES_PER_CH_TILE = GROUPS_PER_TILE * SSM_STATE
TILES_PER_CH = STATES_PER_CH_TILE // LANES
HIST = 32
FIRST_TAP = HIST - (CONV_WIDTH - 1)
SCAN_TILE_GROUP = 4
CONV_ROWS = 64
VMEM_LIMIT = 56 * 1024 * 1024


def _dot(a, b):
    return jnp.dot(a, b, preferred_element_type=F32)


def _rms(x, g):
    return x * lax.rsqrt(jnp.mean(x * x, axis=-1, keepdims=True) + EPS) * g


def _lane_tile(j):
    return slice(j * LANES, (j + 1) * LANES)


def _const_spec(shape):
    zeros = (0,) * len(shape)
    return pl.BlockSpec(shape, lambda *_: zeros, pipeline_mode=pl.Buffered(1))


def _params(sem):
    return pltpu.CompilerParams(dimension_semantics=sem, vmem_limit_bytes=VMEM_LIMIT)


def _s5_params_kernel(ar_ref, ai_ref, ldt_ref, bre_ref, bim_ref, abr_ref, abi_ref, bbr_ref, bbi_ref):
    lr = ar_ref[...]
    li = ai_ref[...]
    dt = jnp.exp(ldt_ref[...])
    mag = jnp.exp(lr * dt)
    ab_re = mag * jnp.cos(li * dt)
    ab_im = mag * jnp.sin(li * dt)
    den = lr * lr + li * li
    nr = ab_re - 1.0
    ni = ab_im
    coef_re = (nr * lr + ni * li) / den
    coef_im = (ni * lr - nr * li) / den
    abr_ref[...] = ab_re
    abi_ref[...] = ab_im
    b_re = bre_ref[...]
    b_im = bim_ref[...]
    bbr_ref[...] = coef_re * b_re - coef_im * b_im
    bbi_ref[...] = coef_re * b_im + coef_im * b_re


def _s5_params(a_re, a_im, log_dt, b_re, b_im):
    a_shape = jax.ShapeDtypeStruct((SSM_GROUPS, 1, SSM_STATE), F32)
    b_shape = jax.ShapeDtypeStruct(b_re.shape, F32)
    return pl.pallas_call(
        _s5_params_kernel,
        out_shape=(a_shape, a_shape, b_shape, b_shape),
        name="s5_params",
    )(a_re.reshape(a_shape.shape), a_im.reshape(a_shape.shape), log_dt.reshape(SSM_GROUPS, 1, 1), b_re, b_im)


def _mem_kv_kernel(m_ref, g_ref, wk_ref, wv_ref, k_ref, v_ref, kh_ref, vh_ref):
    m = _rms(m_ref[...], g_ref[...]).astype(BF16)
    k = _dot(m, wk_ref[...])
    v = _dot(m, wv_ref[...])
    for h in range(MEM_HEADS):
        cols = slice(h * MEM_HEAD_DIM, (h + 1) * MEM_HEAD_DIM)
        k_ref[:, h, :] = k[:, cols]
        v_ref[:, h, :] = v[:, cols]
        kh_ref[0, h] = k[:, cols].T.astype(BF16)
        vh_ref[0, h] = v[:, cols].astype(BF16)


def _mem_kv(mem2d, g, wk, wv):
    rows = mem2d.shape[0]
    nb = rows // N_MEM
    row_spec = pl.BlockSpec((N_MEM, D_MODEL), lambda i: (i, 0))
    head_spec = pl.BlockSpec((N_MEM, MEM_HEADS, MEM_HEAD_DIM), lambda i: (i, 0, 0))
    hm_spec = pl.BlockSpec((1, MEM_HEADS, N_MEM, MEM_HEAD_DIM), lambda i: (i, 0, 0, 0))
    out = jax.ShapeDtypeStruct((rows, MEM_HEADS, MEM_HEAD_DIM), F32)
    out_hm = jax.ShapeDtypeStruct((nb, MEM_HEADS, N_MEM, MEM_HEAD_DIM), BF16)
    return pl.pallas_call(
        _mem_kv_kernel,
        grid=(nb,),
        in_specs=[row_spec, _const_spec((1, D_MODEL)),
                  _const_spec((D_MODEL, D_MODEL)), _const_spec((D_MODEL, D_MODEL))],
        out_specs=(head_spec, head_spec, hm_spec, hm_spec),
        out_shape=(out, out, out_hm, out_hm),
        compiler_params=_params(("arbitrary",)),
        name="mem_kv",
    )(mem2d, g, wk, wv)


def _zero_after(piece):
    bits = pltpu.bitcast(piece, jnp.uint32)
    return pltpu.bitcast((bits >> 16) >> 16, F32)


def _conv_rows(vext, ytm, cw_ref, j, base, *, ns, after):
    acc = _zero_after(after)
    for k in range(CONV_WIDTH):
        acc = acc + cw_ref[j, k:k + 1, :] * vext[j, pl.ds(base + (FIRST_TAP + k) * ns, CONV_ROWS), :]
    ytm[j, pl.ds(base, CONV_ROWS), :] = acc


def _ln_swish(ytm, cb_ref, lng_ref, lnb_ref):
    yc = jnp.concatenate([ytm[j] for j in range(CH_TILES)], axis=1) + cb_ref[...]
    mu = jnp.mean(yc, axis=-1, keepdims=True)
    var = jnp.mean(jnp.square(yc - mu), axis=-1, keepdims=True)
    yn = (yc - mu) * lax.rsqrt(var + EPS) * lng_ref[...] + lnb_ref[...]
    return yn * jax.nn.sigmoid(yn)


def _swiglu_half(x, g, wg_ref, wu_ref, wd_ref):
    hn = _rms(x, g).astype(BF16)
    gate = _dot(hn, wg_ref[...])
    up = _dot(hn, wu_ref[...])
    act = (gate * jax.nn.sigmoid(gate) * up).astype(BF16)
    return x + 0.5 * _dot(act, wd_ref[...])


def _ffn_in_kernel(x_ref, g1_ref, wg_ref, wu_ref, wd_ref, g2_ref, win_ref, x1_ref, z_ref):
    x1 = _swiglu_half(x_ref[...], g1_ref[...], wg_ref, wu_ref, wd_ref)
    x1_ref[...] = x1
    z_ref[...] = _dot(_rms(x1, g2_ref[...]).astype(BF16), win_ref[...])


def _ffn_in(x2d, g1, wg, wu, wd, g2, win, tm=512):
    rows = x2d.shape[0]
    zdim = win.shape[1]
    row_spec = pl.BlockSpec((tm, D_MODEL), lambda i: (i, 0))
    return pl.pallas_call(
        _ffn_in_kernel,
        grid=(rows // tm,),
        in_specs=[row_spec, _const_spec((1, D_MODEL)),
                  _const_spec((D_MODEL, D_FF)), _const_spec((D_MODEL, D_FF)), _const_spec((D_FF, D_MODEL)),
                  _const_spec((1, D_MODEL)), _const_spec((D_MODEL, zdim))],
        out_specs=(row_spec, pl.BlockSpec((tm, zdim), lambda i: (i, 0))),
        out_shape=(jax.ShapeDtypeStruct((rows, D_MODEL), F32), jax.ShapeDtypeStruct((rows, zdim), F32)),
        compiler_params=_params(("arbitrary",)),
        name="ffn_in",
    )(x2d, g1, wg, wu, wd, g2, win)


def _front_kernel(x_ref, g1_ref, wg_ref, wu_ref, wd_ref, g2_ref, win_ref, cw_ref, cb_ref, lng_ref, lnb_ref,
                  x1_ref, u_ref, c_ref, hist_ref, vext, ytm, utm, *, ns, tt, steps):
    j = pl.program_id(0)
    rows = ns * tt
    hist_rows = HIST * ns
    gate_tile_step = (D_FF // LANES) // CH_TILES

    @pl.when(j == 0)
    def _():
        vext[...] = jnp.zeros_like(vext)

    x = x_ref[...].reshape(rows, D_MODEL)
    hn = _rms(x, g1_ref[...]).astype(BF16)
    gate = _dot(hn, wg_ref[...])
    for jt in range(CH_TILES):
        for b in range(rows // CONV_ROWS):
            _conv_rows(vext, ytm, cw_ref, jt, b * CONV_ROWS, ns=ns,
                       after=gate[b * CONV_ROWS:(b + 1) * CONV_ROWS, _lane_tile(jt * gate_tile_step)])
    c_ref[...] = _ln_swish(ytm, cb_ref, lng_ref, lnb_ref)
    up = _dot(hn, wu_ref[...])
    act = (gate * jax.nn.sigmoid(gate) * up).astype(BF16)
    x1 = x + 0.5 * _dot(act, wd_ref[...])
    x1_ref[...] = x1.reshape(ns, tt, D_MODEL)
    z = _dot(_rms(x1, g2_ref[...]).astype(BF16), win_ref[...])
    u = z[:, :SSM_DIM]
    v = z[:, SSM_DIM:SSM_DIM + CONV_DIM] * jax.nn.sigmoid(z[:, SSM_DIM + CONV_DIM:])

    vext[:, :hist_rows, :] = vext[:, rows:rows + hist_rows, :]
    for s in range(ns):
        seq = slice(s * tt, (s + 1) * tt)
        for jt in range(CH_TILES):
            vext[jt, pl.ds(hist_rows + s, tt, stride=ns), :] = v[seq, _lane_tile(jt)]
            utm[jt, pl.ds(s, tt, stride=ns), :] = u[seq, _lane_tile(jt)]
    u_ref[...] = jnp.concatenate([utm[jt] for jt in range(CH_TILES)], axis=1)

    @pl.when(j == steps)
    def _():
        for jt in range(CH_TILES):
            hist_ref[:, :, _lane_tile(jt)] = (
                vext[jt, FIRST_TAP * ns:hist_rows, :].reshape(CONV_WIDTH - 1, ns, LANES))


def _front(x3, g1, wg, wu, wd, g2, win, cw, cb, lng, lnb, *, tt):
    ns, seq_len, _ = x3.shape
    steps = seq_len // tt
    rows = ns * tt
    assert rows % CONV_ROWS == 0 and ns == SUBLANES
    zdim = win.shape[1]
    cur = lambda j: jnp.minimum(j, steps - 1)
    prev = lambda j: jnp.clip(j - 1, 0, steps - 1)
    x_spec = pl.BlockSpec((ns, tt, D_MODEL), lambda j: (0, cur(j), 0))
    tm_spec = lambda m: pl.BlockSpec((rows, SSM_DIM), lambda j: (m(j), 0))
    return pl.pallas_call(
        functools.partial(_front_kernel, ns=ns, tt=tt, steps=steps),
        grid=(steps + 1,),
        in_specs=[x_spec, _const_spec((1, D_MODEL)),
                  _const_spec((D_MODEL, D_FF)), _const_spec((D_MODEL, D_FF)), _const_spec((D_FF, D_MODEL)),
                  _const_spec((1, D_MODEL)), _const_spec((D_MODEL, zdim)),
                  _const_spec(cw.shape), _const_spec(cb.shape), _const_spec(lng.shape), _const_spec(lnb.shape)],
        out_specs=(x_spec, tm_spec(cur), tm_spec(prev),
                   pl.BlockSpec((CONV_WIDTH - 1, ns, CONV_DIM), lambda j: (0, 0, 0))),
        out_shape=(jax.ShapeDtypeStruct((ns, seq_len, D_MODEL), F32),
                   jax.ShapeDtypeStruct((seq_len * ns, SSM_DIM), F32),
                   jax.ShapeDtypeStruct((seq_len * ns, CONV_DIM), F32),
                   jax.ShapeDtypeStruct((CONV_WIDTH - 1, ns, CONV_DIM), F32)),
        scratch_shapes=[pltpu.VMEM((CH_TILES, rows + HIST * ns, LANES), F32),
                        pltpu.VMEM((CH_TILES, rows, LANES), F32),
                        pltpu.VMEM((CH_TILES, rows, LANES), F32)],
        compiler_params=_params(("arbitrary",)),
        name="front",
    )(x3, g1, wg, wu, wd, g2, win, cw, cb, lng, lnb)


def _softmax_rows(sc):
    e = jnp.exp(sc - jnp.max(sc, axis=-1, keepdims=True))
    return e / jnp.sum(e, axis=-1, keepdims=True)


def _attn_long_scores(q_ref, k_ref):
    scale = MEM_HEAD_DIM ** -0.5
    return [_dot(q_ref[:, h * MEM_HEAD_DIM:(h + 1) * MEM_HEAD_DIM], k_ref[0, h]) * scale
            for h in range(MEM_HEADS)]


def _attn_long_values(scores, v_ref):
    return jnp.concatenate([_dot(_softmax_rows(sc).astype(BF16), v_ref[0, h]).astype(BF16)
                            for h, sc in enumerate(scores)], axis=1)


def _attn_short_scores(q_ref, k_ref, *, nseq, lq):
    scale = MEM_HEAD_DIM ** -0.5
    kv_rows = N_MEM * MEM_HEADS
    k2d = k_ref.reshape(nseq * kv_rows, MEM_HEAD_DIM)
    shape = (MEM_HEADS * lq, kv_rows)
    same_head = (lax.broadcasted_iota(jnp.int32, shape, 0) // lq
                 == lax.broadcasted_iota(jnp.int32, shape, 1) % MEM_HEADS)
    scores = []
    for s in range(nseq):
        rows = slice(s * lq, (s + 1) * lq)
        q = jnp.concatenate([q_ref[rows, h * MEM_HEAD_DIM:(h + 1) * MEM_HEAD_DIM] for h in range(MEM_HEADS)],
                            axis=0)
        k = k2d[s * kv_rows:(s + 1) * kv_rows, :].astype(BF16)
        sc = lax.dot_general(q, k, (((1,), (1,)), ((), ())), preferred_element_type=F32) * scale
        scores.append(jnp.where(same_head, sc, -jnp.inf))
    return scores


def _attn_short_values(scores, v_ref, *, lq):
    kv_rows = N_MEM * MEM_HEADS
    v2d = v_ref.reshape(len(scores) * kv_rows, MEM_HEAD_DIM)
    out = []
    for s, sc in enumerate(scores):
        v = v2d[s * kv_rows:(s + 1) * kv_rows, :].astype(BF16)
        o = _dot(_softmax_rows(sc).astype(BF16), v)
        out.append(jnp.concatenate([o[h * lq:(h + 1) * lq, :] for h in range(MEM_HEADS)], axis=1))
    return jnp.concatenate(out, axis=0)


def _tail_kernel(x2a_ref, qa_ref, ka_ref, va_ref, x2b_ref, qb_ref, kb_ref, vb_ref,
                 wo_ref, g_ref, wg_ref, wu_ref, wd_ref, gf_ref, ya_ref, yb_ref, oa_scr, ob_scr,
                 *, a_steps, tm, nseq, lq):
    j = pl.program_id(0)
    in_b = j > a_steps

    @pl.when(j == 0)
    def _():
        oa_scr[...] = jnp.zeros_like(oa_scr)
        ob_scr[...] = jnp.zeros_like(ob_scr)

    b_tile = pl.ds(pl.multiple_of(jnp.maximum(j - a_steps - 1, 0) * tm, tm), tm)
    o = jnp.where(in_b, ob_scr[b_tile, :], oa_scr[...])
    x2 = jnp.where(in_b, x2b_ref[...], x2a_ref[...])

    x3 = x2 + _dot(o, wo_ref[...])
    hn = _rms(x3, g_ref[...]).astype(BF16)
    gate = _dot(hn, wg_ref[...])
    up = _dot(hn, wu_ref[...])
    sc_a = _attn_long_scores(qa_ref, ka_ref)
    sc_b = _attn_short_scores(qb_ref, kb_ref, nseq=nseq, lq=lq)
    act = (gate * jax.nn.sigmoid(gate) * up).astype(BF16)
    x4 = x3 + 0.5 * _dot(act, wd_ref[...])

    b_rows = nseq * lq
    blk = jnp.minimum(j, a_steps - 1)
    ob_scr[pl.ds(pl.multiple_of(blk * b_rows, b_rows), b_rows), :] = (
        _attn_short_values(sc_b, vb_ref, lq=lq).astype(BF16))
    oa_scr[...] = _attn_long_values(sc_a, va_ref)

    y = _rms(x4, gf_ref[...])

    @pl.when(jnp.logical_and(j >= 1, j <= a_steps))
    def _():
        ya_ref[...] = y

    @pl.when(in_b)
    def _():
        yb_ref[...] = y


def _tail(x2a, qa, ka_hm, va_hm, x2b, qb, kb3d, vb3d, wo, g, wg, wu, wd, gf, *, tm):
    a_rows, b_rows = x2a.shape[0], x2b.shape[0]
    a_steps, b_steps = a_rows // tm, b_rows // tm
    n_b = kb3d.shape[0] // N_MEM
    lq = b_rows // n_b
    assert n_b % a_steps == 0
    nseq = n_b // a_steps
    tiles_per_seq = a_rows // ka_hm.shape[0] // tm
    attn_blk = lambda j: jnp.minimum(j, a_steps - 1)
    a_fin = lambda j: (jnp.clip(j - 1, 0, a_steps - 1), 0)
    b_fin = lambda j: (jnp.maximum(j - a_steps - 1, 0), 0)
    row = lambda m: pl.BlockSpec((tm, D_MODEL), m)
    ka_spec = pl.BlockSpec((1, MEM_HEADS, N_MEM, MEM_HEAD_DIM),
                           lambda j: (attn_blk(j) // tiles_per_seq, 0, 0, 0))
    kb_spec = pl.BlockSpec((nseq * N_MEM, MEM_HEADS, MEM_HEAD_DIM), lambda j: (attn_blk(j), 0, 0))
    return pl.pallas_call(
        functools.partial(_tail_kernel, a_steps=a_steps, tm=tm, nseq=nseq, lq=lq),
        grid=(a_steps + 1 + b_steps,),
        in_specs=[row(a_fin), row(lambda j: (attn_blk(j), 0)), ka_spec, ka_spec,
                  row(b_fin), pl.BlockSpec((nseq * lq, D_MODEL), lambda j: (attn_blk(j), 0)), kb_spec, kb_spec,
                  _const_spec((D_MODEL, D_MODEL)), _const_spec((1, D_MODEL)),
                  _const_spec((D_MODEL, D_FF)), _const_spec((D_MODEL, D_FF)), _const_spec((D_FF, D_MODEL)),
                  _const_spec((1, D_MODEL))],
        out_specs=(row(a_fin), row(b_fin)),
        out_shape=(jax.ShapeDtypeStruct((a_rows, D_MODEL), F32), jax.ShapeDtypeStruct((b_rows, D_MODEL), F32)),
        scratch_shapes=[pltpu.VMEM((tm, D_MODEL), BF16), pltpu.VMEM((b_rows, D_MODEL), BF16)],
        compiler_params=_params(("arbitrary",)),
        name="tail",
    )(x2a, qa, ka_hm, va_hm, x2b, qb, kb3d, vb3d, wo, g, wg, wu, wd, gf)


def _s5_input_and_conv(utm, vext, ytm, xr, xi, bblk_ref, cw_ref, *, ns, rows):
    def body(kb, carry):
        x = _dot(utm[kb].astype(BF16), bblk_ref[kb])
        for i in range(TILES_PER_CH):
            xr[kb * TILES_PER_CH + i] = x[:, _lane_tile(i)]
            xi[kb * TILES_PER_CH + i] = x[:, _lane_tile(TILES_PER_CH + i)]
        for b in range(rows // CONV_ROWS):
            _conv_rows(vext, ytm, cw_ref, kb, b * CONV_ROWS, ns=ns,
                       after=x[b * CONV_ROWS:(b + 1) * CONV_ROWS, :LANES])
        return carry

    lax.fori_loop(0, CH_TILES, body, 0)


def _s5_scan_readout(u_all, xr, xi, hcr, hci, cre_ref, cim_ref, ar_ref, ai_ref, d_ref, wglu_ref, *, ns, tt):
    for i in range(ns // SUBLANES):
        srow = slice(i * SUBLANES, (i + 1) * SUBLANES)
        for jg in range(STATE_TILES // SCAN_TILE_GROUP):
            tiles = tuple(range(jg * SCAN_TILE_GROUP, (jg + 1) * SCAN_TILE_GROUP))
            a_r = [ar_ref[j] for j in tiles]
            a_i = [ai_ref[j] for j in tiles]
            init = tuple(hcr[j, srow, :] for j in tiles) + tuple(hci[j, srow, :] for j in tiles)

            def body(t, carry, tiles=tiles, a_r=a_r, a_i=a_i, i=i):
                row = pl.ds(pl.multiple_of(t * ns + i * SUBLANES, SUBLANES), SUBLANES)
                new_r, new_i = [], []
                for idx, j in enumerate(tiles):
                    h_r, h_i = carry[idx], carry[SCAN_TILE_GROUP + idx]
                    n_r = a_r[idx] * h_r - a_i[idx] * h_i + xr[j, row, :]
                    n_i = a_r[idx] * h_i + a_i[idx] * h_r + xi[j, row, :]
                    xr[j, row, :] = n_r
                    xi[j, row, :] = n_i
                    new_r.append(n_r)
                    new_i.append(n_i)
                return tuple(new_r) + tuple(new_i)

            fin = lax.fori_loop(0, tt, body, init, unroll=8)
            for idx, j in enumerate(tiles):
                hcr[j, srow, :] = fin[idx]
                hci[j, srow, :] = fin[SCAN_TILE_GROUP + idx]

    ys = []
    for ob in range(CH_TILES):
        h_re = jnp.concatenate([xr[ob * TILES_PER_CH + i] for i in range(TILES_PER_CH)], axis=1).astype(BF16)
        h_im = jnp.concatenate([xi[ob * TILES_PER_CH + i] for i in range(TILES_PER_CH)], axis=1).astype(BF16)
        ys.append(_dot(h_re, cre_ref[ob]) + _dot(h_im, cim_ref[ob]))
    y = jnp.concatenate(ys, axis=1) + d_ref[...] * u_all()
    g = jax.nn.gelu(y)
    return g * jax.nn.sigmoid(_dot(g.astype(BF16), wglu_ref[...]))


def _project_out(s_out, c_out, x1_ref, wout_ref, gx_ref, wq_ref, x2_ref, q_ref, dtm, xn, *, ns, tt):
    mix = jnp.concatenate([s_out, c_out], axis=1).astype(BF16)
    delta = _dot(mix, wout_ref[...])
    for j in range(MODEL_TILES):
        dtm[j] = delta[:, _lane_tile(j)]
    for s in range(ns):
        d_s = jnp.concatenate([dtm[j, pl.ds(s, tt, stride=ns), :] for j in range(MODEL_TILES)], axis=1)
        x2_s = x1_ref[s] + d_s
        x2_ref[s] = x2_s
        xn[s * tt:(s + 1) * tt, :] = _rms(x2_s, gx_ref[...])
    qv = _dot(xn[...].astype(BF16), wq_ref[...]).astype(BF16)
    for s in range(ns):
        q_ref[s] = qv[s * tt:(s + 1) * tt, :]


def _state_out(hre_out, him_out, hcr, hci):
    hre_out[...] = jnp.concatenate([hcr[j] for j in range(STATE_TILES)], axis=1)
    him_out[...] = jnp.concatenate([hci[j] for j in range(STATE_TILES)], axis=1)


_S5_WEIGHTS = ("bblk", "cre", "cim", "ab_re", "ab_im", "d", "w_glu")
_CONV_WEIGHTS = ("conv_w", "conv_b", "ln_g", "ln_b")
_OUT_WEIGHTS = ("w_out", "g_x", "w_q")


def _mixer_scratch(ns, rows):
    return [pltpu.VMEM((STATE_TILES, rows, LANES), F32),
            pltpu.VMEM((STATE_TILES, rows, LANES), F32),
            pltpu.VMEM((STATE_TILES, ns, LANES), F32),
            pltpu.VMEM((STATE_TILES, ns, LANES), F32),
            pltpu.VMEM((MODEL_TILES, rows, LANES), F32),
            pltpu.VMEM((rows, D_MODEL), F32)]


def _mixer_kernel(*refs, ns, tt, has_init):
    n_in = 7 if has_init else 4
    u_ref, val_ref, gate_ref, x1_ref = refs[:4]
    (bblk_ref, cre_ref, cim_ref, ar_ref, ai_ref, d_ref, wglu_ref,
     cw_ref, cb_ref, lng_ref, lnb_ref, wout_ref, gx_ref, wq_ref) = refs[n_in:n_in + 14]
    x2_ref, q_ref, hre_out, him_out, buf_out = refs[n_in + 14:n_in + 19]
    xr, xi, hcr, hci, dtm, xn, utm, vext, ytm = refs[n_in + 19:]
    rows = ns * tt
    hist_rows = HIST * ns

    if has_init:
        h0r_ref, h0i_ref, cache_ref = refs[4:7]
        for j in range(STATE_TILES):
            hcr[j] = h0r_ref[:, _lane_tile(j)]
            hci[j] = h0i_ref[:, _lane_tile(j)]
        for j in range(CH_TILES):
            vext[j, FIRST_TAP * ns:hist_rows, :] = (
                cache_ref[:, :, _lane_tile(j)].reshape((CONV_WIDTH - 1) * ns, LANES))
    else:
        @pl.when(pl.program_id(0) == 0)
        def _():
            hcr[...] = jnp.zeros_like(hcr)
            hci[...] = jnp.zeros_like(hci)
            vext[:, :hist_rows, :] = jnp.zeros((CH_TILES, hist_rows, LANES), F32)

    for s in range(ns):
        u_s = u_ref[s]
        v_s = val_ref[s] * jax.nn.sigmoid(gate_ref[s])
        for j in range(CH_TILES):
            utm[j, pl.ds(s, tt, stride=ns), :] = u_s[:, _lane_tile(j)]
            vext[j, pl.ds(hist_rows + s, tt, stride=ns), :] = v_s[:, _lane_tile(j)]

    _s5_input_and_conv(utm, vext, ytm, xr, xi, bblk_ref, cw_ref, ns=ns, rows=rows)
    u_all = lambda: jnp.concatenate([utm[j] for j in range(CH_TILES)], axis=1)
    s_out = _s5_scan_readout(u_all, xr, xi, hcr, hci, cre_ref, cim_ref, ar_ref, ai_ref, d_ref, wglu_ref,
                             ns=ns, tt=tt)
    c_out = _ln_swish(ytm, cb_ref, lng_ref, lnb_ref)
    _project_out(s_out, c_out, x1_ref, wout_ref, gx_ref, wq_ref, x2_ref, q_ref, dtm, xn, ns=ns, tt=tt)
    _state_out(hre_out, him_out, hcr, hci)

    def write_hist():
        for j in range(CH_TILES):
            buf_out[:, :, _lane_tile(j)] = (
                vext[j, (tt + FIRST_TAP) * ns:(tt + HIST) * ns, :].reshape(CONV_WIDTH - 1, ns, LANES))

    if has_init:
        write_hist()
    else:
        pl.when(pl.program_id(0) == pl.num_programs(0) - 1)(write_hist)
        vext[:, :hist_rows, :] = vext[:, rows:rows + hist_rows, :]


def _mixer(z3, x13, init, w, *, ns, tt):
    nseq, seq_len, _ = z3.shape
    has_init = init is not None
    if has_init:
        assert tt == seq_len and nseq % ns == 0
        grid = (nseq // ns,)
        blk = lambda i: (i, 0, 0)
        col = lambda c: (lambda i: (i, 0, c))
        st_map = lambda i: (i, 0)
        hist_map = lambda i: (0, i, 0)
    else:
        assert ns == nseq and seq_len % tt == 0
        grid = (seq_len // tt,)
        blk = lambda i: (0, i, 0)
        col = lambda c: (lambda i: (0, i, c))
        st_map = lambda i: (0, 0)
        hist_map = lambda i: (0, 0, 0)
    rows = ns * tt
    assert rows % CONV_ROWS == 0 and ns % SUBLANES == 0
    st_spec = pl.BlockSpec((ns, N_STATE), st_map)
    hist_spec = pl.BlockSpec((CONV_WIDTH - 1, ns, CONV_DIM), hist_map)
    row_spec = pl.BlockSpec((ns, tt, D_MODEL), blk)
    weights = [w[k] for k in _S5_WEIGHTS + _CONV_WEIGHTS + _OUT_WEIGHTS]
    in_specs = [pl.BlockSpec((ns, tt, SSM_DIM), col(c)) for c in range(3)] + [row_spec]
    args = [z3, z3, z3, x13]
    if has_init:
        in_specs += [st_spec, st_spec, hist_spec]
        args += list(init)
    in_specs += [_const_spec(a.shape) for a in weights]
    return pl.pallas_call(
        functools.partial(_mixer_kernel, ns=ns, tt=tt, has_init=has_init),
        grid=grid,
        in_specs=in_specs,
        out_specs=(row_spec, row_spec, st_spec, st_spec, hist_spec),
        out_shape=(jax.ShapeDtypeStruct((nseq, seq_len, D_MODEL), F32),
                   jax.ShapeDtypeStruct((nseq, seq_len, D_MODEL), BF16),
                   jax.ShapeDtypeStruct((nseq, N_STATE), F32), jax.ShapeDtypeStruct((nseq, N_STATE), F32),
                   jax.ShapeDtypeStruct((CONV_WIDTH - 1, nseq, CONV_DIM), F32)),
        scratch_shapes=_mixer_scratch(ns, rows) + [
            pltpu.VMEM((CH_TILES, rows, LANES), F32),
            pltpu.VMEM((CH_TILES, rows + HIST * ns, LANES), F32),
            pltpu.VMEM((CH_TILES, rows, LANES), F32)],
        compiler_params=_params(("arbitrary",)),
        name="mixer_init" if has_init else "mixer_zero",
    )(*args, *weights)


def _mixer_tm_kernel(u_ref, c_ref, x1_ref, bblk_ref, cre_ref, cim_ref, ar_ref, ai_ref, d_ref, wglu_ref,
                     wout_ref, gx_ref, wq_ref, x2_ref, q_ref, hre_out, him_out,
                     xr, xi, hcr, hci, dtm, xn, *, ns, tt):
    @pl.when(pl.program_id(0) == 0)
    def _():
        hcr[...] = jnp.zeros_like(hcr)
        hci[...] = jnp.zeros_like(hci)

    for kb in range(CH_TILES):
        x = _dot(u_ref[:, _lane_tile(kb)].astype(BF16), bblk_ref[kb])
        for i in range(TILES_PER_CH):
            xr[kb * TILES_PER_CH + i] = x[:, _lane_tile(i)]
            xi[kb * TILES_PER_CH + i] = x[:, _lane_tile(TILES_PER_CH + i)]
    s_out = _s5_scan_readout(lambda: u_ref[...], xr, xi, hcr, hci, cre_ref, cim_ref, ar_ref, ai_ref,
                             d_ref, wglu_ref, ns=ns, tt=tt)
    _project_out(s_out, c_ref[...], x1_ref, wout_ref, gx_ref, wq_ref, x2_ref, q_ref, dtm, xn, ns=ns, tt=tt)
    _state_out(hre_out, him_out, hcr, hci)


def _mixer_tm(u_tm, c_tm, x13, w, *, tt):
    ns, seq_len, _ = x13.shape
    rows = ns * tt
    weights = [w[k] for k in _S5_WEIGHTS + _OUT_WEIGHTS]
    tm_spec = pl.BlockSpec((rows, SSM_DIM), lambda i: (i, 0))
    row_spec = pl.BlockSpec((ns, tt, D_MODEL), lambda i: (0, i, 0))
    st_spec = pl.BlockSpec((ns, N_STATE), lambda i: (0, 0))
    return pl.pallas_call(
        functools.partial(_mixer_tm_kernel, ns=ns, tt=tt),
        grid=(seq_len // tt,),
        in_specs=[tm_spec, tm_spec, row_spec] + [_const_spec(a.shape) for a in weights],
        out_specs=(row_spec, row_spec, st_spec, st_spec),
        out_shape=(jax.ShapeDtypeStruct((ns, seq_len, D_MODEL), F32),
                   jax.ShapeDtypeStruct((ns, seq_len, D_MODEL), BF16),
                   jax.ShapeDtypeStruct((ns, N_STATE), F32), jax.ShapeDtypeStruct((ns, N_STATE), F32)),
        scratch_shapes=_mixer_scratch(ns, rows),
        compiler_params=_params(("arbitrary",)),
        name="mixer_tm",
    )(u_tm, c_tm, x13, *weights)


def _block_diag_tiles(x):
    _, a, b = x.shape
    x4 = x.reshape(CH_TILES, GROUPS_PER_TILE, a, b)
    eye = jnp.eye(GROUPS_PER_TILE, dtype=x.dtype)
    out = x4[:, :, :, None, :] * eye[None, :, None, :, None]
    return out.reshape(CH_TILES, GROUPS_PER_TILE * a, GROUPS_PER_TILE * b)


def _row(x):
    return x.reshape(1, -1)


def _state_slabs(x):
    return jnp.broadcast_to(x.reshape(STATE_TILES, 1, LANES), (STATE_TILES, SUBLANES, LANES))


def kernel(x_prompt, x_sample, state_ssm_re, state_ssm_im, cache_conv, cache_mem_k, cache_mem_v, mem_prompt,
           g_mem, w_mem_k, w_mem_v, g_ffn1, w_ffn1_gate, w_ffn1_up, w_ffn1_down, g_mix, w_in,
           ssm_a_re, ssm_a_im, ssm_log_dt, ssm_b_re, ssm_b_im, ssm_c_re, ssm_c_im, ssm_d, w_ssm_glu,
           conv_w, conv_b, conv_ln_g, conv_ln_b, w_out, g_xattn, w_mem_q, w_mem_o,
           g_ffn2, w_ffn2_gate, w_ffn2_up, w_ffn2_down, g_final):
    depth = g_ffn1.shape[0]
    assert depth == 1
    l = 0
    bp, seq, _ = x_prompt.shape
    bs, dseq, _ = x_sample.shape
    bf = lambda a: a.astype(BF16)

    ab_re, ab_im, bbar_re, bbar_im = _s5_params(ssm_a_re[l], ssm_a_im[l], ssm_log_dt[l],
                                                jnp.swapaxes(ssm_b_re[l], 1, 2), jnp.swapaxes(ssm_b_im[l], 1, 2))
    mixer_w = dict(
        bblk=bf(jnp.concatenate([_block_diag_tiles(bbar_re), _block_diag_tiles(bbar_im)], axis=-1)),
        cre=bf(_block_diag_tiles(jnp.swapaxes(ssm_c_re[l], 1, 2))),
        cim=bf(-_block_diag_tiles(jnp.swapaxes(ssm_c_im[l], 1, 2))),
        ab_re=_state_slabs(ab_re), ab_im=_state_slabs(ab_im),
        d=_row(ssm_d[l]), w_glu=bf(w_ssm_glu[l]),
        conv_w=jnp.swapaxes(conv_w[l].reshape(CONV_WIDTH, CH_TILES, LANES), 0, 1),
        conv_b=_row(conv_b[l]), ln_g=_row(conv_ln_g[l]), ln_b=_row(conv_ln_b[l]),
        w_out=bf(w_out[l]), g_x=_row(g_xattn[l]), w_q=bf(w_mem_q[l]),
    )
    ffn1 = (_row(g_ffn1[l]), bf(w_ffn1_gate[l]), bf(w_ffn1_up[l]), bf(w_ffn1_down[l]))
    ffn2 = (_row(g_ffn2[l]), bf(w_ffn2_gate[l]), bf(w_ffn2_up[l]), bf(w_ffn2_down[l]))
    g_mix_r, w_in_b, w_o_b, g_final_r = _row(g_mix[l]), bf(w_in[l]), bf(w_mem_o[l]), _row(g_final)

    mk, mv, mk_hm, mv_hm = _mem_kv(mem_prompt.reshape(bp * N_MEM, D_MODEL), _row(g_mem[l]),
                                   bf(w_mem_k[l]), bf(w_mem_v[l]))

    p_rows, s_rows = bp * seq, bs * dseq
    prompt_tt = 64
    x1p, u_tm, c_tm, p_buf = _front(x_prompt, *ffn1, g_mix_r, w_in_b, *[mixer_w[k] for k in _CONV_WEIGHTS],
                                    tt=prompt_tt)
    x2p, qp, p_re, p_im = _mixer_tm(u_tm, c_tm, x1p, mixer_w, tt=prompt_tt)

    x1s, zs = _ffn_in(x_sample.reshape(s_rows, D_MODEL), *ffn1, g_mix_r, w_in_b)
    init_s = (state_ssm_re[l].reshape(bs, N_STATE), state_ssm_im[l].reshape(bs, N_STATE),
              jnp.swapaxes(cache_conv[l], 0, 1))
    x2s, qs, s_re, s_im, s_buf = _mixer(zs.reshape(bs, dseq, -1), x1s.reshape(bs, dseq, D_MODEL), init_s,
                                        mixer_w, ns=32, tt=dseq)

    kv_rows = lambda a: a.reshape(bs * N_MEM, MEM_HEADS, MEM_HEAD_DIM)
    yp, ys = _tail(x2p.reshape(p_rows, D_MODEL), qp.reshape(p_rows, D_MODEL), mk_hm, mv_hm,
                   x2s.reshape(s_rows, D_MODEL), qs.reshape(s_rows, D_MODEL),
                   kv_rows(cache_mem_k), kv_rows(cache_mem_v), w_o_b, *ffn2, g_final_r, tm=256)
    yp = yp.reshape(bp, seq, D_MODEL)
    ys = ys.reshape(bs, dseq, D_MODEL)

    st = lambda a, n: a.reshape(1, n, SSM_GROUPS, SSM_STATE)
    kv = lambda a: a.reshape(1, bp, N_MEM, MEM_HEADS, MEM_HEAD_DIM)
    buf = lambda a: jnp.swapaxes(a, 0, 1)[None]
    return (yp, ys, st(p_re, bp), st(p_im, bp), buf(p_buf), kv(mk), kv(mv),
            st(s_re, bs), st(s_im, bs), buf(s_buf))
```

```python
import functools

import jax
import jax.numpy as jnp
from jax import lax
from jax.experimental import pallas as pl
from jax.experimental.pallas import tpu as pltpu

F32 = jnp.float32
BF16 = jnp.bfloat16

EPS = 1e-6
D_MODEL = 1024
D_FF = 2816
SSM_DIM = 512
CONV_DIM = 512
SSM_GROUPS = 32
SSM_GROUP_CH = 16
SSM_STATE = 64
N_STATE = SSM_GROUPS * SSM_STATE
CONV_WIDTH = 31
N_MEM = 256
MEM_HEADS = 4
MEM_HEAD_DIM = 256

LANES = 128
SUBLANES = 8
STATE_TILES = N_STATE // LANES
CH_TILES = SSM_DIM // LANES
MODEL_TILES = D_MODEL // LANES
GROUPS_PER_TILE = LANES // SSM_GROUP_CH
STATES_PER_CH_TILE = GROUPS_PER_TILE * SSM_STATE
TILES_PER_CH = STATES_PER_CH_TILE // LANES
HIST = 32
FIRST_TAP = HIST - (CONV_WIDTH - 1)
SCAN_TILE_GROUP = 4
CONV_ROWS = 64
VMEM_LIMIT = 56 * 1024 * 1024


def _dot(a, b):
    return jnp.dot(a, b, preferred_element_type=F32)


def _rms(x, g):
    return x * lax.rsqrt(jnp.mean(x * x, axis=-1, keepdims=True) + EPS) * g


def _lane_tile(j):
    return slice(j * LANES, (j + 1) * LANES)


def _const_spec(shape):
    zeros = (0,) * len(shape)
    return pl.BlockSpec(shape, lambda *_: zeros, pipeline_mode=pl.Buffered(1))


def _params(sem):
    return pltpu.CompilerParams(dimension_semantics=sem, vmem_limit_bytes=VMEM_LIMIT)


def _s5_params_kernel(ar_ref, ai_ref, ldt_ref, bre_ref, bim_ref, abr_ref, abi_ref, bbr_ref, bbi_ref):
    lr = ar_ref[...]
    li = ai_ref[...]
    dt = jnp.exp(ldt_ref[...])
    mag = jnp.exp(lr * dt)
    ab_re = mag * jnp.cos(li * dt)
    ab_im = mag * jnp.sin(li * dt)
    den = lr * lr + li * li
    nr = ab_re - 1.0
    ni = ab_im
    coef_re = (nr * lr + ni * li) / den
    coef_im = (ni * lr - nr * li) / den
    abr_ref[...] = ab_re
    abi_ref[...] = ab_im
    b_re = bre_ref[...]
    b_im = bim_ref[...]
    bbr_ref[...] = coef_re * b_re - coef_im * b_im
    bbi_ref[...] = coef_re * b_im + coef_im * b_re


def _s5_params(a_re, a_im, log_dt, b_re, b_im):
    a_shape = jax.ShapeDtypeStruct((SSM_GROUPS, 1, SSM_STATE), F32)
    b_shape = jax.ShapeDtypeStruct(b_re.shape, F32)
    return pl.pallas_call(
        _s5_params_kernel,
        out_shape=(a_shape, a_shape, b_shape, b_shape),
        name="s5_params",
    )(a_re.reshape(a_shape.shape), a_im.reshape(a_shape.shape), log_dt.reshape(SSM_GROUPS, 1, 1), b_re, b_im)


BF16_SUBLANES = 16


def _cast_blocking(w, max_steps):
    n_rows = w.shape[0]
    n_blocks = max(n for n in range(1, max_steps + 1)
                   if n_rows % n == 0 and (n_rows // n) % BF16_SUBLANES == 0)
    return n_rows // n_blocks, n_blocks


def _cast_specs(to_cast, max_steps):
    specs = []
    for w in to_cast:
        blk_rows, n_blocks = _cast_blocking(w, max_steps)
        specs.append(pl.BlockSpec((blk_rows, w.shape[1]),
                                  (lambda n: (lambda j: (jnp.minimum(j, n - 1), 0)))(n_blocks)))
    return specs


def _cast_blocks(cast_in, cast_out):
    for src, dst in zip(cast_in, cast_out):
        dst[...] = src[...].astype(BF16)


def _mem_kv_kernel(*refs, n_cast):
    m_ref, g_ref, wk_ref, wv_ref = refs[:4]
    k_ref, v_ref, kh_ref, vh_ref = refs[4 + n_cast:8 + n_cast]
    _cast_blocks(refs[4:4 + n_cast], refs[8 + n_cast:])
    m = _rms(m_ref[...], g_ref[...]).astype(BF16)
    k = _dot(m, wk_ref[...])
    v = _dot(m, wv_ref[...])
    for h in range(MEM_HEADS):
        cols = slice(h * MEM_HEAD_DIM, (h + 1) * MEM_HEAD_DIM)
        k_ref[:, h, :] = k[:, cols]
        v_ref[:, h, :] = v[:, cols]
        kh_ref[0, h] = k[:, cols].T.astype(BF16)
        vh_ref[0, h] = v[:, cols].astype(BF16)


def _mem_kv(mem2d, g, wk, wv, to_cast):
    rows = mem2d.shape[0]
    nb = rows // N_MEM
    cast_specs = _cast_specs(to_cast, nb)
    row_spec = pl.BlockSpec((N_MEM, D_MODEL), lambda i: (i, 0))
    head_spec = pl.BlockSpec((N_MEM, MEM_HEADS, MEM_HEAD_DIM), lambda i: (i, 0, 0))
    hm_spec = pl.BlockSpec((1, MEM_HEADS, N_MEM, MEM_HEAD_DIM), lambda i: (i, 0, 0, 0))
    out = jax.ShapeDtypeStruct((rows, MEM_HEADS, MEM_HEAD_DIM), F32)
    out_hm = jax.ShapeDtypeStruct((nb, MEM_HEADS, N_MEM, MEM_HEAD_DIM), BF16)
    return pl.pallas_call(
        functools.partial(_mem_kv_kernel, n_cast=len(to_cast)),
        grid=(nb,),
        in_specs=[row_spec, _const_spec((1, D_MODEL)),
                  _const_spec((D_MODEL, D_MODEL)), _const_spec((D_MODEL, D_MODEL))] + cast_specs,
        out_specs=(head_spec, head_spec, hm_spec, hm_spec) + tuple(cast_specs),
        out_shape=(out, out, out_hm, out_hm) + tuple(jax.ShapeDtypeStruct(w.shape, BF16) for w in to_cast),
        compiler_params=_params(("arbitrary",)),
        name="mem_kv",
    )(mem2d, g, wk, wv, *to_cast)


def _zero_after(piece):
    bits = pltpu.bitcast(piece, jnp.uint32)
    return pltpu.bitcast((bits >> 16) >> 16, F32)


def _conv_rows(vext, ytm, cw_ref, j, base, *, ns, after):
    acc = _zero_after(after)
    for k in range(CONV_WIDTH):
        acc = acc + cw_ref[j, k:k + 1, :] * vext[j, pl.ds(base + (FIRST_TAP + k) * ns, CONV_ROWS), :]
    ytm[j, pl.ds(base, CONV_ROWS), :] = acc


def _ln_swish(ytm, cb_ref, lng_ref, lnb_ref):
    yc = jnp.concatenate([ytm[j] for j in range(CH_TILES)], axis=1) + cb_ref[...]
    mu = jnp.mean(yc, axis=-1, keepdims=True)
    var = jnp.mean(jnp.square(yc - mu), axis=-1, keepdims=True)
    yn = (yc - mu) * lax.rsqrt(var + EPS) * lng_ref[...] + lnb_ref[...]
    return yn * jax.nn.sigmoid(yn)


def _swiglu_half(x, g, wg_ref, wu_ref, wd_ref):
    hn = _rms(x, g).astype(BF16)
    gate = _dot(hn, wg_ref[...])
    up = _dot(hn, wu_ref[...])
    act = (gate * jax.nn.sigmoid(gate) * up).astype(BF16)
    return x + 0.5 * _dot(act, wd_ref[...])


def _ffn_in_kernel(x_ref, g1_ref, wg_ref, wu_ref, wd_ref, g2_ref, win_ref, x1_ref, z_ref):
    x1 = _swiglu_half(x_ref[...], g1_ref[...], wg_ref, wu_ref, wd_ref)
    x1_ref[...] = x1
    z_ref[...] = _dot(_rms(x1, g2_ref[...]).astype(BF16), win_ref[...])


def _ffn_in(x2d, g1, wg, wu, wd, g2, win, tm=512):
    rows = x2d.shape[0]
    zdim = win.shape[1]
    row_spec = pl.BlockSpec((tm, D_MODEL), lambda i: (i, 0))
    return pl.pallas_call(
        _ffn_in_kernel,
        grid=(rows // tm,),
        in_specs=[row_spec, _const_spec((1, D_MODEL)),
                  _const_spec((D_MODEL, D_FF)), _const_spec((D_MODEL, D_FF)), _const_spec((D_FF, D_MODEL)),
                  _const_spec((1, D_MODEL)), _const_spec((D_MODEL, zdim))],
        out_specs=(row_spec, pl.BlockSpec((tm, zdim), lambda i: (i, 0))),
        out_shape=(jax.ShapeDtypeStruct((rows, D_MODEL), F32), jax.ShapeDtypeStruct((rows, zdim), F32)),
        compiler_params=_params(("arbitrary",)),
        name="ffn_in",
    )(x2d, g1, wg, wu, wd, g2, win)


def _front_kernel(*refs, ns, tt, steps, n_cast):
    (x_ref, g1_ref, wg_ref, wu_ref, wd_ref, g2_ref, win_ref, cw_ref, cb_ref, lng_ref, lnb_ref) = refs[:11]
    cast_in = refs[11:11 + n_cast]
    x1_ref, u_ref, c_ref, hist_ref = refs[11 + n_cast:15 + n_cast]
    cast_out = refs[15 + n_cast:15 + 2 * n_cast]
    vext, ytm, utm = refs[15 + 2 * n_cast:]
    j = pl.program_id(0)
    rows = ns * tt
    hist_rows = HIST * ns
    gate_tile_step = (D_FF // LANES) // CH_TILES

    @pl.when(j == 0)
    def _():
        vext[...] = jnp.zeros_like(vext)

    x = x_ref[...].reshape(rows, D_MODEL)
    hn = _rms(x, g1_ref[...]).astype(BF16)
    gate = _dot(hn, wg_ref[...])
    for jt in range(CH_TILES):
        for b in range(rows // CONV_ROWS):
            _conv_rows(vext, ytm, cw_ref, jt, b * CONV_ROWS, ns=ns,
                       after=gate[b * CONV_ROWS:(b + 1) * CONV_ROWS, _lane_tile(jt * gate_tile_step)])
    c_ref[...] = _ln_swish(ytm, cb_ref, lng_ref, lnb_ref)
    up = _dot(hn, wu_ref[...])
    act = (gate * jax.nn.sigmoid(gate) * up).astype(BF16)
    x1 = x + 0.5 * _dot(act, wd_ref[...])
    x1_ref[...] = x1.reshape(ns, tt, D_MODEL)
    z = _dot(_rms(x1, g2_ref[...]).astype(BF16), win_ref[...])
    u = z[:, :SSM_DIM]
    v = z[:, SSM_DIM:SSM_DIM + CONV_DIM] * jax.nn.sigmoid(z[:, SSM_DIM + CONV_DIM:])

    vext[:, :hist_rows, :] = vext[:, rows:rows + hist_rows, :]
    for s in range(ns):
        seq = slice(s * tt, (s + 1) * tt)
        for jt in range(CH_TILES):
            vext[jt, pl.ds(hist_rows + s, tt, stride=ns), :] = v[seq, _lane_tile(jt)]
            utm[jt, pl.ds(s, tt, stride=ns), :] = u[seq, _lane_tile(jt)]
    u_ref[...] = jnp.concatenate([utm[jt] for jt in range(CH_TILES)], axis=1)

    _cast_blocks(cast_in, cast_out)

    @pl.when(j == steps)
    def _():
        for jt in range(CH_TILES):
            hist_ref[:, :, _lane_tile(jt)] = (
                vext[jt, FIRST_TAP * ns:hist_rows, :].reshape(CONV_WIDTH - 1, ns, LANES))


def _front(x3, g1, wg, wu, wd, g2, win, cw, cb, lng, lnb, to_cast, *, tt):
    ns, seq_len, _ = x3.shape
    steps = seq_len // tt
    rows = ns * tt
    assert rows % CONV_ROWS == 0 and ns == SUBLANES
    zdim = win.shape[1]
    cur = lambda j: jnp.minimum(j, steps - 1)
    prev = lambda j: jnp.clip(j - 1, 0, steps - 1)
    x_spec = pl.BlockSpec((ns, tt, D_MODEL), lambda j: (0, cur(j), 0))
    tm_spec = lambda m: pl.BlockSpec((rows, SSM_DIM), lambda j: (m(j), 0))
    cast_specs = _cast_specs(to_cast, steps + 1)
    return pl.pallas_call(
        functools.partial(_front_kernel, ns=ns, tt=tt, steps=steps, n_cast=len(to_cast)),
        grid=(steps + 1,),
        in_specs=[x_spec, _const_spec((1, D_MODEL)),
                  _const_spec((D_MODEL, D_FF)), _const_spec((D_MODEL, D_FF)), _const_spec((D_FF, D_MODEL)),
                  _const_spec((1, D_MODEL)), _const_spec((D_MODEL, zdim)),
                  _const_spec(cw.shape), _const_spec(cb.shape), _const_spec(lng.shape), _const_spec(lnb.shape)]
                 + cast_specs,
        out_specs=(x_spec, tm_spec(cur), tm_spec(prev),
                   pl.BlockSpec((CONV_WIDTH - 1, ns, CONV_DIM), lambda j: (0, 0, 0))) + tuple(cast_specs),
        out_shape=(jax.ShapeDtypeStruct((ns, seq_len, D_MODEL), F32),
                   jax.ShapeDtypeStruct((seq_len * ns, SSM_DIM), F32),
                   jax.ShapeDtypeStruct((seq_len * ns, CONV_DIM), F32),
                   jax.ShapeDtypeStruct((CONV_WIDTH - 1, ns, CONV_DIM), F32))
                  + tuple(jax.ShapeDtypeStruct(w.shape, BF16) for w in to_cast),
        scratch_shapes=[pltpu.VMEM((CH_TILES, rows + HIST * ns, LANES), F32),
                        pltpu.VMEM((CH_TILES, rows, LANES), F32),
                        pltpu.VMEM((CH_TILES, rows, LANES), F32)],
        compiler_params=_params(("arbitrary",)),
        name="front",
    )(x3, g1, wg, wu, wd, g2, win, cw, cb, lng, lnb, *to_cast)


def _softmax_rows(sc):
    e = jnp.exp(sc - jnp.max(sc, axis=-1, keepdims=True))
    return e / jnp.sum(e, axis=-1, keepdims=True)


def _attn_long_scores(q_ref, k_ref):
    scale = MEM_HEAD_DIM ** -0.5
    return [_dot(q_ref[:, h * MEM_HEAD_DIM:(h + 1) * MEM_HEAD_DIM], k_ref[0, h]) * scale
            for h in range(MEM_HEADS)]


def _attn_long_values(scores, v_ref):
    return jnp.concatenate([_dot(_softmax_rows(sc).astype(BF16), v_ref[0, h]).astype(BF16)
                            for h, sc in enumerate(scores)], axis=1)


def _attn_short_scores(q_ref, k_ref, *, nseq, lq):
    scale = MEM_HEAD_DIM ** -0.5
    kv_rows = N_MEM * MEM_HEADS
    k2d = k_ref.reshape(nseq * kv_rows, MEM_HEAD_DIM)
    shape = (MEM_HEADS * lq, kv_rows)
    same_head = (lax.broadcasted_iota(jnp.int32, shape, 0) // lq
                 == lax.broadcasted_iota(jnp.int32, shape, 1) % MEM_HEADS)
    scores = []
    for s in range(nseq):
        rows = slice(s * lq, (s + 1) * lq)
        q = jnp.concatenate([q_ref[rows, h * MEM_HEAD_DIM:(h + 1) * MEM_HEAD_DIM] for h in range(MEM_HEADS)],
                            axis=0)
        k = k2d[s * kv_rows:(s + 1) * kv_rows, :].astype(BF16)
        sc = lax.dot_general(q, k, (((1,), (1,)), ((), ())), preferred_element_type=F32) * scale
        scores.append(jnp.where(same_head, sc, -jnp.inf))
    return scores


def _attn_short_values(scores, v_ref, *, lq):
    kv_rows = N_MEM * MEM_HEADS
    v2d = v_ref.reshape(len(scores) * kv_rows, MEM_HEAD_DIM)
    out = []
    for s, sc in enumerate(scores):
        v = v2d[s * kv_rows:(s + 1) * kv_rows, :].astype(BF16)
        o = _dot(_softmax_rows(sc).astype(BF16), v)
        out.append(jnp.concatenate([o[h * lq:(h + 1) * lq, :] for h in range(MEM_HEADS)], axis=1))
    return jnp.concatenate(out, axis=0)


def _tail_kernel(x2a_ref, qa_ref, ka_ref, va_ref, x2b_ref, qb_ref, kb_ref, vb_ref,
                 wo_ref, g_ref, wg_ref, wu_ref, wd_ref, gf_ref, ya_ref, yb_ref, oa_scr, ob_scr,
                 *, a_steps, tm, nseq, lq):
    j = pl.program_id(0)
    in_b = j > a_steps

    @pl.when(j == 0)
    def _():
        oa_scr[...] = jnp.zeros_like(oa_scr)
        ob_scr[...] = jnp.zeros_like(ob_scr)

    b_tile = pl.ds(pl.multiple_of(jnp.maximum(j - a_steps - 1, 0) * tm, tm), tm)
    o = jnp.where(in_b, ob_scr[b_tile, :], oa_scr[...])
    x2 = jnp.where(in_b, x2b_ref[...], x2a_ref[...])

    x3 = x2 + _dot(o, wo_ref[...])
    hn = _rms(x3, g_ref[...]).astype(BF16)
    gate = _dot(hn, wg_ref[...])
    up = _dot(hn, wu_ref[...])
    sc_a = _attn_long_scores(qa_ref, ka_ref)
    sc_b = _attn_short_scores(qb_ref, kb_ref, nseq=nseq, lq=lq)
    act = (gate * jax.nn.sigmoid(gate) * up).astype(BF16)
    x4 = x3 + 0.5 * _dot(act, wd_ref[...])

    b_rows = nseq * lq
    blk = jnp.minimum(j, a_steps - 1)
    ob_scr[pl.ds(pl.multiple_of(blk * b_rows, b_rows), b_rows), :] = (
        _attn_short_values(sc_b, vb_ref, lq=lq).astype(BF16))
    oa_scr[...] = _attn_long_values(sc_a, va_ref)

    y = _rms(x4, gf_ref[...])

    @pl.when(jnp.logical_and(j >= 1, j <= a_steps))
    def _():
        ya_ref[...] = y

    @pl.when(in_b)
    def _():
        yb_ref[...] = y


def _tail(x2a, qa, ka_hm, va_hm, x2b, qb, kb3d, vb3d, wo, g, wg, wu, wd, gf, *, tm):
    a_rows, b_rows = x2a.shape[0], x2b.shape[0]
    a_steps, b_steps = a_rows // tm, b_rows // tm
    n_b = kb3d.shape[0] // N_MEM
    lq = b_rows // n_b
    assert n_b % a_steps == 0
    nseq = n_b // a_steps
    tiles_per_seq = a_rows // ka_hm.shape[0] // tm
    attn_blk = lambda j: jnp.minimum(j, a_steps - 1)
    a_fin = lambda j: (jnp.clip(j - 1, 0, a_steps - 1), 0)
    b_fin = lambda j: (jnp.maximum(j - a_steps - 1, 0), 0)
    row = lambda m: pl.BlockSpec((tm, D_MODEL), m)
    ka_spec = pl.BlockSpec((1, MEM_HEADS, N_MEM, MEM_HEAD_DIM),
                           lambda j: (attn_blk(j) // tiles_per_seq, 0, 0, 0))
    kb_spec = pl.BlockSpec((nseq * N_MEM, MEM_HEADS, MEM_HEAD_DIM), lambda j: (attn_blk(j), 0, 0))
    return pl.pallas_call(
        functools.partial(_tail_kernel, a_steps=a_steps, tm=tm, nseq=nseq, lq=lq),
        grid=(a_steps + 1 + b_steps,),
        in_specs=[row(a_fin), row(lambda j: (attn_blk(j), 0)), ka_spec, ka_spec,
                  row(b_fin), pl.BlockSpec((nseq * lq, D_MODEL), lambda j: (attn_blk(j), 0)), kb_spec, kb_spec,
                  _const_spec((D_MODEL, D_MODEL)), _const_spec((1, D_MODEL)),
                  _const_spec((D_MODEL, D_FF)), _const_spec((D_MODEL, D_FF)), _const_spec((D_FF, D_MODEL)),
                  _const_spec((1, D_MODEL))],
        out_specs=(row(a_fin), row(b_fin)),
        out_shape=(jax.ShapeDtypeStruct((a_rows, D_MODEL), F32), jax.ShapeDtypeStruct((b_rows, D_MODEL), F32)),
        scratch_shapes=[pltpu.VMEM((tm, D_MODEL), BF16), pltpu.VMEM((b_rows, D_MODEL), BF16)],
        compiler_params=_params(("arbitrary",)),
        name="tail",
    )(x2a, qa, ka_hm, va_hm, x2b, qb, kb3d, vb3d, wo, g, wg, wu, wd, gf)


def _s5_input_and_conv(utm, vext, ytm, xr, xi, bblk_ref, cw_ref, *, ns, rows):
    def body(kb, carry):
        x = _dot(utm[kb].astype(BF16), bblk_ref[kb])
        for i in range(TILES_PER_CH):
            xr[kb * TILES_PER_CH + i] = x[:, _lane_tile(i)]
            xi[kb * TILES_PER_CH + i] = x[:, _lane_tile(TILES_PER_CH + i)]
        for b in range(rows // CONV_ROWS):
            _conv_rows(vext, ytm, cw_ref, kb, b * CONV_ROWS, ns=ns,
                       after=x[b * CONV_ROWS:(b + 1) * CONV_ROWS, :LANES])
        return carry

    lax.fori_loop(0, CH_TILES, body, 0)


def _s5_scan_readout(u_all, xr, xi, hcr, hci, cre_ref, cim_ref, ar_ref, ai_ref, d_ref, wglu_ref, *, ns, tt):
    for i in range(ns // SUBLANES):
        srow = slice(i * SUBLANES, (i + 1) * SUBLANES)
        for jg in range(STATE_TILES // SCAN_TILE_GROUP):
            tiles = tuple(range(jg * SCAN_TILE_GROUP, (jg + 1) * SCAN_TILE_GROUP))
            a_r = [ar_ref[j] for j in tiles]
            a_i = [ai_ref[j] for j in tiles]
            init = tuple(hcr[j, srow, :] for j in tiles) + tuple(hci[j, srow, :] for j in tiles)

            def body(t, carry, tiles=tiles, a_r=a_r, a_i=a_i, i=i):
                row = pl.ds(pl.multiple_of(t * ns + i * SUBLANES, SUBLANES), SUBLANES)
                new_r, new_i = [], []
                for idx, j in enumerate(tiles):
                    h_r, h_i = carry[idx], carry[SCAN_TILE_GROUP + idx]
                    n_r = a_r[idx] * h_r - a_i[idx] * h_i + xr[j, row, :]
                    n_i = a_r[idx] * h_i + a_i[idx] * h_r + xi[j, row, :]
                    xr[j, row, :] = n_r
                    xi[j, row, :] = n_i
                    new_r.append(n_r)
                    new_i.append(n_i)
                return tuple(new_r) + tuple(new_i)

            fin = lax.fori_loop(0, tt, body, init, unroll=8)
            for idx, j in enumerate(tiles):
                hcr[j, srow, :] = fin[idx]
                hci[j, srow, :] = fin[SCAN_TILE_GROUP + idx]

    ys = []
    for ob in range(CH_TILES):
        h_re = jnp.concatenate([xr[ob * TILES_PER_CH + i] for i in range(TILES_PER_CH)], axis=1).astype(BF16)
        h_im = jnp.concatenate([xi[ob * TILES_PER_CH + i] for i in range(TILES_PER_CH)], axis=1).astype(BF16)
        ys.append(_dot(h_re, cre_ref[ob]) + _dot(h_im, cim_ref[ob]))
    y = jnp.concatenate(ys, axis=1) + d_ref[...] * u_all()
    g = jax.nn.gelu(y)
    return g * jax.nn.sigmoid(_dot(g.astype(BF16), wglu_ref[...]))


def _project_out(s_out, c_out, x1_ref, wout_ref, gx_ref, wq_ref, x2_ref, q_ref, dtm, xn, *, ns, tt):
    mix = jnp.concatenate([s_out, c_out], axis=1).astype(BF16)
    delta = _dot(mix, wout_ref[...])
    for j in range(MODEL_TILES):
        dtm[j] = delta[:, _lane_tile(j)]
    for s in range(ns):
        d_s = jnp.concatenate([dtm[j, pl.ds(s, tt, stride=ns), :] for j in range(MODEL_TILES)], axis=1)
        x2_s = x1_ref[s] + d_s
        x2_ref[s] = x2_s
        xn[s * tt:(s + 1) * tt, :] = _rms(x2_s, gx_ref[...])
    qv = _dot(xn[...].astype(BF16), wq_ref[...]).astype(BF16)
    for s in range(ns):
        q_ref[s] = qv[s * tt:(s + 1) * tt, :]


def _state_out(hre_out, him_out, hcr, hci):
    hre_out[...] = jnp.concatenate([hcr[j] for j in range(STATE_TILES)], axis=1)
    him_out[...] = jnp.concatenate([hci[j] for j in range(STATE_TILES)], axis=1)


_S5_WEIGHTS = ("bblk", "cre", "cim", "ab_re", "ab_im", "d", "w_glu")
_CONV_WEIGHTS = ("conv_w", "conv_b", "ln_g", "ln_b")
_OUT_WEIGHTS = ("w_out", "g_x", "w_q")


def _mixer_scratch(ns, rows):
    return [pltpu.VMEM((STATE_TILES, rows, LANES), F32),
            pltpu.VMEM((STATE_TILES, rows, LANES), F32),
            pltpu.VMEM((STATE_TILES, ns, LANES), F32),
            pltpu.VMEM((STATE_TILES, ns, LANES), F32),
            pltpu.VMEM((MODEL_TILES, rows, LANES), F32),
            pltpu.VMEM((rows, D_MODEL), F32)]


def _mixer_kernel(*refs, ns, tt, has_init):
    n_in = 7 if has_init else 4
    u_ref, val_ref, gate_ref, x1_ref = refs[:4]
    (bblk_ref, cre_ref, cim_ref, ar_ref, ai_ref, d_ref, wglu_ref,
     cw_ref, cb_ref, lng_ref, lnb_ref, wout_ref, gx_ref, wq_ref) = refs[n_in:n_in + 14]
    x2_ref, q_ref, hre_out, him_out, buf_out = refs[n_in + 14:n_in + 19]
    xr, xi, hcr, hci, dtm, xn, utm, vext, ytm = refs[n_in + 19:]
    rows = ns * tt
    hist_rows = HIST * ns

    if has_init:
        h0r_ref, h0i_ref, cache_ref = refs[4:7]
        for j in range(STATE_TILES):
            hcr[j] = h0r_ref[:, _lane_tile(j)]
            hci[j] = h0i_ref[:, _lane_tile(j)]
        for j in range(CH_TILES):
            vext[j, FIRST_TAP * ns:hist_rows, :] = (
                cache_ref[:, :, _lane_tile(j)].reshape((CONV_WIDTH - 1) * ns, LANES))
    else:
        @pl.when(pl.program_id(0) == 0)
        def _():
            hcr[...] = jnp.zeros_like(hcr)
            hci[...] = jnp.zeros_like(hci)
            vext[:, :hist_rows, :] = jnp.zeros((CH_TILES, hist_rows, LANES), F32)

    for s in range(ns):
        u_s = u_ref[s]
        v_s = val_ref[s] * jax.nn.sigmoid(gate_ref[s])
        for j in range(CH_TILES):
            utm[j, pl.ds(s, tt, stride=ns), :] = u_s[:, _lane_tile(j)]
            vext[j, pl.ds(hist_rows + s, tt, stride=ns), :] = v_s[:, _lane_tile(j)]

    _s5_input_and_conv(utm, vext, ytm, xr, xi, bblk_ref, cw_ref, ns=ns, rows=rows)
    u_all = lambda: jnp.concatenate([utm[j] for j in range(CH_TILES)], axis=1)
    s_out = _s5_scan_readout(u_all, xr, xi, hcr, hci, cre_ref, cim_ref, ar_ref, ai_ref, d_ref, wglu_ref,
                             ns=ns, tt=tt)
    c_out = _ln_swish(ytm, cb_ref, lng_ref, lnb_ref)
    _project_out(s_out, c_out, x1_ref, wout_ref, gx_ref, wq_ref, x2_ref, q_ref, dtm, xn, ns=ns, tt=tt)
    _state_out(hre_out, him_out, hcr, hci)

    def write_hist():
        for j in range(CH_TILES):
            buf_out[:, :, _lane_tile(j)] = (
                vext[j, (tt + FIRST_TAP) * ns:(tt + HIST) * ns, :].reshape(CONV_WIDTH - 1, ns, LANES))

    if has_init:
        write_hist()
    else:
        pl.when(pl.program_id(0) == pl.num_programs(0) - 1)(write_hist)
        vext[:, :hist_rows, :] = vext[:, rows:rows + hist_rows, :]


def _mixer(z3, x13, init, w, *, ns, tt):
    nseq, seq_len, _ = z3.shape
    has_init = init is not None
    if has_init:
        assert tt == seq_len and nseq % ns == 0
        grid = (nseq // ns,)
        blk = lambda i: (i, 0, 0)
        col = lambda c: (lambda i: (i, 0, c))
        st_map = lambda i: (i, 0)
        hist_map = lambda i: (0, i, 0)
    else:
        assert ns == nseq and seq_len % tt == 0
        grid = (seq_len // tt,)
        blk = lambda i: (0, i, 0)
        col = lambda c: (lambda i: (0, i, c))
        st_map = lambda i: (0, 0)
        hist_map = lambda i: (0, 0, 0)
    rows = ns * tt
    assert rows % CONV_ROWS == 0 and ns % SUBLANES == 0
    st_spec = pl.BlockSpec((ns, N_STATE), st_map)
    hist_spec = pl.BlockSpec((CONV_WIDTH - 1, ns, CONV_DIM), hist_map)
    row_spec = pl.BlockSpec((ns, tt, D_MODEL), blk)
    weights = [w[k] for k in _S5_WEIGHTS + _CONV_WEIGHTS + _OUT_WEIGHTS]
    in_specs = [pl.BlockSpec((ns, tt, SSM_DIM), col(c)) for c in range(3)] + [row_spec]
    args = [z3, z3, z3, x13]
    if has_init:
        in_specs += [st_spec, st_spec, hist_spec]
        args += list(init)
    in_specs += [_const_spec(a.shape) for a in weights]
    return pl.pallas_call(
        functools.partial(_mixer_kernel, ns=ns, tt=tt, has_init=has_init),
        grid=grid,
        in_specs=in_specs,
        out_specs=(row_spec, row_spec, st_spec, st_spec, hist_spec),
        out_shape=(jax.ShapeDtypeStruct((nseq, seq_len, D_MODEL), F32),
                   jax.ShapeDtypeStruct((nseq, seq_len, D_MODEL), BF16),
                   jax.ShapeDtypeStruct((nseq, N_STATE), F32), jax.ShapeDtypeStruct((nseq, N_STATE), F32),
                   jax.ShapeDtypeStruct((CONV_WIDTH - 1, nseq, CONV_DIM), F32)),
        scratch_shapes=_mixer_scratch(ns, rows) + [
            pltpu.VMEM((CH_TILES, rows, LANES), F32),
            pltpu.VMEM((CH_TILES, rows + HIST * ns, LANES), F32),
            pltpu.VMEM((CH_TILES, rows, LANES), F32)],
        compiler_params=_params(("arbitrary",)),
        name="mixer_init" if has_init else "mixer_zero",
    )(*args, *weights)


def _mixer_tm_kernel(u_ref, c_ref, x1_ref, bblk_ref, cre_ref, cim_ref, ar_ref, ai_ref, d_ref, wglu_ref,
                     wout_ref, gx_ref, wq_ref, x2_ref, q_ref, hre_out, him_out,
                     xr, xi, hcr, hci, dtm, xn, *, ns, tt):
    @pl.when(pl.program_id(0) == 0)
    def _():
        hcr[...] = jnp.zeros_like(hcr)
        hci[...] = jnp.zeros_like(hci)

    for kb in range(CH_TILES):
        x = _dot(u_ref[:, _lane_tile(kb)].astype(BF16), bblk_ref[kb])
        for i in range(TILES_PER_CH):
            xr[kb * TILES_PER_CH + i] = x[:, _lane_tile(i)]
            xi[kb * TILES_PER_CH + i] = x[:, _lane_tile(TILES_PER_CH + i)]
    s_out = _s5_scan_readout(lambda: u_ref[...], xr, xi, hcr, hci, cre_ref, cim_ref, ar_ref, ai_ref,
                             d_ref, wglu_ref, ns=ns, tt=tt)
    _project_out(s_out, c_ref[...], x1_ref, wout_ref, gx_ref, wq_ref, x2_ref, q_ref, dtm, xn, ns=ns, tt=tt)
    _state_out(hre_out, him_out, hcr, hci)


def _mixer_tm(u_tm, c_tm, x13, w, *, tt):
    ns, seq_len, _ = x13.shape
    rows = ns * tt
    weights = [w[k] for k in _S5_WEIGHTS + _OUT_WEIGHTS]
    tm_spec = pl.BlockSpec((rows, SSM_DIM), lambda i: (i, 0))
    row_spec = pl.BlockSpec((ns, tt, D_MODEL), lambda i: (0, i, 0))
    st_spec = pl.BlockSpec((ns, N_STATE), lambda i: (0, 0))
    return pl.pallas_call(
        functools.partial(_mixer_tm_kernel, ns=ns, tt=tt),
        grid=(seq_len // tt,),
        in_specs=[tm_spec, tm_spec, row_spec] + [_const_spec(a.shape) for a in weights],
        out_specs=(row_spec, row_spec, st_spec, st_spec),
        out_shape=(jax.ShapeDtypeStruct((ns, seq_len, D_MODEL), F32),
                   jax.ShapeDtypeStruct((ns, seq_len, D_MODEL), BF16),
                   jax.ShapeDtypeStruct((ns, N_STATE), F32), jax.ShapeDtypeStruct((ns, N_STATE), F32)),
        scratch_shapes=_mixer_scratch(ns, rows),
        compiler_params=_params(("arbitrary",)),
        name="mixer_tm",
    )(u_tm, c_tm, x13, *weights)


def _block_diag_tiles(x):
    _, a, b = x.shape
    x4 = x.reshape(CH_TILES, GROUPS_PER_TILE, a, b)
    eye = jnp.eye(GROUPS_PER_TILE, dtype=x.dtype)
    out = x4[:, :, :, None, :] * eye[None, :, None, :, None]
    return out.reshape(CH_TILES, GROUPS_PER_TILE * a, GROUPS_PER_TILE * b)


def _row(x):
    return x.reshape(1, -1)


def _state_slabs(x):
    return jnp.broadcast_to(x.reshape(STATE_TILES, 1, LANES), (STATE_TILES, SUBLANES, LANES))


def kernel(x_prompt, x_sample, state_ssm_re, state_ssm_im, cache_conv, cache_mem_k, cache_mem_v, mem_prompt,
           g_mem, w_mem_k, w_mem_v, g_ffn1, w_ffn1_gate, w_ffn1_up, w_ffn1_down, g_mix, w_in,
           ssm_a_re, ssm_a_im, ssm_log_dt, ssm_b_re, ssm_b_im, ssm_c_re, ssm_c_im, ssm_d, w_ssm_glu,
           conv_w, conv_b, conv_ln_g, conv_ln_b, w_out, g_xattn, w_mem_q, w_mem_o,
           g_ffn2, w_ffn2_gate, w_ffn2_up, w_ffn2_down, g_final):
    depth = g_ffn1.shape[0]
    assert depth == 1
    l = 0
    bp, seq, _ = x_prompt.shape
    bs, dseq, _ = x_sample.shape
    bf = lambda a: a.astype(BF16)

    ab_re, ab_im, bbar_re, bbar_im = _s5_params(ssm_a_re[l], ssm_a_im[l], ssm_log_dt[l],
                                                jnp.swapaxes(ssm_b_re[l], 1, 2), jnp.swapaxes(ssm_b_im[l], 1, 2))
    mixer_w = dict(
        bblk=bf(jnp.concatenate([_block_diag_tiles(bbar_re), _block_diag_tiles(bbar_im)], axis=-1)),
        cre=bf(_block_diag_tiles(jnp.swapaxes(ssm_c_re[l], 1, 2))),
        cim=bf(-_block_diag_tiles(jnp.swapaxes(ssm_c_im[l], 1, 2))),
        ab_re=_state_slabs(ab_re), ab_im=_state_slabs(ab_im),
        d=_row(ssm_d[l]),
        conv_w=jnp.swapaxes(conv_w[l].reshape(CONV_WIDTH, CH_TILES, LANES), 0, 1),
        conv_b=_row(conv_b[l]), ln_g=_row(conv_ln_g[l]), ln_b=_row(conv_ln_b[l]),
        g_x=_row(g_xattn[l]),
    )
    g_mix_r, g_final_r = _row(g_mix[l]), _row(g_final)

    first = [w_ffn1_gate[l], w_ffn1_up[l], w_ffn1_down[l], w_in[l], w_ssm_glu[l]]
    mk, mv, mk_hm, mv_hm, w1g_b, w1u_b, w1d_b, w_in_b, w_glu_b = _mem_kv(
        mem_prompt.reshape(bp * N_MEM, D_MODEL), _row(g_mem[l]), bf(w_mem_k[l]), bf(w_mem_v[l]), first)
    ffn1 = (_row(g_ffn1[l]), w1g_b, w1u_b, w1d_b)
    mixer_w["w_glu"] = w_glu_b

    p_rows, s_rows = bp * seq, bs * dseq
    prompt_tt = 64
    later = [w_out[l], w_mem_q[l], w_mem_o[l], w_ffn2_gate[l], w_ffn2_up[l], w_ffn2_down[l]]
    x1p, u_tm, c_tm, p_buf, w_out_b, w_q_b, w_o_b, w2g_b, w2u_b, w2d_b = _front(
        x_prompt, *ffn1, g_mix_r, w_in_b, *[mixer_w[k] for k in _CONV_WEIGHTS], later, tt=prompt_tt)
    mixer_w.update(w_out=w_out_b, w_q=w_q_b)
    ffn2 = (_row(g_ffn2[l]), w2g_b, w2u_b, w2d_b)
    x2p, qp, p_re, p_im = _mixer_tm(u_tm, c_tm, x1p, mixer_w, tt=prompt_tt)

    x1s, zs = _ffn_in(x_sample.reshape(s_rows, D_MODEL), *ffn1, g_mix_r, w_in_b)
    init_s = (state_ssm_re[l].reshape(bs, N_STATE), state_ssm_im[l].reshape(bs, N_STATE),
              jnp.swapaxes(cache_conv[l], 0, 1))
    x2s, qs, s_re, s_im, s_buf = _mixer(zs.reshape(bs, dseq, -1), x1s.reshape(bs, dseq, D_MODEL), init_s,
                                        mixer_w, ns=32, tt=dseq)

    kv_rows = lambda a: a.reshape(bs * N_MEM, MEM_HEADS, MEM_HEAD_DIM)
    yp, ys = _tail(x2p.reshape(p_rows, D_MODEL), qp.reshape(p_rows, D_MODEL), mk_hm, mv_hm,
                   x2s.reshape(s_rows, D_MODEL), qs.reshape(s_rows, D_MODEL),
                   kv_rows(cache_mem_k), kv_rows(cache_mem_v), w_o_b, *ffn2, g_final_r, tm=256)
    yp = yp.reshape(bp, seq, D_MODEL)
    ys = ys.reshape(bs, dseq, D_MODEL)

    st = lambda a, n: a.reshape(1, n, SSM_GROUPS, SSM_STATE)
    kv = lambda a: a.reshape(1, bp, N_MEM, MEM_HEADS, MEM_HEAD_DIM)
    buf = lambda a: jnp.swapaxes(a, 0, 1)[None]
    return (yp, ys, st(p_re, bp), st(p_im, bp), buf(p_buf), kv(mk), kv(mv),
            st(s_re, bs), st(s_im, bs), buf(s_buf))
```

```python
import functools

import jax
import jax.numpy as jnp
from jax import lax
from jax.experimental import pallas as pl
from jax.experimental.pallas import tpu as pltpu

F32 = jnp.float32
BF16 = jnp.bfloat16

EPS = 1e-6
D_MODEL = 1024
D_FF = 2816
SSM_DIM = 512
CONV_DIM = 512
SSM_GROUPS = 32
SSM_GROUP_CH = 16
SSM_STATE = 64
N_STATE = SSM_GROUPS * SSM_STATE
CONV_WIDTH = 31
N_MEM = 256
MEM_HEADS = 4
MEM_HEAD_DIM = 256

LANES = 128
SUBLANES = 8
STATE_TILES = N_STATE // LANES
CH_TILES = SSM_DIM // LANES
MODEL_TILES = D_MODEL // LANES
GROUPS_PER_TILE = LANES // SSM_GROUP_CH
STATES_PER_CH_TILE = GROUPS_PER_TILE * SSM_STATE
TILES_PER_CH = STATES_PER_CH_TILE // LANES
HIST = 32
FIRST_TAP = HIST - (CONV_WIDTH - 1)
SCAN_TILE_GROUP = 4
CONV_ROWS = 64
VMEM_LIMIT = 56 * 1024 * 1024


def _dot(a, b):
    return jnp.dot(a, b, preferred_element_type=F32)


def _rms(x, g):
    return x * lax.rsqrt(jnp.mean(x * x, axis=-1, keepdims=True) + EPS) * g


def _lane_tile(j):
    return slice(j * LANES, (j + 1) * LANES)


def _const_spec(shape):
    zeros = (0,) * len(shape)
    return pl.BlockSpec(shape, lambda *_: zeros, pipeline_mode=pl.Buffered(1))


def _params(sem):
    return pltpu.CompilerParams(dimension_semantics=sem, vmem_limit_bytes=VMEM_LIMIT)


def _s5_params_kernel(ar_ref, ai_ref, ldt_ref, bre_ref, bim_ref, abr_ref, abi_ref, bbr_ref, bbi_ref):
    lr = ar_ref[...]
    li = ai_ref[...]
    dt = jnp.exp(ldt_ref[...])
    mag = jnp.exp(lr * dt)
    ab_re = mag * jnp.cos(li * dt)
    ab_im = mag * jnp.sin(li * dt)
    den = lr * lr + li * li
    nr = ab_re - 1.0
    ni = ab_im
    coef_re = (nr * lr + ni * li) / den
    coef_im = (ni * lr - nr * li) / den
    abr_ref[...] = ab_re
    abi_ref[...] = ab_im
    b_re = bre_ref[...]
    b_im = bim_ref[...]
    bbr_ref[...] = coef_re * b_re - coef_im * b_im
    bbi_ref[...] = coef_re * b_im + coef_im * b_re


def _s5_params(a_re, a_im, log_dt, b_re, b_im):
    a_shape = jax.ShapeDtypeStruct((SSM_GROUPS, 1, SSM_STATE), F32)
    b_shape = jax.ShapeDtypeStruct(b_re.shape, F32)
    return pl.pallas_call(
        _s5_params_kernel,
        out_shape=(a_shape, a_shape, b_shape, b_shape),
        name="s5_params",
    )(a_re.reshape(a_shape.shape), a_im.reshape(a_shape.shape), log_dt.reshape(SSM_GROUPS, 1, 1), b_re, b_im)


BF16_SUBLANES = 16


def _cast_blocking(w, max_steps):
    n_rows = w.shape[0]
    n_blocks = max(n for n in range(1, max_steps + 1)
                   if n_rows % n == 0 and (n_rows // n) % BF16_SUBLANES == 0)
    return n_rows // n_blocks, n_blocks


def _cast_specs(to_cast, max_steps):
    specs = []
    for w in to_cast:
        blk_rows, n_blocks = _cast_blocking(w, max_steps)
        specs.append(pl.BlockSpec((blk_rows, w.shape[1]),
                                  (lambda n: (lambda j: (jnp.minimum(j, n - 1), 0)))(n_blocks)))
    return specs


def _cast_blocks(cast_in, cast_out):
    for src, dst in zip(cast_in, cast_out):
        dst[...] = src[...].astype(BF16)


def _mem_kv_kernel(*refs, n_cast):
    m_ref, g_ref, wk_ref, wv_ref = refs[:4]
    k_ref, v_ref, kh_ref, vh_ref = refs[4 + n_cast:8 + n_cast]
    _cast_blocks(refs[4:4 + n_cast], refs[8 + n_cast:])
    m = _rms(m_ref[...], g_ref[...]).astype(BF16)
    k = _dot(m, wk_ref[...])
    v = _dot(m, wv_ref[...])
    for h in range(MEM_HEADS):
        cols = slice(h * MEM_HEAD_DIM, (h + 1) * MEM_HEAD_DIM)
        k_ref[:, h, :] = k[:, cols]
        v_ref[:, h, :] = v[:, cols]
        kh_ref[0, h] = k[:, cols].T.astype(BF16)
        vh_ref[0, h] = v[:, cols].astype(BF16)


def _mem_kv(mem2d, g, wk, wv, to_cast):
    rows = mem2d.shape[0]
    nb = rows // N_MEM
    cast_specs = _cast_specs(to_cast, nb)
    row_spec = pl.BlockSpec((N_MEM, D_MODEL), lambda i: (i, 0))
    head_spec = pl.BlockSpec((N_MEM, MEM_HEADS, MEM_HEAD_DIM), lambda i: (i, 0, 0))
    hm_spec = pl.BlockSpec((1, MEM_HEADS, N_MEM, MEM_HEAD_DIM), lambda i: (i, 0, 0, 0))
    out = jax.ShapeDtypeStruct((rows, MEM_HEADS, MEM_HEAD_DIM), F32)
    out_hm = jax.ShapeDtypeStruct((nb, MEM_HEADS, N_MEM, MEM_HEAD_DIM), BF16)
    return pl.pallas_call(
        functools.partial(_mem_kv_kernel, n_cast=len(to_cast)),
        grid=(nb,),
        in_specs=[row_spec, _const_spec((1, D_MODEL)),
                  _const_spec((D_MODEL, D_MODEL)), _const_spec((D_MODEL, D_MODEL))] + cast_specs,
        out_specs=(head_spec, head_spec, hm_spec, hm_spec) + tuple(cast_specs),
        out_shape=(out, out, out_hm, out_hm) + tuple(jax.ShapeDtypeStruct(w.shape, BF16) for w in to_cast),
        compiler_params=_params(("arbitrary",)),
        name="mem_kv",
    )(mem2d, g, wk, wv, *to_cast)


def _zero_after(piece):
    bits = pltpu.bitcast(piece, jnp.uint32)
    return pltpu.bitcast((bits >> 16) >> 16, F32)


def _conv_rows(vext, ytm, cw_ref, j, base, *, ns, after):
    acc = _zero_after(after)
    for k in range(CONV_WIDTH):
        acc = acc + cw_ref[j, k:k + 1, :] * vext[j, pl.ds(base + (FIRST_TAP + k) * ns, CONV_ROWS), :]
    ytm[j, pl.ds(base, CONV_ROWS), :] = acc


def _ln_swish(ytm, cb_ref, lng_ref, lnb_ref):
    yc = jnp.concatenate([ytm[j] for j in range(CH_TILES)], axis=1) + cb_ref[...]
    mu = jnp.mean(yc, axis=-1, keepdims=True)
    var = jnp.mean(jnp.square(yc - mu), axis=-1, keepdims=True)
    yn = (yc - mu) * lax.rsqrt(var + EPS) * lng_ref[...] + lnb_ref[...]
    return yn * jax.nn.sigmoid(yn)


def _swiglu_half(x, g, wg_ref, wu_ref, wd_ref):
    hn = _rms(x, g).astype(BF16)
    gate = _dot(hn, wg_ref[...])
    up = _dot(hn, wu_ref[...])
    act = (gate * jax.nn.sigmoid(gate) * up).astype(BF16)
    return x + 0.5 * _dot(act, wd_ref[...])


def _ffn_in_kernel(x_ref, g1_ref, wg_ref, wu_ref, wd_ref, g2_ref, win_ref, x1_ref, z_ref):
    x1 = _swiglu_half(x_ref[...], g1_ref[...], wg_ref, wu_ref, wd_ref)
    x1_ref[...] = x1
    z_ref[...] = _dot(_rms(x1, g2_ref[...]).astype(BF16), win_ref[...])


def _ffn_in(x2d, g1, wg, wu, wd, g2, win, tm=512):
    rows = x2d.shape[0]
    zdim = win.shape[1]
    row_spec = pl.BlockSpec((tm, D_MODEL), lambda i: (i, 0))
    return pl.pallas_call(
        _ffn_in_kernel,
        grid=(rows // tm,),
        in_specs=[row_spec, _const_spec((1, D_MODEL)),
                  _const_spec((D_MODEL, D_FF)), _const_spec((D_MODEL, D_FF)), _const_spec((D_FF, D_MODEL)),
                  _const_spec((1, D_MODEL)), _const_spec((D_MODEL, zdim))],
        out_specs=(row_spec, pl.BlockSpec((tm, zdim), lambda i: (i, 0))),
        out_shape=(jax.ShapeDtypeStruct((rows, D_MODEL), F32), jax.ShapeDtypeStruct((rows, zdim), F32)),
        compiler_params=_params(("arbitrary",)),
        name="ffn_in",
    )(x2d, g1, wg, wu, wd, g2, win)


def _front_kernel(*refs, ns, tt, steps, n_cast):
    (x_ref, g1_ref, wg_ref, wu_ref, wd_ref, g2_ref, win_ref, cw_ref, cb_ref, lng_ref, lnb_ref) = refs[:11]
    cast_in = refs[11:11 + n_cast]
    x1_ref, u_ref, c_ref, hist_ref = refs[11 + n_cast:15 + n_cast]
    cast_out = refs[15 + n_cast:15 + 2 * n_cast]
    vext, ytm, utm = refs[15 + 2 * n_cast:]
    j = pl.program_id(0)
    rows = ns * tt
    hist_rows = HIST * ns
    gate_tile_step = (D_FF // LANES) // CH_TILES

    @pl.when(j == 0)
    def _():
        vext[...] = jnp.zeros_like(vext)

    x = x_ref[...].reshape(rows, D_MODEL)
    hn = _rms(x, g1_ref[...]).astype(BF16)
    gate = _dot(hn, wg_ref[...])
    for jt in range(CH_TILES):
        for b in range(rows // CONV_ROWS):
            _conv_rows(vext, ytm, cw_ref, jt, b * CONV_ROWS, ns=ns,
                       after=gate[b * CONV_ROWS:(b + 1) * CONV_ROWS, _lane_tile(jt * gate_tile_step)])
    c_ref[...] = _ln_swish(ytm, cb_ref, lng_ref, lnb_ref)
    up = _dot(hn, wu_ref[...])
    act = (gate * jax.nn.sigmoid(gate) * up).astype(BF16)
    x1 = x + 0.5 * _dot(act, wd_ref[...])
    x1_ref[...] = x1.reshape(ns, tt, D_MODEL)
    z = _dot(_rms(x1, g2_ref[...]).astype(BF16), win_ref[...])
    u = z[:, :SSM_DIM]
    v = z[:, SSM_DIM:SSM_DIM + CONV_DIM] * jax.nn.sigmoid(z[:, SSM_DIM + CONV_DIM:])

    vext[:, :hist_rows, :] = vext[:, rows:rows + hist_rows, :]
    for s in range(ns):
        seq = slice(s * tt, (s + 1) * tt)
        for jt in range(CH_TILES):
            vext[jt, pl.ds(hist_rows + s, tt, stride=ns), :] = v[seq, _lane_tile(jt)]
            utm[jt, pl.ds(s, tt, stride=ns), :] = u[seq, _lane_tile(jt)]
    u_ref[...] = jnp.concatenate([utm[jt] for jt in range(CH_TILES)], axis=1)

    _cast_blocks(cast_in, cast_out)

    @pl.when(j == steps)
    def _():
        for jt in range(CH_TILES):
            hist_ref[:, :, _lane_tile(jt)] = (
                vext[jt, FIRST_TAP * ns:hist_rows, :].reshape(CONV_WIDTH - 1, ns, LANES))


def _front(x3, g1, wg, wu, wd, g2, win, cw, cb, lng, lnb, to_cast, *, tt):
    ns, seq_len, _ = x3.shape
    steps = seq_len // tt
    rows = ns * tt
    assert rows % CONV_ROWS == 0 and ns == SUBLANES
    zdim = win.shape[1]
    cur = lambda j: jnp.minimum(j, steps - 1)
    prev = lambda j: jnp.clip(j - 1, 0, steps - 1)
    x_spec = pl.BlockSpec((ns, tt, D_MODEL), lambda j: (0, cur(j), 0))
    tm_spec = lambda m: pl.BlockSpec((rows, SSM_DIM), lambda j: (m(j), 0))
    cast_specs = _cast_specs(to_cast, steps + 1)
    return pl.pallas_call(
        functools.partial(_front_kernel, ns=ns, tt=tt, steps=steps, n_cast=len(to_cast)),
        grid=(steps + 1,),
        in_specs=[x_spec, _const_spec((1, D_MODEL)),
                  _const_spec((D_MODEL, D_FF)), _const_spec((D_MODEL, D_FF)), _const_spec((D_FF, D_MODEL)),
                  _const_spec((1, D_MODEL)), _const_spec((D_MODEL, zdim)),
                  _const_spec(cw.shape), _const_spec(cb.shape), _const_spec(lng.shape), _const_spec(lnb.shape)]
                 + cast_specs,
        out_specs=(x_spec, tm_spec(cur), tm_spec(prev),
                   pl.BlockSpec((CONV_WIDTH - 1, ns, CONV_DIM), lambda j: (0, 0, 0))) + tuple(cast_specs),
        out_shape=(jax.ShapeDtypeStruct((ns, seq_len, D_MODEL), F32),
                   jax.ShapeDtypeStruct((seq_len * ns, SSM_DIM), F32),
                   jax.ShapeDtypeStruct((seq_len * ns, CONV_DIM), F32),
                   jax.ShapeDtypeStruct((CONV_WIDTH - 1, ns, CONV_DIM), F32))
                  + tuple(jax.ShapeDtypeStruct(w.shape, BF16) for w in to_cast),
        scratch_shapes=[pltpu.VMEM((CH_TILES, rows + HIST * ns, LANES), F32),
                        pltpu.VMEM((CH_TILES, rows, LANES), F32),
                        pltpu.VMEM((CH_TILES, rows, LANES), F32)],
        compiler_params=_params(("arbitrary",)),
        name="front",
    )(x3, g1, wg, wu, wd, g2, win, cw, cb, lng, lnb, *to_cast)


def _softmax_rows(sc):
    e = jnp.exp(sc - jnp.max(sc, axis=-1, keepdims=True))
    return e / jnp.sum(e, axis=-1, keepdims=True)


def _attn_long_scores(q_ref, k_ref):
    scale = MEM_HEAD_DIM ** -0.5
    return [_dot(q_ref[:, h * MEM_HEAD_DIM:(h + 1) * MEM_HEAD_DIM], k_ref[0, h]) * scale
            for h in range(MEM_HEADS)]


def _attn_long_values(scores, v_ref):
    return jnp.concatenate([_dot(_softmax_rows(sc).astype(BF16), v_ref[0, h]).astype(BF16)
                            for h, sc in enumerate(scores)], axis=1)


def _attn_short_scores(q_ref, k_ref, *, nseq, lq):
    scale = MEM_HEAD_DIM ** -0.5
    kv_rows = N_MEM * MEM_HEADS
    k2d = k_ref.reshape(nseq * kv_rows, MEM_HEAD_DIM)
    shape = (MEM_HEADS * lq, kv_rows)
    same_head = (lax.broadcasted_iota(jnp.int32, shape, 0) // lq
                 == lax.broadcasted_iota(jnp.int32, shape, 1) % MEM_HEADS)
    scores = []
    for s in range(nseq):
        rows = slice(s * lq, (s + 1) * lq)
        q = jnp.concatenate([q_ref[rows, h * MEM_HEAD_DIM:(h + 1) * MEM_HEAD_DIM] for h in range(MEM_HEADS)],
                            axis=0)
        k = k2d[s * kv_rows:(s + 1) * kv_rows, :].astype(BF16)
        sc = lax.dot_general(q, k, (((1,), (1,)), ((), ())), preferred_element_type=F32) * scale
        scores.append(jnp.where(same_head, sc, -jnp.inf))
    return scores


def _attn_short_values(scores, v_ref, *, lq):
    kv_rows = N_MEM * MEM_HEADS
    v2d = v_ref.reshape(len(scores) * kv_rows, MEM_HEAD_DIM)
    out = []
    for s, sc in enumerate(scores):
        v = v2d[s * kv_rows:(s + 1) * kv_rows, :].astype(BF16)
        o = _dot(_softmax_rows(sc).astype(BF16), v)
        out.append(jnp.concatenate([o[h * lq:(h + 1) * lq, :] for h in range(MEM_HEADS)], axis=1))
    return jnp.concatenate(out, axis=0)


def _tail_kernel(x2a_ref, qa_ref, ka_ref, va_ref, x2b_ref, qb_ref, kb_ref, vb_ref,
                 wo_ref, g_ref, wg_ref, wu_ref, wd_ref, gf_ref, ya_ref, yb_ref, oa_scr, ob_scr,
                 *, a_steps, tm, nseq, lq):
    j = pl.program_id(0)
    in_b = j > a_steps

    @pl.when(j == 0)
    def _():
        oa_scr[...] = jnp.zeros_like(oa_scr)
        ob_scr[...] = jnp.zeros_like(ob_scr)

    b_tile = pl.ds(pl.multiple_of(jnp.maximum(j - a_steps - 1, 0) * tm, tm), tm)
    o = jnp.where(in_b, ob_scr[b_tile, :], oa_scr[...])
    x2 = jnp.where(in_b, x2b_ref[...], x2a_ref[...])

    x3 = x2 + _dot(o, wo_ref[...])
    hn = _rms(x3, g_ref[...]).astype(BF16)
    gate = _dot(hn, wg_ref[...])
    up = _dot(hn, wu_ref[...])
    sc_a = _attn_long_scores(qa_ref, ka_ref)
    sc_b = _attn_short_scores(qb_ref, kb_ref, nseq=nseq, lq=lq)
    act = (gate * jax.nn.sigmoid(gate) * up).astype(BF16)
    x4 = x3 + 0.5 * _dot(act, wd_ref[...])

    b_rows = nseq * lq
    blk = jnp.minimum(j, a_steps - 1)
    ob_scr[pl.ds(pl.multiple_of(blk * b_rows, b_rows), b_rows), :] = (
        _attn_short_values(sc_b, vb_ref, lq=lq).astype(BF16))
    oa_scr[...] = _attn_long_values(sc_a, va_ref)

    y = _rms(x4, gf_ref[...])

    @pl.when(jnp.logical_and(j >= 1, j <= a_steps))
    def _():
        ya_ref[...] = y

    @pl.when(in_b)
    def _():
        yb_ref[...] = y


def _tail(x2a, qa, ka_hm, va_hm, x2b, qb, kb3d, vb3d, wo, g, wg, wu, wd, gf, *, tm):
    a_rows, b_rows = x2a.shape[0], x2b.shape[0]
    a_steps, b_steps = a_rows // tm, b_rows // tm
    n_b = kb3d.shape[0] // N_MEM
    lq = b_rows // n_b
    assert n_b % a_steps == 0
    nseq = n_b // a_steps
    tiles_per_seq = a_rows // ka_hm.shape[0] // tm
    attn_blk = lambda j: jnp.minimum(j, a_steps - 1)
    a_fin = lambda j: (jnp.clip(j - 1, 0, a_steps - 1), 0)
    b_fin = lambda j: (jnp.maximum(j - a_steps - 1, 0), 0)
    row = lambda m: pl.BlockSpec((tm, D_MODEL), m)
    ka_spec = pl.BlockSpec((1, MEM_HEADS, N_MEM, MEM_HEAD_DIM),
                           lambda j: (attn_blk(j) // tiles_per_seq, 0, 0, 0))
    kb_spec = pl.BlockSpec((nseq * N_MEM, MEM_HEADS, MEM_HEAD_DIM), lambda j: (attn_blk(j), 0, 0))
    return pl.pallas_call(
        functools.partial(_tail_kernel, a_steps=a_steps, tm=tm, nseq=nseq, lq=lq),
        grid=(a_steps + 1 + b_steps,),
        in_specs=[row(a_fin), row(lambda j: (attn_blk(j), 0)), ka_spec, ka_spec,
                  row(b_fin), pl.BlockSpec((nseq * lq, D_MODEL), lambda j: (attn_blk(j), 0)), kb_spec, kb_spec,
                  _const_spec((D_MODEL, D_MODEL)), _const_spec((1, D_MODEL)),
                  _const_spec((D_MODEL, D_FF)), _const_spec((D_MODEL, D_FF)), _const_spec((D_FF, D_MODEL)),
                  _const_spec((1, D_MODEL))],
        out_specs=(row(a_fin), row(b_fin)),
        out_shape=(jax.ShapeDtypeStruct((a_rows, D_MODEL), F32), jax.ShapeDtypeStruct((b_rows, D_MODEL), F32)),
        scratch_shapes=[pltpu.VMEM((tm, D_MODEL), BF16), pltpu.VMEM((b_rows, D_MODEL), BF16)],
        compiler_params=_params(("arbitrary",)),
        name="tail",
    )(x2a, qa, ka_hm, va_hm, x2b, qb, kb3d, vb3d, wo, g, wg, wu, wd, gf)


def _s5_input_and_conv(utm, vext, ytm, xr, xi, bblk_ref, cw_ref, *, ns, rows):
    def body(kb, carry):
        x = _dot(utm[kb].astype(BF16), bblk_ref[kb])
        for i in range(TILES_PER_CH):
            xr[kb * TILES_PER_CH + i] = x[:, _lane_tile(i)]
            xi[kb * TILES_PER_CH + i] = x[:, _lane_tile(TILES_PER_CH + i)]
        for b in range(rows // CONV_ROWS):
            _conv_rows(vext, ytm, cw_ref, kb, b * CONV_ROWS, ns=ns,
                       after=x[b * CONV_ROWS:(b + 1) * CONV_ROWS, :LANES])
        return carry

    lax.fori_loop(0, CH_TILES, body, 0)


SCAN_UNROLL = 8


def _s5_scan(xr, xi, hcr, hci, ar_ref, ai_ref, *, ns, tt, after=None):
    def step(row, carry, tiles, a_r, a_i):
        new_r, new_i = [], []
        for idx, j in enumerate(tiles):
            h_r, h_i = carry[idx], carry[SCAN_TILE_GROUP + idx]
            n_r = a_r[idx] * h_r - a_i[idx] * h_i + xr[j, row, :]
            n_i = a_r[idx] * h_i + a_i[idx] * h_r + xi[j, row, :]
            xr[j, row, :] = n_r
            xi[j, row, :] = n_i
            new_r.append(n_r)
            new_i.append(n_i)
        return tuple(new_r) + tuple(new_i)

    for i in range(ns // SUBLANES):
        srow = slice(i * SUBLANES, (i + 1) * SUBLANES)
        for jg in range(STATE_TILES // SCAN_TILE_GROUP):
            tiles = tuple(range(jg * SCAN_TILE_GROUP, (jg + 1) * SCAN_TILE_GROUP))
            a_r = [ar_ref[j] for j in tiles]
            a_i = [ai_ref[j] for j in tiles]
            carry = tuple(hcr[j, srow, :] for j in tiles) + tuple(hci[j, srow, :] for j in tiles)
            if after is None:
                def body(t, c, tiles=tiles, a_r=a_r, a_i=a_i, i=i):
                    row = pl.ds(pl.multiple_of(t * ns + i * SUBLANES, SUBLANES), SUBLANES)
                    return step(row, c, tiles, a_r, a_i)

                carry = lax.fori_loop(0, tt, body, carry, unroll=SCAN_UNROLL)
            else:
                for t in range(tt):
                    if t % SCAN_UNROLL == 0:
                        zero = _zero_after(after(t // SCAN_UNROLL, jg))
                        carry = tuple(c + zero for c in carry)
                    carry = step(pl.ds(t * ns + i * SUBLANES, SUBLANES), carry, tiles, a_r, a_i)
            for idx, j in enumerate(tiles):
                hcr[j, srow, :] = carry[idx]
                hci[j, srow, :] = carry[SCAN_TILE_GROUP + idx]


def _s5_readout(u_all, xr, xi, cre_ref, cim_ref, d_ref, wglu_ref):
    ys = []
    for ob in range(CH_TILES):
        h_re = jnp.concatenate([xr[ob * TILES_PER_CH + i] for i in range(TILES_PER_CH)], axis=1).astype(BF16)
        h_im = jnp.concatenate([xi[ob * TILES_PER_CH + i] for i in range(TILES_PER_CH)], axis=1).astype(BF16)
        ys.append(_dot(h_re, cre_ref[ob]) + _dot(h_im, cim_ref[ob]))
    y = jnp.concatenate(ys, axis=1) + d_ref[...] * u_all
    g = jax.nn.gelu(y)
    return g * jax.nn.sigmoid(_dot(g.astype(BF16), wglu_ref[...]))


def _project_delta(s_out, c_out, wout_ref):
    return _dot(jnp.concatenate([s_out, c_out], axis=1).astype(BF16), wout_ref[...])


def _finish_projection(delta, x1_ref, gx_ref, wq_ref, x2_ref, q_ref, dtm, xn, *, ns, tt):
    for j in range(MODEL_TILES):
        dtm[j] = delta[:, _lane_tile(j)]
    for s in range(ns):
        d_s = jnp.concatenate([dtm[j, pl.ds(s, tt, stride=ns), :] for j in range(MODEL_TILES)], axis=1)
        x2_s = x1_ref[s] + d_s
        x2_ref[s] = x2_s
        xn[s * tt:(s + 1) * tt, :] = _rms(x2_s, gx_ref[...])
    qv = _dot(xn[...].astype(BF16), wq_ref[...]).astype(BF16)
    for s in range(ns):
        q_ref[s] = qv[s * tt:(s + 1) * tt, :]


def _state_out(hre_out, him_out, hcr, hci):
    hre_out[...] = jnp.concatenate([hcr[j] for j in range(STATE_TILES)], axis=1)
    him_out[...] = jnp.concatenate([hci[j] for j in range(STATE_TILES)], axis=1)


_S5_WEIGHTS = ("bblk", "cre", "cim", "ab_re", "ab_im", "d", "w_glu")
_CONV_WEIGHTS = ("conv_w", "conv_b", "ln_g", "ln_b")
_OUT_WEIGHTS = ("w_out", "g_x", "w_q")


def _mixer_scratch(ns, rows):
    return [pltpu.VMEM((STATE_TILES, rows, LANES), F32),
            pltpu.VMEM((STATE_TILES, rows, LANES), F32),
            pltpu.VMEM((STATE_TILES, ns, LANES), F32),
            pltpu.VMEM((STATE_TILES, ns, LANES), F32),
            pltpu.VMEM((MODEL_TILES, rows, LANES), F32),
            pltpu.VMEM((rows, D_MODEL), F32)]


def _mixer_kernel(*refs, ns, tt, has_init):
    n_in = 7 if has_init else 4
    u_ref, val_ref, gate_ref, x1_ref = refs[:4]
    (bblk_ref, cre_ref, cim_ref, ar_ref, ai_ref, d_ref, wglu_ref,
     cw_ref, cb_ref, lng_ref, lnb_ref, wout_ref, gx_ref, wq_ref) = refs[n_in:n_in + 14]
    x2_ref, q_ref, hre_out, him_out, buf_out = refs[n_in + 14:n_in + 19]
    xr, xi, hcr, hci, dtm, xn, utm, vext, ytm = refs[n_in + 19:]
    rows = ns * tt
    hist_rows = HIST * ns

    if has_init:
        h0r_ref, h0i_ref, cache_ref = refs[4:7]
        for j in range(STATE_TILES):
            hcr[j] = h0r_ref[:, _lane_tile(j)]
            hci[j] = h0i_ref[:, _lane_tile(j)]
        for j in range(CH_TILES):
            vext[j, FIRST_TAP * ns:hist_rows, :] = (
                cache_ref[:, :, _lane_tile(j)].reshape((CONV_WIDTH - 1) * ns, LANES))
    else:
        @pl.when(pl.program_id(0) == 0)
        def _():
            hcr[...] = jnp.zeros_like(hcr)
            hci[...] = jnp.zeros_like(hci)
            vext[:, :hist_rows, :] = jnp.zeros((CH_TILES, hist_rows, LANES), F32)

    for s in range(ns):
        u_s = u_ref[s]
        v_s = val_ref[s] * jax.nn.sigmoid(gate_ref[s])
        for j in range(CH_TILES):
            utm[j, pl.ds(s, tt, stride=ns), :] = u_s[:, _lane_tile(j)]
            vext[j, pl.ds(hist_rows + s, tt, stride=ns), :] = v_s[:, _lane_tile(j)]

    _s5_input_and_conv(utm, vext, ytm, xr, xi, bblk_ref, cw_ref, ns=ns, rows=rows)
    _s5_scan(xr, xi, hcr, hci, ar_ref, ai_ref, ns=ns, tt=tt)
    u_all = jnp.concatenate([utm[j] for j in range(CH_TILES)], axis=1)
    s_out = _s5_readout(u_all, xr, xi, cre_ref, cim_ref, d_ref, wglu_ref)
    c_out = _ln_swish(ytm, cb_ref, lng_ref, lnb_ref)
    delta = _project_delta(s_out, c_out, wout_ref)
    _finish_projection(delta, x1_ref, gx_ref, wq_ref, x2_ref, q_ref, dtm, xn, ns=ns, tt=tt)
    _state_out(hre_out, him_out, hcr, hci)

    def write_hist():
        for j in range(CH_TILES):
            buf_out[:, :, _lane_tile(j)] = (
                vext[j, (tt + FIRST_TAP) * ns:(tt + HIST) * ns, :].reshape(CONV_WIDTH - 1, ns, LANES))

    if has_init:
        write_hist()
    else:
        pl.when(pl.program_id(0) == pl.num_programs(0) - 1)(write_hist)
        vext[:, :hist_rows, :] = vext[:, rows:rows + hist_rows, :]


def _mixer(z3, x13, init, w, *, ns, tt):
    nseq, seq_len, _ = z3.shape
    has_init = init is not None
    if has_init:
        assert tt == seq_len and nseq % ns == 0
        grid = (nseq // ns,)
        blk = lambda i: (i, 0, 0)
        col = lambda c: (lambda i: (i, 0, c))
        st_map = lambda i: (i, 0)
        hist_map = lambda i: (0, i, 0)
    else:
        assert ns == nseq and seq_len % tt == 0
        grid = (seq_len // tt,)
        blk = lambda i: (0, i, 0)
        col = lambda c: (lambda i: (0, i, c))
        st_map = lambda i: (0, 0)
        hist_map = lambda i: (0, 0, 0)
    rows = ns * tt
    assert rows % CONV_ROWS == 0 and ns % SUBLANES == 0
    st_spec = pl.BlockSpec((ns, N_STATE), st_map)
    hist_spec = pl.BlockSpec((CONV_WIDTH - 1, ns, CONV_DIM), hist_map)
    row_spec = pl.BlockSpec((ns, tt, D_MODEL), blk)
    weights = [w[k] for k in _S5_WEIGHTS + _CONV_WEIGHTS + _OUT_WEIGHTS]
    in_specs = [pl.BlockSpec((ns, tt, SSM_DIM), col(c)) for c in range(3)] + [row_spec]
    args = [z3, z3, z3, x13]
    if has_init:
        in_specs += [st_spec, st_spec, hist_spec]
        args += list(init)
    in_specs += [_const_spec(a.shape) for a in weights]
    return pl.pallas_call(
        functools.partial(_mixer_kernel, ns=ns, tt=tt, has_init=has_init),
        grid=grid,
        in_specs=in_specs,
        out_specs=(row_spec, row_spec, st_spec, st_spec, hist_spec),
        out_shape=(jax.ShapeDtypeStruct((nseq, seq_len, D_MODEL), F32),
                   jax.ShapeDtypeStruct((nseq, seq_len, D_MODEL), BF16),
                   jax.ShapeDtypeStruct((nseq, N_STATE), F32), jax.ShapeDtypeStruct((nseq, N_STATE), F32),
                   jax.ShapeDtypeStruct((CONV_WIDTH - 1, nseq, CONV_DIM), F32)),
        scratch_shapes=_mixer_scratch(ns, rows) + [
            pltpu.VMEM((CH_TILES, rows, LANES), F32),
            pltpu.VMEM((CH_TILES, rows + HIST * ns, LANES), F32),
            pltpu.VMEM((CH_TILES, rows, LANES), F32)],
        compiler_params=_params(("arbitrary",)),
        name="mixer_init" if has_init else "mixer_zero",
    )(*args, *weights)


def _mixer_tm_kernel(u_ref, c_ref, x1_ref, bblk_ref, cre_ref, cim_ref, ar_ref, ai_ref, d_ref, wglu_ref,
                     wout_ref, gx_ref, wq_ref, x2_ref, q_ref, hre_out, him_out,
                     xr, xi, hcr, hci, dtm, xn, s_prev, *, ns, tt, steps):
    i = pl.program_id(0)
    rows = ns * tt

    @pl.when(i == 0)
    def _():
        hcr[...] = jnp.zeros_like(hcr)
        hci[...] = jnp.zeros_like(hci)
        s_prev[...] = jnp.zeros_like(s_prev)

    u = u_ref[...]
    for kb in range(CH_TILES):
        x = _dot(u[:, _lane_tile(kb)].astype(BF16), bblk_ref[kb])
        for t in range(TILES_PER_CH):
            xr[kb * TILES_PER_CH + t] = x[:, _lane_tile(t)]
            xi[kb * TILES_PER_CH + t] = x[:, _lane_tile(TILES_PER_CH + t)]

    delta = _project_delta(s_prev[...], c_ref[...], wout_ref)
    group_rows = SCAN_UNROLL * ns
    tiles_per_group = MODEL_TILES // (STATE_TILES // SCAN_TILE_GROUP)

    def after(group, tile_group):
        r0 = group * group_rows
        return delta[r0:r0 + SUBLANES, _lane_tile(tile_group * tiles_per_group)]

    _s5_scan(xr, xi, hcr, hci, ar_ref, ai_ref, ns=ns, tt=tt, after=after)
    _finish_projection(delta, x1_ref, gx_ref, wq_ref, x2_ref, q_ref, dtm, xn, ns=ns, tt=tt)
    s_prev[...] = _s5_readout(u, xr, xi, cre_ref, cim_ref, d_ref, wglu_ref)

    @pl.when(i == steps - 1)
    def _():
        _state_out(hre_out, him_out, hcr, hci)


def _mixer_tm(u_tm, c_tm, x13, w, *, tt):
    ns, seq_len, _ = x13.shape
    steps = seq_len // tt
    rows = ns * tt
    weights = [w[k] for k in _S5_WEIGHTS + _OUT_WEIGHTS]
    cur = lambda i: jnp.minimum(i, steps - 1)
    prev = lambda i: jnp.clip(i - 1, 0, steps - 1)
    tm_spec = lambda m: pl.BlockSpec((rows, SSM_DIM), lambda i: (m(i), 0))
    row_spec = pl.BlockSpec((ns, tt, D_MODEL), lambda i: (0, prev(i), 0))
    st_spec = pl.BlockSpec((ns, N_STATE), lambda i: (0, 0))
    return pl.pallas_call(
        functools.partial(_mixer_tm_kernel, ns=ns, tt=tt, steps=steps),
        grid=(steps + 1,),
        in_specs=[tm_spec(cur), tm_spec(prev), row_spec] + [_const_spec(a.shape) for a in weights],
        out_specs=(row_spec, row_spec, st_spec, st_spec),
        out_shape=(jax.ShapeDtypeStruct((ns, seq_len, D_MODEL), F32),
                   jax.ShapeDtypeStruct((ns, seq_len, D_MODEL), BF16),
                   jax.ShapeDtypeStruct((ns, N_STATE), F32), jax.ShapeDtypeStruct((ns, N_STATE), F32)),
        scratch_shapes=_mixer_scratch(ns, rows) + [pltpu.VMEM((rows, SSM_DIM), F32)],
        compiler_params=_params(("arbitrary",)),
        name="mixer_tm",
    )(u_tm, c_tm, x13, *weights)


def _block_diag_tiles(x):
    _, a, b = x.shape
    x4 = x.reshape(CH_TILES, GROUPS_PER_TILE, a, b)
    eye = jnp.eye(GROUPS_PER_TILE, dtype=x.dtype)
    out = x4[:, :, :, None, :] * eye[None, :, None, :, None]
    return out.reshape(CH_TILES, GROUPS_PER_TILE * a, GROUPS_PER_TILE * b)


def _row(x):
    return x.reshape(1, -1)


def _state_slabs(x):
    return jnp.broadcast_to(x.reshape(STATE_TILES, 1, LANES), (STATE_TILES, SUBLANES, LANES))


def kernel(x_prompt, x_sample, state_ssm_re, state_ssm_im, cache_conv, cache_mem_k, cache_mem_v, mem_prompt,
           g_mem, w_mem_k, w_mem_v, g_ffn1, w_ffn1_gate, w_ffn1_up, w_ffn1_down, g_mix, w_in,
           ssm_a_re, ssm_a_im, ssm_log_dt, ssm_b_re, ssm_b_im, ssm_c_re, ssm_c_im, ssm_d, w_ssm_glu,
           conv_w, conv_b, conv_ln_g, conv_ln_b, w_out, g_xattn, w_mem_q, w_mem_o,
           g_ffn2, w_ffn2_gate, w_ffn2_up, w_ffn2_down, g_final):
    depth = g_ffn1.shape[0]
    assert depth == 1
    l = 0
    bp, seq, _ = x_prompt.shape
    bs, dseq, _ = x_sample.shape
    bf = lambda a: a.astype(BF16)

    ab_re, ab_im, bbar_re, bbar_im = _s5_params(ssm_a_re[l], ssm_a_im[l], ssm_log_dt[l],
                                                jnp.swapaxes(ssm_b_re[l], 1, 2), jnp.swapaxes(ssm_b_im[l], 1, 2))
    mixer_w = dict(
        bblk=bf(jnp.concatenate([_block_diag_tiles(bbar_re), _block_diag_tiles(bbar_im)], axis=-1)),
        cre=bf(_block_diag_tiles(jnp.swapaxes(ssm_c_re[l], 1, 2))),
        cim=bf(-_block_diag_tiles(jnp.swapaxes(ssm_c_im[l], 1, 2))),
        ab_re=_state_slabs(ab_re), ab_im=_state_slabs(ab_im),
        d=_row(ssm_d[l]),
        conv_w=jnp.swapaxes(conv_w[l].reshape(CONV_WIDTH, CH_TILES, LANES), 0, 1),
        conv_b=_row(conv_b[l]), ln_g=_row(conv_ln_g[l]), ln_b=_row(conv_ln_b[l]),
        g_x=_row(g_xattn[l]),
    )
    g_mix_r, g_final_r = _row(g_mix[l]), _row(g_final)

    first = [w_ffn1_gate[l], w_ffn1_up[l], w_ffn1_down[l], w_in[l], w_ssm_glu[l]]
    mk, mv, mk_hm, mv_hm, w1g_b, w1u_b, w1d_b, w_in_b, w_glu_b = _mem_kv(
        mem_prompt.reshape(bp * N_MEM, D_MODEL), _row(g_mem[l]), bf(w_mem_k[l]), bf(w_mem_v[l]), first)
    ffn1 = (_row(g_ffn1[l]), w1g_b, w1u_b, w1d_b)
    mixer_w["w_glu"] = w_glu_b

    p_rows, s_rows = bp * seq, bs * dseq
    prompt_tt = 64
    later = [w_out[l], w_mem_q[l], w_mem_o[l], w_ffn2_gate[l], w_ffn2_up[l], w_ffn2_down[l]]
    x1p, u_tm, c_tm, p_buf, w_out_b, w_q_b, w_o_b, w2g_b, w2u_b, w2d_b = _front(
        x_prompt, *ffn1, g_mix_r, w_in_b, *[mixer_w[k] for k in _CONV_WEIGHTS], later, tt=prompt_tt)
    mixer_w.update(w_out=w_out_b, w_q=w_q_b)
    ffn2 = (_row(g_ffn2[l]), w2g_b, w2u_b, w2d_b)
    x2p, qp, p_re, p_im = _mixer_tm(u_tm, c_tm, x1p, mixer_w, tt=prompt_tt)

    x1s, zs = _ffn_in(x_sample.reshape(s_rows, D_MODEL), *ffn1, g_mix_r, w_in_b)
    init_s = (state_ssm_re[l].reshape(bs, N_STATE), state_ssm_im[l].reshape(bs, N_STATE),
              jnp.swapaxes(cache_conv[l], 0, 1))
    x2s, qs, s_re, s_im, s_buf = _mixer(zs.reshape(bs, dseq, -1), x1s.reshape(bs, dseq, D_MODEL), init_s,
                                        mixer_w, ns=32, tt=dseq)

    kv_rows = lambda a: a.reshape(bs * N_MEM, MEM_HEADS, MEM_HEAD_DIM)
    yp, ys = _tail(x2p.reshape(p_rows, D_MODEL), qp.reshape(p_rows, D_MODEL), mk_hm, mv_hm,
                   x2s.reshape(s_rows, D_MODEL), qs.reshape(s_rows, D_MODEL),
                   kv_rows(cache_mem_k), kv_rows(cache_mem_v), w_o_b, *ffn2, g_final_r, tm=256)
    yp = yp.reshape(bp, seq, D_MODEL)
    ys = ys.reshape(bs, dseq, D_MODEL)

    st = lambda a, n: a.reshape(1, n, SSM_GROUPS, SSM_STATE)
    kv = lambda a: a.reshape(1, bp, N_MEM, MEM_HEADS, MEM_HEAD_DIM)
    buf = lambda a: jnp.swapaxes(a, 0, 1)[None]
    return (yp, ys, st(p_re, bp), st(p_im, bp), buf(p_buf), kv(mk), kv(mv),
            st(s_re, bs), st(s_im, bs), buf(s_buf))
```

```python
import functools

import jax
import jax.numpy as jnp
from jax import lax
from jax.experimental import pallas as pl
from jax.experimental.pallas import tpu as pltpu

F32 = jnp.float32
BF16 = jnp.bfloat16

EPS = 1e-6
D_MODEL = 1024
D_FF = 2816
SSM_DIM = 512
CONV_DIM = 512
SSM_GROUPS = 32
SSM_GROUP_CH = 16
SSM_STATE = 64
N_STATE = SSM_GROUPS * SSM_STATE
CONV_WIDTH = 31
N_MEM = 256
MEM_HEADS = 4
MEM_HEAD_DIM = 256

LANES = 128
SUBLANES = 8
STATE_TILES = N_STATE // LANES
CH_TILES = SSM_DIM // LANES
MODEL_TILES = D_MODEL // LANES
GROUPS_PER_TILE = LANES // SSM_GROUP_CH
STATES_PER_CH_TILE = GROUPS_PER_TILE * SSM_STATE
TILES_PER_CH = STATES_PER_CH_TILE // LANES
HIST = 32
FIRST_TAP = HIST - (CONV_WIDTH - 1)
SCAN_TILE_GROUP = 4
CONV_ROWS = 64
VMEM_LIMIT = 56 * 1024 * 1024


def _dot(a, b):
    return jnp.dot(a, b, preferred_element_type=F32)


def _rms(x, g):
    return x * lax.rsqrt(jnp.mean(x * x, axis=-1, keepdims=True) + EPS) * g


def _lane_tile(j):
    return slice(j * LANES, (j + 1) * LANES)


def _const_spec(shape):
    zeros = (0,) * len(shape)
    return pl.BlockSpec(shape, lambda *_: zeros, pipeline_mode=pl.Buffered(1))


def _params(sem):
    return pltpu.CompilerParams(dimension_semantics=sem, vmem_limit_bytes=VMEM_LIMIT)


def _s5_params_kernel(ar_ref, ai_ref, ldt_ref, bre_ref, bim_ref, abr_ref, abi_ref, bbr_ref, bbi_ref):
    lr = ar_ref[...]
    li = ai_ref[...]
    dt = jnp.exp(ldt_ref[...])
    mag = jnp.exp(lr * dt)
    ab_re = mag * jnp.cos(li * dt)
    ab_im = mag * jnp.sin(li * dt)
    den = lr * lr + li * li
    nr = ab_re - 1.0
    ni = ab_im
    coef_re = (nr * lr + ni * li) / den
    coef_im = (ni * lr - nr * li) / den
    abr_ref[...] = ab_re
    abi_ref[...] = ab_im
    b_re = bre_ref[...]
    b_im = bim_ref[...]
    bbr_ref[...] = coef_re * b_re - coef_im * b_im
    bbi_ref[...] = coef_re * b_im + coef_im * b_re


def _s5_params(a_re, a_im, log_dt, b_re, b_im):
    a_shape = jax.ShapeDtypeStruct((SSM_GROUPS, 1, SSM_STATE), F32)
    b_shape = jax.ShapeDtypeStruct(b_re.shape, F32)
    return pl.pallas_call(
        _s5_params_kernel,
        out_shape=(a_shape, a_shape, b_shape, b_shape),
        name="s5_params",
    )(a_re.reshape(a_shape.shape), a_im.reshape(a_shape.shape), log_dt.reshape(SSM_GROUPS, 1, 1), b_re, b_im)


BF16_SUBLANES = 16


def _cast_blocking(w, max_steps):
    n_rows = w.shape[0]
    n_blocks = max(n for n in range(1, max_steps + 1)
                   if n_rows % n == 0 and (n_rows // n) % BF16_SUBLANES == 0)
    return n_rows // n_blocks, n_blocks


def _cast_specs(to_cast, max_steps):
    specs = []
    for w in to_cast:
        blk_rows, n_blocks = _cast_blocking(w, max_steps)
        specs.append(pl.BlockSpec((blk_rows, w.shape[1]),
                                  (lambda n: (lambda j: (jnp.minimum(j, n - 1), 0)))(n_blocks)))
    return specs


def _cast_blocks(cast_in, cast_out):
    for src, dst in zip(cast_in, cast_out):
        dst[...] = src[...].astype(BF16)


def _mem_kv_kernel(*refs, n_cast):
    m_ref, g_ref, wk_ref, wv_ref = refs[:4]
    k_ref, v_ref, kh_ref, vh_ref = refs[4 + n_cast:8 + n_cast]
    _cast_blocks(refs[4:4 + n_cast], refs[8 + n_cast:])
    m = _rms(m_ref[...], g_ref[...]).astype(BF16)
    k = _dot(m, wk_ref[...])
    v = _dot(m, wv_ref[...])
    for h in range(MEM_HEADS):
        cols = slice(h * MEM_HEAD_DIM, (h + 1) * MEM_HEAD_DIM)
        k_ref[:, h, :] = k[:, cols]
        v_ref[:, h, :] = v[:, cols]
        kh_ref[0, h] = k[:, cols].T.astype(BF16)
        vh_ref[0, h] = v[:, cols].astype(BF16)


def _mem_kv(mem2d, g, wk, wv, to_cast):
    rows = mem2d.shape[0]
    nb = rows // N_MEM
    cast_specs = _cast_specs(to_cast, nb)
    row_spec = pl.BlockSpec((N_MEM, D_MODEL), lambda i: (i, 0))
    head_spec = pl.BlockSpec((N_MEM, MEM_HEADS, MEM_HEAD_DIM), lambda i: (i, 0, 0))
    hm_spec = pl.BlockSpec((1, MEM_HEADS, N_MEM, MEM_HEAD_DIM), lambda i: (i, 0, 0, 0))
    out = jax.ShapeDtypeStruct((rows, MEM_HEADS, MEM_HEAD_DIM), F32)
    out_hm = jax.ShapeDtypeStruct((nb, MEM_HEADS, N_MEM, MEM_HEAD_DIM), BF16)
    return pl.pallas_call(
        functools.partial(_mem_kv_kernel, n_cast=len(to_cast)),
        grid=(nb,),
        in_specs=[row_spec, _const_spec((1, D_MODEL)),
                  _const_spec((D_MODEL, D_MODEL)), _const_spec((D_MODEL, D_MODEL))] + cast_specs,
        out_specs=(head_spec, head_spec, hm_spec, hm_spec) + tuple(cast_specs),
        out_shape=(out, out, out_hm, out_hm) + tuple(jax.ShapeDtypeStruct(w.shape, BF16) for w in to_cast),
        compiler_params=_params(("arbitrary",)),
        name="mem_kv",
    )(mem2d, g, wk, wv, *to_cast)


def _zero_after(piece):
    bits = pltpu.bitcast(piece, jnp.uint32)
    return pltpu.bitcast((bits >> 16) >> 16, F32)


def _conv_rows(vext, ytm, cw_ref, j, base, *, ns, after):
    acc = _zero_after(after)
    for k in range(CONV_WIDTH):
        acc = acc + cw_ref[j, k:k + 1, :] * vext[j, pl.ds(base + (FIRST_TAP + k) * ns, CONV_ROWS), :]
    ytm[j, pl.ds(base, CONV_ROWS), :] = acc


def _ln_swish(ytm, cb_ref, lng_ref, lnb_ref):
    yc = jnp.concatenate([ytm[j] for j in range(CH_TILES)], axis=1) + cb_ref[...]
    mu = jnp.mean(yc, axis=-1, keepdims=True)
    var = jnp.mean(jnp.square(yc - mu), axis=-1, keepdims=True)
    yn = (yc - mu) * lax.rsqrt(var + EPS) * lng_ref[...] + lnb_ref[...]
    return yn * jax.nn.sigmoid(yn)


def _swiglu_half(x, g, wg_ref, wu_ref, wd_ref):
    hn = _rms(x, g).astype(BF16)
    gate = _dot(hn, wg_ref[...])
    up = _dot(hn, wu_ref[...])
    act = (gate * jax.nn.sigmoid(gate) * up).astype(BF16)
    return x + 0.5 * _dot(act, wd_ref[...])


def _ffn_in_kernel(x_ref, g1_ref, wg_ref, wu_ref, wd_ref, g2_ref, win_ref, x1_ref, z_ref):
    x1 = _swiglu_half(x_ref[...], g1_ref[...], wg_ref, wu_ref, wd_ref)
    x1_ref[...] = x1
    z_ref[...] = _dot(_rms(x1, g2_ref[...]).astype(BF16), win_ref[...])


def _ffn_in(x2d, g1, wg, wu, wd, g2, win, tm=512):
    rows = x2d.shape[0]
    zdim = win.shape[1]
    row_spec = pl.BlockSpec((tm, D_MODEL), lambda i: (i, 0))
    return pl.pallas_call(
        _ffn_in_kernel,
        grid=(rows // tm,),
        in_specs=[row_spec, _const_spec((1, D_MODEL)),
                  _const_spec((D_MODEL, D_FF)), _const_spec((D_MODEL, D_FF)), _const_spec((D_FF, D_MODEL)),
                  _const_spec((1, D_MODEL)), _const_spec((D_MODEL, zdim))],
        out_specs=(row_spec, pl.BlockSpec((tm, zdim), lambda i: (i, 0))),
        out_shape=(jax.ShapeDtypeStruct((rows, D_MODEL), F32), jax.ShapeDtypeStruct((rows, zdim), F32)),
        compiler_params=_params(("arbitrary",)),
        name="ffn_in",
    )(x2d, g1, wg, wu, wd, g2, win)


def _front_kernel(*refs, ns, tt, steps, n_cast):
    (x_ref, g1_ref, wg_ref, wu_ref, wd_ref, g2_ref, win_ref, cw_ref, cb_ref, lng_ref, lnb_ref) = refs[:11]
    cast_in = refs[11:11 + n_cast]
    x1_ref, u_ref, c_ref, hist_ref = refs[11 + n_cast:15 + n_cast]
    cast_out = refs[15 + n_cast:15 + 2 * n_cast]
    vext, ytm, utm = refs[15 + 2 * n_cast:]
    j = pl.program_id(0)
    rows = ns * tt
    hist_rows = HIST * ns
    gate_tile_step = (D_FF // LANES) // CH_TILES

    @pl.when(j == 0)
    def _():
        vext[...] = jnp.zeros_like(vext)

    x = x_ref[...].reshape(rows, D_MODEL)
    hn = _rms(x, g1_ref[...]).astype(BF16)
    gate = _dot(hn, wg_ref[...])
    for jt in range(CH_TILES):
        for b in range(rows // CONV_ROWS):
            _conv_rows(vext, ytm, cw_ref, jt, b * CONV_ROWS, ns=ns,
                       after=gate[b * CONV_ROWS:(b + 1) * CONV_ROWS, _lane_tile(jt * gate_tile_step)])
    c_ref[...] = _ln_swish(ytm, cb_ref, lng_ref, lnb_ref)
    up = _dot(hn, wu_ref[...])
    act = (gate * jax.nn.sigmoid(gate) * up).astype(BF16)
    x1 = x + 0.5 * _dot(act, wd_ref[...])
    x1_ref[...] = x1.reshape(ns, tt, D_MODEL)
    z = _dot(_rms(x1, g2_ref[...]).astype(BF16), win_ref[...])
    u = z[:, :SSM_DIM]
    v = z[:, SSM_DIM:SSM_DIM + CONV_DIM] * jax.nn.sigmoid(z[:, SSM_DIM + CONV_DIM:])

    vext[:, :hist_rows, :] = vext[:, rows:rows + hist_rows, :]
    for s in range(ns):
        seq = slice(s * tt, (s + 1) * tt)
        for jt in range(CH_TILES):
            vext[jt, pl.ds(hist_rows + s, tt, stride=ns), :] = v[seq, _lane_tile(jt)]
            utm[jt, pl.ds(s, tt, stride=ns), :] = u[seq, _lane_tile(jt)]
    u_ref[...] = jnp.concatenate([utm[jt] for jt in range(CH_TILES)], axis=1)

    _cast_blocks(cast_in, cast_out)

    @pl.when(j == steps)
    def _():
        for jt in range(CH_TILES):
            hist_ref[:, :, _lane_tile(jt)] = (
                vext[jt, FIRST_TAP * ns:hist_rows, :].reshape(CONV_WIDTH - 1, ns, LANES))


def _front(x3, g1, wg, wu, wd, g2, win, cw, cb, lng, lnb, to_cast, *, tt):
    ns, seq_len, _ = x3.shape
    steps = seq_len // tt
    rows = ns * tt
    assert rows % CONV_ROWS == 0 and ns == SUBLANES
    zdim = win.shape[1]
    cur = lambda j: jnp.minimum(j, steps - 1)
    prev = lambda j: jnp.clip(j - 1, 0, steps - 1)
    x_spec = pl.BlockSpec((ns, tt, D_MODEL), lambda j: (0, cur(j), 0))
    tm_spec = lambda m: pl.BlockSpec((rows, SSM_DIM), lambda j: (m(j), 0))
    cast_specs = _cast_specs(to_cast, steps + 1)
    return pl.pallas_call(
        functools.partial(_front_kernel, ns=ns, tt=tt, steps=steps, n_cast=len(to_cast)),
        grid=(steps + 1,),
        in_specs=[x_spec, _const_spec((1, D_MODEL)),
                  _const_spec((D_MODEL, D_FF)), _const_spec((D_MODEL, D_FF)), _const_spec((D_FF, D_MODEL)),
                  _const_spec((1, D_MODEL)), _const_spec((D_MODEL, zdim)),
                  _const_spec(cw.shape), _const_spec(cb.shape), _const_spec(lng.shape), _const_spec(lnb.shape)]
                 + cast_specs,
        out_specs=(x_spec, tm_spec(cur), tm_spec(prev),
                   pl.BlockSpec((CONV_WIDTH - 1, ns, CONV_DIM), lambda j: (0, 0, 0))) + tuple(cast_specs),
        out_shape=(jax.ShapeDtypeStruct((ns, seq_len, D_MODEL), F32),
                   jax.ShapeDtypeStruct((seq_len * ns, SSM_DIM), F32),
                   jax.ShapeDtypeStruct((seq_len * ns, CONV_DIM), F32),
                   jax.ShapeDtypeStruct((CONV_WIDTH - 1, ns, CONV_DIM), F32))
                  + tuple(jax.ShapeDtypeStruct(w.shape, BF16) for w in to_cast),
        scratch_shapes=[pltpu.VMEM((CH_TILES, rows + HIST * ns, LANES), F32),
                        pltpu.VMEM((CH_TILES, rows, LANES), F32),
                        pltpu.VMEM((CH_TILES, rows, LANES), F32)],
        compiler_params=_params(("arbitrary",)),
        name="front",
    )(x3, g1, wg, wu, wd, g2, win, cw, cb, lng, lnb, *to_cast)


def _softmax_rows(sc):
    e = jnp.exp(sc - jnp.max(sc, axis=-1, keepdims=True))
    return e / jnp.sum(e, axis=-1, keepdims=True)


def _attn_long_scores(q_ref, k_ref):
    scale = MEM_HEAD_DIM ** -0.5
    return [_dot(q_ref[:, h * MEM_HEAD_DIM:(h + 1) * MEM_HEAD_DIM], k_ref[0, h]) * scale
            for h in range(MEM_HEADS)]


def _attn_long_values(scores, v_ref):
    return jnp.concatenate([_dot(_softmax_rows(sc).astype(BF16), v_ref[0, h]).astype(BF16)
                            for h, sc in enumerate(scores)], axis=1)


def _attn_short_scores(q_ref, k_ref, *, nseq, lq):
    scale = MEM_HEAD_DIM ** -0.5
    kv_rows = N_MEM * MEM_HEADS
    k2d = k_ref.reshape(nseq * kv_rows, MEM_HEAD_DIM)
    shape = (MEM_HEADS * lq, kv_rows)
    same_head = (lax.broadcasted_iota(jnp.int32, shape, 0) // lq
                 == lax.broadcasted_iota(jnp.int32, shape, 1) % MEM_HEADS)
    scores = []
    for s in range(nseq):
        rows = slice(s * lq, (s + 1) * lq)
        q = jnp.concatenate([q_ref[rows, h * MEM_HEAD_DIM:(h + 1) * MEM_HEAD_DIM] for h in range(MEM_HEADS)],
                            axis=0)
        k = k2d[s * kv_rows:(s + 1) * kv_rows, :].astype(BF16)
        sc = lax.dot_general(q, k, (((1,), (1,)), ((), ())), preferred_element_type=F32) * scale
        scores.append(jnp.where(same_head, sc, -jnp.inf))
    return scores


def _attn_short_values(scores, v_ref, *, lq):
    kv_rows = N_MEM * MEM_HEADS
    v2d = v_ref.reshape(len(scores) * kv_rows, MEM_HEAD_DIM)
    out = []
    for s, sc in enumerate(scores):
        v = v2d[s * kv_rows:(s + 1) * kv_rows, :].astype(BF16)
        o = _dot(_softmax_rows(sc).astype(BF16), v)
        out.append(jnp.concatenate([o[h * lq:(h + 1) * lq, :] for h in range(MEM_HEADS)], axis=1))
    return jnp.concatenate(out, axis=0)


def _tail_kernel(x2a_ref, qa_ref, ka_ref, va_ref, x2b_ref, qb_ref, kb_ref, vb_ref,
                 wo_ref, g_ref, wg_ref, wu_ref, wd_ref, gf_ref, ya_ref, yb_ref, oa_scr, ob_scr,
                 *, a_steps, tm, nseq, lq):
    j = pl.program_id(0)
    in_b = j > a_steps

    @pl.when(j == 0)
    def _():
        oa_scr[...] = jnp.zeros_like(oa_scr)
        ob_scr[...] = jnp.zeros_like(ob_scr)

    b_tile = pl.ds(pl.multiple_of(jnp.maximum(j - a_steps - 1, 0) * tm, tm), tm)
    o = jnp.where(in_b, ob_scr[b_tile, :], oa_scr[...])
    x2 = jnp.where(in_b, x2b_ref[...], x2a_ref[...])

    x3 = x2 + _dot(o, wo_ref[...])
    hn = _rms(x3, g_ref[...]).astype(BF16)
    gate = _dot(hn, wg_ref[...])
    up = _dot(hn, wu_ref[...])
    sc_a = _attn_long_scores(qa_ref, ka_ref)
    sc_b = _attn_short_scores(qb_ref, kb_ref, nseq=nseq, lq=lq)
    act = (gate * jax.nn.sigmoid(gate) * up).astype(BF16)
    x4 = x3 + 0.5 * _dot(act, wd_ref[...])

    b_rows = nseq * lq
    blk = jnp.minimum(j, a_steps - 1)
    ob_scr[pl.ds(pl.multiple_of(blk * b_rows, b_rows), b_rows), :] = (
        _attn_short_values(sc_b, vb_ref, lq=lq).astype(BF16))
    oa_scr[...] = _attn_long_values(sc_a, va_ref)

    y = _rms(x4, gf_ref[...])

    @pl.when(jnp.logical_and(j >= 1, j <= a_steps))
    def _():
        ya_ref[...] = y

    @pl.when(in_b)
    def _():
        yb_ref[...] = y


def _tail(x2a, qa, ka_hm, va_hm, x2b, qb, kb3d, vb3d, wo, g, wg, wu, wd, gf, *, tm):
    a_rows, b_rows = x2a.shape[0], x2b.shape[0]
    a_steps, b_steps = a_rows // tm, b_rows // tm
    n_b = kb3d.shape[0] // N_MEM
    lq = b_rows // n_b
    assert n_b % a_steps == 0
    nseq = n_b // a_steps
    tiles_per_seq = a_rows // ka_hm.shape[0] // tm
    attn_blk = lambda j: jnp.minimum(j, a_steps - 1)
    a_fin = lambda j: (jnp.clip(j - 1, 0, a_steps - 1), 0)
    b_fin = lambda j: (jnp.maximum(j - a_steps - 1, 0), 0)
    row = lambda m: pl.BlockSpec((tm, D_MODEL), m)
    ka_spec = pl.BlockSpec((1, MEM_HEADS, N_MEM, MEM_HEAD_DIM),
                           lambda j: (attn_blk(j) // tiles_per_seq, 0, 0, 0))
    kb_spec = pl.BlockSpec((nseq * N_MEM, MEM_HEADS, MEM_HEAD_DIM), lambda j: (attn_blk(j), 0, 0))
    return pl.pallas_call(
        functools.partial(_tail_kernel, a_steps=a_steps, tm=tm, nseq=nseq, lq=lq),
        grid=(a_steps + 1 + b_steps,),
        in_specs=[row(a_fin), row(lambda j: (attn_blk(j), 0)), ka_spec, ka_spec,
                  row(b_fin), pl.BlockSpec((nseq * lq, D_MODEL), lambda j: (attn_blk(j), 0)), kb_spec, kb_spec,
                  _const_spec((D_MODEL, D_MODEL)), _const_spec((1, D_MODEL)),
                  _const_spec((D_MODEL, D_FF)), _const_spec((D_MODEL, D_FF)), _const_spec((D_FF, D_MODEL)),
                  _const_spec((1, D_MODEL))],
        out_specs=(row(a_fin), row(b_fin)),
        out_shape=(jax.ShapeDtypeStruct((a_rows, D_MODEL), F32), jax.ShapeDtypeStruct((b_rows, D_MODEL), F32)),
        scratch_shapes=[pltpu.VMEM((tm, D_MODEL), BF16), pltpu.VMEM((b_rows, D_MODEL), BF16)],
        compiler_params=_params(("arbitrary",)),
        name="tail",
    )(x2a, qa, ka_hm, va_hm, x2b, qb, kb3d, vb3d, wo, g, wg, wu, wd, gf)


def _s5_input_and_conv(utm, vext, ytm, xr, xi, bblk_ref, cw_ref, *, ns, rows):
    def body(kb, carry):
        x = _dot(utm[kb].astype(BF16), bblk_ref[kb])
        for i in range(TILES_PER_CH):
            xr[kb * TILES_PER_CH + i] = x[:, _lane_tile(i)]
            xi[kb * TILES_PER_CH + i] = x[:, _lane_tile(TILES_PER_CH + i)]
        for b in range(rows // CONV_ROWS):
            _conv_rows(vext, ytm, cw_ref, kb, b * CONV_ROWS, ns=ns,
                       after=x[b * CONV_ROWS:(b + 1) * CONV_ROWS, :LANES])
        return carry

    lax.fori_loop(0, CH_TILES, body, 0)


SCAN_UNROLL = 8


def _s5_scan(xr, xi, hcr, hci, ar_ref, ai_ref, *, ns, tt, after=None):
    def step(row, carry, tiles, a_r, a_i):
        new_r, new_i = [], []
        for idx, j in enumerate(tiles):
            h_r, h_i = carry[idx], carry[SCAN_TILE_GROUP + idx]
            n_r = a_r[idx] * h_r - a_i[idx] * h_i + xr[j, row, :]
            n_i = a_r[idx] * h_i + a_i[idx] * h_r + xi[j, row, :]
            xr[j, row, :] = n_r
            xi[j, row, :] = n_i
            new_r.append(n_r)
            new_i.append(n_i)
        return tuple(new_r) + tuple(new_i)

    for i in range(ns // SUBLANES):
        srow = slice(i * SUBLANES, (i + 1) * SUBLANES)
        for jg in range(STATE_TILES // SCAN_TILE_GROUP):
            tiles = tuple(range(jg * SCAN_TILE_GROUP, (jg + 1) * SCAN_TILE_GROUP))
            a_r = [ar_ref[j] for j in tiles]
            a_i = [ai_ref[j] for j in tiles]
            carry = tuple(hcr[j, srow, :] for j in tiles) + tuple(hci[j, srow, :] for j in tiles)
            if after is None:
                def body(t, c, tiles=tiles, a_r=a_r, a_i=a_i, i=i):
                    row = pl.ds(pl.multiple_of(t * ns + i * SUBLANES, SUBLANES), SUBLANES)
                    return step(row, c, tiles, a_r, a_i)

                carry = lax.fori_loop(0, tt, body, carry, unroll=SCAN_UNROLL)
            else:
                for t in range(tt):
                    if t % SCAN_UNROLL == 0:
                        zero = _zero_after(after(t // SCAN_UNROLL, jg))
                        carry = tuple(c + zero for c in carry)
                    carry = step(pl.ds(t * ns + i * SUBLANES, SUBLANES), carry, tiles, a_r, a_i)
            for idx, j in enumerate(tiles):
                hcr[j, srow, :] = carry[idx]
                hci[j, srow, :] = carry[SCAN_TILE_GROUP + idx]


def _s5_readout(u_all, xr, xi, cre_ref, cim_ref, d_ref, wglu_ref):
    ys = []
    for ob in range(CH_TILES):
        h_re = jnp.concatenate([xr[ob * TILES_PER_CH + i] for i in range(TILES_PER_CH)], axis=1).astype(BF16)
        h_im = jnp.concatenate([xi[ob * TILES_PER_CH + i] for i in range(TILES_PER_CH)], axis=1).astype(BF16)
        ys.append(_dot(h_re, cre_ref[ob]) + _dot(h_im, cim_ref[ob]))
    y = jnp.concatenate(ys, axis=1) + d_ref[...] * u_all
    g = jax.nn.gelu(y)
    return g * jax.nn.sigmoid(_dot(g.astype(BF16), wglu_ref[...]))


def _project_delta(s_out, c_out, wout_ref):
    return _dot(jnp.concatenate([s_out, c_out], axis=1).astype(BF16), wout_ref[...])


def _finish_projection(delta, x1_ref, gx_ref, wq_ref, x2_ref, q_ref, dtm, xn, *, ns, tt):
    for j in range(MODEL_TILES):
        dtm[j] = delta[:, _lane_tile(j)]
    for s in range(ns):
        d_s = jnp.concatenate([dtm[j, pl.ds(s, tt, stride=ns), :] for j in range(MODEL_TILES)], axis=1)
        x2_s = x1_ref[s] + d_s
        x2_ref[s] = x2_s
        xn[s * tt:(s + 1) * tt, :] = _rms(x2_s, gx_ref[...])
    qv = _dot(xn[...].astype(BF16), wq_ref[...]).astype(BF16)
    for s in range(ns):
        q_ref[s] = qv[s * tt:(s + 1) * tt, :]


def _state_out(hre_out, him_out, hcr, hci):
    hre_out[...] = jnp.concatenate([hcr[j] for j in range(STATE_TILES)], axis=1)
    him_out[...] = jnp.concatenate([hci[j] for j in range(STATE_TILES)], axis=1)


_S5_WEIGHTS = ("bblk", "cre", "cim", "ab_re", "ab_im", "d", "w_glu")
_CONV_WEIGHTS = ("conv_w", "conv_b", "ln_g", "ln_b")
_OUT_WEIGHTS = ("w_out", "g_x", "w_q")


def _mixer_scratch(ns, rows):
    return [pltpu.VMEM((STATE_TILES, rows, LANES), F32),
            pltpu.VMEM((STATE_TILES, rows, LANES), F32),
            pltpu.VMEM((STATE_TILES, ns, LANES), F32),
            pltpu.VMEM((STATE_TILES, ns, LANES), F32),
            pltpu.VMEM((MODEL_TILES, rows, LANES), F32),
            pltpu.VMEM((rows, D_MODEL), F32)]


def _mixer_kernel(u_ref, val_ref, gate_ref, x1_ref, h0r_ref, h0i_ref, cache_ref,
                  bblk_ref, cre_ref, cim_ref, ar_ref, ai_ref, d_ref, wglu_ref,
                  cw_ref, cb_ref, lng_ref, lnb_ref, wout_ref, gx_ref, wq_ref,
                  x2_ref, q_ref, hre_out, him_out, hist_out,
                  xr, xi, hcr, hci, dtm, xn, utm, vext, ytm, *, ns, tt):
    rows = ns * tt
    hist_rows = HIST * ns

    for j in range(STATE_TILES):
        hcr[j] = h0r_ref[:, _lane_tile(j)]
        hci[j] = h0i_ref[:, _lane_tile(j)]
    for j in range(CH_TILES):
        vext[j, FIRST_TAP * ns:hist_rows, :] = (
            cache_ref[:, :, _lane_tile(j)].reshape((CONV_WIDTH - 1) * ns, LANES))

    for s in range(ns):
        u_s = u_ref[s]
        v_s = val_ref[s] * jax.nn.sigmoid(gate_ref[s])
        for j in range(CH_TILES):
            utm[j, pl.ds(s, tt, stride=ns), :] = u_s[:, _lane_tile(j)]
            vext[j, pl.ds(hist_rows + s, tt, stride=ns), :] = v_s[:, _lane_tile(j)]

    _s5_input_and_conv(utm, vext, ytm, xr, xi, bblk_ref, cw_ref, ns=ns, rows=rows)
    _s5_scan(xr, xi, hcr, hci, ar_ref, ai_ref, ns=ns, tt=tt)
    u_all = jnp.concatenate([utm[j] for j in range(CH_TILES)], axis=1)
    s_out = _s5_readout(u_all, xr, xi, cre_ref, cim_ref, d_ref, wglu_ref)
    c_out = _ln_swish(ytm, cb_ref, lng_ref, lnb_ref)
    delta = _project_delta(s_out, c_out, wout_ref)
    _finish_projection(delta, x1_ref, gx_ref, wq_ref, x2_ref, q_ref, dtm, xn, ns=ns, tt=tt)
    _state_out(hre_out, him_out, hcr, hci)
    for j in range(CH_TILES):
        hist_out[:, :, _lane_tile(j)] = (
            vext[j, (tt + FIRST_TAP) * ns:(tt + HIST) * ns, :].reshape(CONV_WIDTH - 1, ns, LANES))


def _mixer(z3, x13, h0_re, h0_im, hist, w, *, ns):
    nseq, tt, _ = z3.shape
    rows = ns * tt
    assert nseq % ns == 0 and rows % CONV_ROWS == 0 and ns % SUBLANES == 0
    st_spec = pl.BlockSpec((ns, N_STATE), lambda i: (i, 0))
    hist_spec = pl.BlockSpec((CONV_WIDTH - 1, ns, CONV_DIM), lambda i: (0, i, 0))
    row_spec = pl.BlockSpec((ns, tt, D_MODEL), lambda i: (i, 0, 0))
    z_specs = [pl.BlockSpec((ns, tt, SSM_DIM), (lambda c: (lambda i: (i, 0, c)))(c)) for c in range(3)]
    weights = [w[k] for k in _S5_WEIGHTS + _CONV_WEIGHTS + _OUT_WEIGHTS]
    return pl.pallas_call(
        functools.partial(_mixer_kernel, ns=ns, tt=tt),
        grid=(nseq // ns,),
        in_specs=z_specs + [row_spec, st_spec, st_spec, hist_spec] + [_const_spec(a.shape) for a in weights],
        out_specs=(row_spec, row_spec, st_spec, st_spec, hist_spec),
        out_shape=(jax.ShapeDtypeStruct((nseq, tt, D_MODEL), F32),
                   jax.ShapeDtypeStruct((nseq, tt, D_MODEL), BF16),
                   jax.ShapeDtypeStruct((nseq, N_STATE), F32), jax.ShapeDtypeStruct((nseq, N_STATE), F32),
                   jax.ShapeDtypeStruct((CONV_WIDTH - 1, nseq, CONV_DIM), F32)),
        scratch_shapes=_mixer_scratch(ns, rows) + [
            pltpu.VMEM((CH_TILES, rows, LANES), F32),
            pltpu.VMEM((CH_TILES, rows + HIST * ns, LANES), F32),
            pltpu.VMEM((CH_TILES, rows, LANES), F32)],
        compiler_params=_params(("arbitrary",)),
        name="mixer",
    )(z3, z3, z3, x13, h0_re, h0_im, hist, *weights)


def _mixer_tm_kernel(u_ref, c_ref, x1_ref, bblk_ref, cre_ref, cim_ref, ar_ref, ai_ref, d_ref, wglu_ref,
                     wout_ref, gx_ref, wq_ref, x2_ref, q_ref, hre_out, him_out,
                     xr, xi, hcr, hci, dtm, xn, s_prev, *, ns, tt, steps):
    i = pl.program_id(0)

    @pl.when(i == 0)
    def _():
        hcr[...] = jnp.zeros_like(hcr)
        hci[...] = jnp.zeros_like(hci)
        s_prev[...] = jnp.zeros_like(s_prev)

    u = u_ref[...]
    for kb in range(CH_TILES):
        x = _dot(u[:, _lane_tile(kb)].astype(BF16), bblk_ref[kb])
        for t in range(TILES_PER_CH):
            xr[kb * TILES_PER_CH + t] = x[:, _lane_tile(t)]
            xi[kb * TILES_PER_CH + t] = x[:, _lane_tile(TILES_PER_CH + t)]

    delta = _project_delta(s_prev[...], c_ref[...], wout_ref)
    group_rows = SCAN_UNROLL * ns
    tiles_per_group = MODEL_TILES // (STATE_TILES // SCAN_TILE_GROUP)

    def after(group, tile_group):
        r0 = group * group_rows
        return delta[r0:r0 + SUBLANES, _lane_tile(tile_group * tiles_per_group)]

    _s5_scan(xr, xi, hcr, hci, ar_ref, ai_ref, ns=ns, tt=tt, after=after)
    _finish_projection(delta, x1_ref, gx_ref, wq_ref, x2_ref, q_ref, dtm, xn, ns=ns, tt=tt)
    s_prev[...] = _s5_readout(u, xr, xi, cre_ref, cim_ref, d_ref, wglu_ref)

    @pl.when(i == steps - 1)
    def _():
        _state_out(hre_out, him_out, hcr, hci)


def _mixer_tm(u_tm, c_tm, x13, w, *, tt):
    ns, seq_len, _ = x13.shape
    steps = seq_len // tt
    rows = ns * tt
    weights = [w[k] for k in _S5_WEIGHTS + _OUT_WEIGHTS]
    cur = lambda i: jnp.minimum(i, steps - 1)
    prev = lambda i: jnp.clip(i - 1, 0, steps - 1)
    tm_spec = lambda m: pl.BlockSpec((rows, SSM_DIM), lambda i: (m(i), 0))
    row_spec = pl.BlockSpec((ns, tt, D_MODEL), lambda i: (0, prev(i), 0))
    st_spec = pl.BlockSpec((ns, N_STATE), lambda i: (0, 0))
    return pl.pallas_call(
        functools.partial(_mixer_tm_kernel, ns=ns, tt=tt, steps=steps),
        grid=(steps + 1,),
        in_specs=[tm_spec(cur), tm_spec(prev), row_spec] + [_const_spec(a.shape) for a in weights],
        out_specs=(row_spec, row_spec, st_spec, st_spec),
        out_shape=(jax.ShapeDtypeStruct((ns, seq_len, D_MODEL), F32),
                   jax.ShapeDtypeStruct((ns, seq_len, D_MODEL), BF16),
                   jax.ShapeDtypeStruct((ns, N_STATE), F32), jax.ShapeDtypeStruct((ns, N_STATE), F32)),
        scratch_shapes=_mixer_scratch(ns, rows) + [pltpu.VMEM((rows, SSM_DIM), F32)],
        compiler_params=_params(("arbitrary",)),
        name="mixer_tm",
    )(u_tm, c_tm, x13, *weights)


def _block_diag_tiles(x):
    _, a, b = x.shape
    x4 = x.reshape(CH_TILES, GROUPS_PER_TILE, a, b)
    eye = jnp.eye(GROUPS_PER_TILE, dtype=x.dtype)
    out = x4[:, :, :, None, :] * eye[None, :, None, :, None]
    return out.reshape(CH_TILES, GROUPS_PER_TILE * a, GROUPS_PER_TILE * b)


def _row(x):
    return x.reshape(1, -1)


def _state_slabs(x):
    return jnp.broadcast_to(x.reshape(STATE_TILES, 1, LANES), (STATE_TILES, SUBLANES, LANES))


def kernel(x_prompt, x_sample, state_ssm_re, state_ssm_im, cache_conv, cache_mem_k, cache_mem_v, mem_prompt,
           g_mem, w_mem_k, w_mem_v, g_ffn1, w_ffn1_gate, w_ffn1_up, w_ffn1_down, g_mix, w_in,
           ssm_a_re, ssm_a_im, ssm_log_dt, ssm_b_re, ssm_b_im, ssm_c_re, ssm_c_im, ssm_d, w_ssm_glu,
           conv_w, conv_b, conv_ln_g, conv_ln_b, w_out, g_xattn, w_mem_q, w_mem_o,
           g_ffn2, w_ffn2_gate, w_ffn2_up, w_ffn2_down, g_final):
    depth = g_ffn1.shape[0]
    assert depth == 1
    l = 0
    bp, seq, _ = x_prompt.shape
    bs, dseq, _ = x_sample.shape
    bf = lambda a: a.astype(BF16)

    ab_re, ab_im, bbar_re, bbar_im = _s5_params(ssm_a_re[l], ssm_a_im[l], ssm_log_dt[l],
                                                jnp.swapaxes(ssm_b_re[l], 1, 2), jnp.swapaxes(ssm_b_im[l], 1, 2))
    mixer_w = dict(
        bblk=bf(jnp.concatenate([_block_diag_tiles(bbar_re), _block_diag_tiles(bbar_im)], axis=-1)),
        cre=bf(_block_diag_tiles(jnp.swapaxes(ssm_c_re[l], 1, 2))),
        cim=bf(-_block_diag_tiles(jnp.swapaxes(ssm_c_im[l], 1, 2))),
        ab_re=_state_slabs(ab_re), ab_im=_state_slabs(ab_im),
        d=_row(ssm_d[l]),
        conv_w=jnp.swapaxes(conv_w[l].reshape(CONV_WIDTH, CH_TILES, LANES), 0, 1),
        conv_b=_row(conv_b[l]), ln_g=_row(conv_ln_g[l]), ln_b=_row(conv_ln_b[l]),
        g_x=_row(g_xattn[l]),
    )
    g_mix_r, g_final_r = _row(g_mix[l]), _row(g_final)

    first = [w_ffn1_gate[l], w_ffn1_up[l], w_ffn1_down[l], w_in[l], w_ssm_glu[l]]
    mk, mv, mk_hm, mv_hm, w1g_b, w1u_b, w1d_b, w_in_b, w_glu_b = _mem_kv(
        mem_prompt.reshape(bp * N_MEM, D_MODEL), _row(g_mem[l]), bf(w_mem_k[l]), bf(w_mem_v[l]), first)
    ffn1 = (_row(g_ffn1[l]), w1g_b, w1u_b, w1d_b)
    mixer_w["w_glu"] = w_glu_b

    p_rows, s_rows = bp * seq, bs * dseq
    prompt_tt = 64
    later = [w_out[l], w_mem_q[l], w_mem_o[l], w_ffn2_gate[l], w_ffn2_up[l], w_ffn2_down[l]]
    x1p, u_tm, c_tm, p_buf, w_out_b, w_q_b, w_o_b, w2g_b, w2u_b, w2d_b = _front(
        x_prompt, *ffn1, g_mix_r, w_in_b, *[mixer_w[k] for k in _CONV_WEIGHTS], later, tt=prompt_tt)
    mixer_w.update(w_out=w_out_b, w_q=w_q_b)
    ffn2 = (_row(g_ffn2[l]), w2g_b, w2u_b, w2d_b)
    x2p, qp, p_re, p_im = _mixer_tm(u_tm, c_tm, x1p, mixer_w, tt=prompt_tt)

    x1s, zs = _ffn_in(x_sample.reshape(s_rows, D_MODEL), *ffn1, g_mix_r, w_in_b)
    x2s, qs, s_re, s_im, s_buf = _mixer(
        zs.reshape(bs, dseq, -1), x1s.reshape(bs, dseq, D_MODEL),
        state_ssm_re[l].reshape(bs, N_STATE), state_ssm_im[l].reshape(bs, N_STATE),
        jnp.swapaxes(cache_conv[l], 0, 1), mixer_w, ns=32)

    kv_rows = lambda a: a.reshape(bs * N_MEM, MEM_HEADS, MEM_HEAD_DIM)
    yp, ys = _tail(x2p.reshape(p_rows, D_MODEL), qp.reshape(p_rows, D_MODEL), mk_hm, mv_hm,
                   x2s.reshape(s_rows, D_MODEL), qs.reshape(s_rows, D_MODEL),
                   kv_rows(cache_mem_k), kv_rows(cache_mem_v), w_o_b, *ffn2, g_final_r, tm=256)
    yp = yp.reshape(bp, seq, D_MODEL)
    ys = ys.reshape(bs, dseq, D_MODEL)

    st = lambda a, n: a.reshape(1, n, SSM_GROUPS, SSM_STATE)
    kv = lambda a: a.reshape(1, bp, N_MEM, MEM_HEADS, MEM_HEAD_DIM)
    buf = lambda a: jnp.swapaxes(a, 0, 1)[None]
    return (yp, ys, st(p_re, bp), st(p_im, bp), buf(p_buf), kv(mk), kv(mv),
            st(s_re, bs), st(s_im, bs), buf(s_buf))
```

```python
import functools

import jax
import jax.numpy as jnp
from jax import lax
from jax.experimental import pallas as pl
from jax.experimental.pallas import tpu as pltpu

F32 = jnp.float32
BF16 = jnp.bfloat16

EPS = 1e-6
D_MODEL = 1024
D_FF = 2816
SSM_DIM = 512
CONV_DIM = 512
SSM_GROUPS = 32
SSM_GROUP_CH = 16
SSM_STATE = 64
N_STATE = SSM_GROUPS * SSM_STATE
CONV_WIDTH = 31
N_MEM = 256
MEM_HEADS = 4
MEM_HEAD_DIM = 256

LANES = 128
SUBLANES = 8
STATE_TILES = N_STATE // LANES
CH_TILES = SSM_DIM // LANES
MODEL_TILES = D_MODEL // LANES
GROUPS_PER_TILE = LANES // SSM_GROUP_CH
STATES_PER_CH_TILE = GROUPS_PER_TILE * SSM_STATE
TILES_PER_CH = STATES_PER_CH_TILE // LANES
HIST = 32
FIRST_TAP = HIST - (CONV_WIDTH - 1)
SCAN_TILE_GROUP = 4
CONV_ROWS = 64
VMEM_LIMIT = 56 * 1024 * 1024


def _dot(a, b):
    return jnp.dot(a, b, preferred_element_type=F32)


def _rms(x, g):
    return x * lax.rsqrt(jnp.mean(x * x, axis=-1, keepdims=True) + EPS) * g


def _lane_tile(j):
    return slice(j * LANES, (j + 1) * LANES)


def _const_spec(shape):
    zeros = (0,) * len(shape)
    return pl.BlockSpec(shape, lambda *_: zeros, pipeline_mode=pl.Buffered(1))


VMEM_LIMIT_HIGH = 62 * 1024 * 1024


def _params(sem, vmem_limit=VMEM_LIMIT):
    return pltpu.CompilerParams(dimension_semantics=sem, vmem_limit_bytes=vmem_limit)


def _s5_params_kernel(ar_ref, ai_ref, ldt_ref, bre_ref, bim_ref, abr_ref, abi_ref, bbr_ref, bbi_ref):
    lr = ar_ref[...]
    li = ai_ref[...]
    dt = jnp.exp(ldt_ref[...])
    mag = jnp.exp(lr * dt)
    ab_re = mag * jnp.cos(li * dt)
    ab_im = mag * jnp.sin(li * dt)
    den = lr * lr + li * li
    nr = ab_re - 1.0
    ni = ab_im
    coef_re = (nr * lr + ni * li) / den
    coef_im = (ni * lr - nr * li) / den
    abr_ref[...] = ab_re
    abi_ref[...] = ab_im
    b_re = bre_ref[...]
    b_im = bim_ref[...]
    bbr_ref[...] = coef_re * b_re - coef_im * b_im
    bbi_ref[...] = coef_re * b_im + coef_im * b_re


def _s5_params(a_re, a_im, log_dt, b_re, b_im):
    a_shape = jax.ShapeDtypeStruct((SSM_GROUPS, 1, SSM_STATE), F32)
    b_shape = jax.ShapeDtypeStruct(b_re.shape, F32)
    return pl.pallas_call(
        _s5_params_kernel,
        out_shape=(a_shape, a_shape, b_shape, b_shape),
        name="s5_params",
    )(a_re.reshape(a_shape.shape), a_im.reshape(a_shape.shape), log_dt.reshape(SSM_GROUPS, 1, 1), b_re, b_im)


BF16_SUBLANES = 16


def _cast_blocking(w, max_steps):
    n_rows = w.shape[0]
    n_blocks = max(n for n in range(1, max_steps + 1)
                   if n_rows % n == 0 and (n_rows // n) % BF16_SUBLANES == 0)
    return n_rows // n_blocks, n_blocks


def _cast_specs(to_cast, max_steps):
    specs = []
    for w in to_cast:
        blk_rows, n_blocks = _cast_blocking(w, max_steps)
        specs.append(pl.BlockSpec((blk_rows, w.shape[1]),
                                  (lambda n: (lambda j: (jnp.minimum(j, n - 1), 0)))(n_blocks)))
    return specs


def _cast_blocks(cast_in, cast_out):
    for src, dst in zip(cast_in, cast_out):
        dst[...] = src[...].astype(BF16)


def _mem_kv_kernel(*refs, n_cast):
    m_ref, g_ref, wk_ref, wv_ref = refs[:4]
    k_ref, v_ref, kh_ref, vh_ref = refs[4 + n_cast:8 + n_cast]
    _cast_blocks(refs[4:4 + n_cast], refs[8 + n_cast:])
    m = _rms(m_ref[...], g_ref[...]).astype(BF16)
    k = _dot(m, wk_ref[...])
    v = _dot(m, wv_ref[...])
    for h in range(MEM_HEADS):
        cols = slice(h * MEM_HEAD_DIM, (h + 1) * MEM_HEAD_DIM)
        k_ref[:, h, :] = k[:, cols]
        v_ref[:, h, :] = v[:, cols]
        kh_ref[0, h] = k[:, cols].T.astype(BF16)
        vh_ref[0, h] = v[:, cols].astype(BF16)


def _mem_kv(mem2d, g, wk, wv, to_cast):
    rows = mem2d.shape[0]
    nb = rows // N_MEM
    cast_specs = _cast_specs(to_cast, nb)
    row_spec = pl.BlockSpec((N_MEM, D_MODEL), lambda i: (i, 0))
    head_spec = pl.BlockSpec((N_MEM, MEM_HEADS, MEM_HEAD_DIM), lambda i: (i, 0, 0))
    hm_spec = pl.BlockSpec((1, MEM_HEADS, N_MEM, MEM_HEAD_DIM), lambda i: (i, 0, 0, 0))
    out = jax.ShapeDtypeStruct((rows, MEM_HEADS, MEM_HEAD_DIM), F32)
    out_hm = jax.ShapeDtypeStruct((nb, MEM_HEADS, N_MEM, MEM_HEAD_DIM), BF16)
    return pl.pallas_call(
        functools.partial(_mem_kv_kernel, n_cast=len(to_cast)),
        grid=(nb,),
        in_specs=[row_spec, _const_spec((1, D_MODEL)),
                  _const_spec((D_MODEL, D_MODEL)), _const_spec((D_MODEL, D_MODEL))] + cast_specs,
        out_specs=(head_spec, head_spec, hm_spec, hm_spec) + tuple(cast_specs),
        out_shape=(out, out, out_hm, out_hm) + tuple(jax.ShapeDtypeStruct(w.shape, BF16) for w in to_cast),
        compiler_params=_params(("arbitrary",)),
        name="mem_kv",
    )(mem2d, g, wk, wv, *to_cast)


def _zero_after(piece):
    bits = pltpu.bitcast(piece, jnp.uint32)
    return pltpu.bitcast((bits >> 16) >> 16, F32)


def _conv_rows(vext, ytm, cw_ref, j, base, *, ns, after):
    acc = _zero_after(after)
    for k in range(CONV_WIDTH):
        acc = acc + cw_ref[j, k:k + 1, :] * vext[j, pl.ds(base + (FIRST_TAP + k) * ns, CONV_ROWS), :]
    ytm[j, pl.ds(base, CONV_ROWS), :] = acc


def _ln_swish(ytm, cb_ref, lng_ref, lnb_ref):
    yc = jnp.concatenate([ytm[j] for j in range(CH_TILES)], axis=1) + cb_ref[...]
    mu = jnp.mean(yc, axis=-1, keepdims=True)
    var = jnp.mean(jnp.square(yc - mu), axis=-1, keepdims=True)
    yn = (yc - mu) * lax.rsqrt(var + EPS) * lng_ref[...] + lnb_ref[...]
    return yn * jax.nn.sigmoid(yn)


def _swiglu_half(x, g, wg_ref, wu_ref, wd_ref):
    hn = _rms(x, g).astype(BF16)
    gate = _dot(hn, wg_ref[...])
    up = _dot(hn, wu_ref[...])
    act = (gate * jax.nn.sigmoid(gate) * up).astype(BF16)
    return x + 0.5 * _dot(act, wd_ref[...])


def _ffn_in_kernel(x_ref, g1_ref, wg_ref, wu_ref, wd_ref, g2_ref, win_ref, x1_ref, z_ref):
    x1 = _swiglu_half(x_ref[...], g1_ref[...], wg_ref, wu_ref, wd_ref)
    x1_ref[...] = x1
    z_ref[...] = _dot(_rms(x1, g2_ref[...]).astype(BF16), win_ref[...])


def _ffn_in(x2d, g1, wg, wu, wd, g2, win, tm=512):
    rows = x2d.shape[0]
    zdim = win.shape[1]
    row_spec = pl.BlockSpec((tm, D_MODEL), lambda i: (i, 0))
    return pl.pallas_call(
        _ffn_in_kernel,
        grid=(rows // tm,),
        in_specs=[row_spec, _const_spec((1, D_MODEL)),
                  _const_spec((D_MODEL, D_FF)), _const_spec((D_MODEL, D_FF)), _const_spec((D_FF, D_MODEL)),
                  _const_spec((1, D_MODEL)), _const_spec((D_MODEL, zdim))],
        out_specs=(row_spec, pl.BlockSpec((tm, zdim), lambda i: (i, 0))),
        out_shape=(jax.ShapeDtypeStruct((rows, D_MODEL), F32), jax.ShapeDtypeStruct((rows, zdim), F32)),
        compiler_params=_params(("arbitrary",)),
        name="ffn_in",
    )(x2d, g1, wg, wu, wd, g2, win)


def _front_kernel(*refs, ns, tt, steps, n_cast):
    (x_ref, g1_ref, wg_ref, wu_ref, wd_ref, g2_ref, win_ref, cw_ref, cb_ref, lng_ref, lnb_ref) = refs[:11]
    cast_in = refs[11:11 + n_cast]
    x1_ref, u_ref, c_ref, hist_ref = refs[11 + n_cast:15 + n_cast]
    cast_out = refs[15 + n_cast:15 + 2 * n_cast]
    vext, ytm, utm = refs[15 + 2 * n_cast:]
    j = pl.program_id(0)
    rows = ns * tt
    hist_rows = HIST * ns
    gate_tile_step = (D_FF // LANES) // CH_TILES

    @pl.when(j == 0)
    def _():
        vext[...] = jnp.zeros_like(vext)

    x = x_ref[...].reshape(rows, D_MODEL)
    hn = _rms(x, g1_ref[...]).astype(BF16)
    gate = _dot(hn, wg_ref[...])
    for jt in range(CH_TILES):
        for b in range(rows // CONV_ROWS):
            _conv_rows(vext, ytm, cw_ref, jt, b * CONV_ROWS, ns=ns,
                       after=gate[b * CONV_ROWS:(b + 1) * CONV_ROWS, _lane_tile(jt * gate_tile_step)])
    c_ref[...] = _ln_swish(ytm, cb_ref, lng_ref, lnb_ref)
    up = _dot(hn, wu_ref[...])
    act = (gate * jax.nn.sigmoid(gate) * up).astype(BF16)
    x1 = x + 0.5 * _dot(act, wd_ref[...])
    x1_ref[...] = x1.reshape(ns, tt, D_MODEL)
    z = _dot(_rms(x1, g2_ref[...]).astype(BF16), win_ref[...])
    u = z[:, :SSM_DIM]
    v = z[:, SSM_DIM:SSM_DIM + CONV_DIM] * jax.nn.sigmoid(z[:, SSM_DIM + CONV_DIM:])

    vext[:, :hist_rows, :] = vext[:, rows:rows + hist_rows, :]
    for s in range(ns):
        seq = slice(s * tt, (s + 1) * tt)
        for jt in range(CH_TILES):
            vext[jt, pl.ds(hist_rows + s, tt, stride=ns), :] = v[seq, _lane_tile(jt)]
            utm[jt, pl.ds(s, tt, stride=ns), :] = u[seq, _lane_tile(jt)]
    u_ref[...] = jnp.concatenate([utm[jt] for jt in range(CH_TILES)], axis=1)

    _cast_blocks(cast_in, cast_out)

    @pl.when(j == steps)
    def _():
        for jt in range(CH_TILES):
            hist_ref[:, :, _lane_tile(jt)] = (
                vext[jt, FIRST_TAP * ns:hist_rows, :].reshape(CONV_WIDTH - 1, ns, LANES))


def _front(x3, g1, wg, wu, wd, g2, win, cw, cb, lng, lnb, to_cast, *, tt):
    ns, seq_len, _ = x3.shape
    steps = seq_len // tt
    rows = ns * tt
    assert rows % CONV_ROWS == 0 and ns == SUBLANES
    zdim = win.shape[1]
    cur = lambda j: jnp.minimum(j, steps - 1)
    prev = lambda j: jnp.clip(j - 1, 0, steps - 1)
    x_spec = pl.BlockSpec((ns, tt, D_MODEL), lambda j: (0, cur(j), 0))
    tm_spec = lambda m: pl.BlockSpec((rows, SSM_DIM), lambda j: (m(j), 0))
    cast_specs = _cast_specs(to_cast, steps + 1)
    return pl.pallas_call(
        functools.partial(_front_kernel, ns=ns, tt=tt, steps=steps, n_cast=len(to_cast)),
        grid=(steps + 1,),
        in_specs=[x_spec, _const_spec((1, D_MODEL)),
                  _const_spec((D_MODEL, D_FF)), _const_spec((D_MODEL, D_FF)), _const_spec((D_FF, D_MODEL)),
                  _const_spec((1, D_MODEL)), _const_spec((D_MODEL, zdim)),
                  _const_spec(cw.shape), _const_spec(cb.shape), _const_spec(lng.shape), _const_spec(lnb.shape)]
                 + cast_specs,
        out_specs=(x_spec, tm_spec(cur), tm_spec(prev),
                   pl.BlockSpec((CONV_WIDTH - 1, ns, CONV_DIM), lambda j: (0, 0, 0))) + tuple(cast_specs),
        out_shape=(jax.ShapeDtypeStruct((ns, seq_len, D_MODEL), F32),
                   jax.ShapeDtypeStruct((seq_len * ns, SSM_DIM), F32),
                   jax.ShapeDtypeStruct((seq_len * ns, CONV_DIM), F32),
                   jax.ShapeDtypeStruct((CONV_WIDTH - 1, ns, CONV_DIM), F32))
                  + tuple(jax.ShapeDtypeStruct(w.shape, BF16) for w in to_cast),
        scratch_shapes=[pltpu.VMEM((CH_TILES, rows + HIST * ns, LANES), F32),
                        pltpu.VMEM((CH_TILES, rows, LANES), F32),
                        pltpu.VMEM((CH_TILES, rows, LANES), F32)],
        compiler_params=_params(("arbitrary",)),
        name="front",
    )(x3, g1, wg, wu, wd, g2, win, cw, cb, lng, lnb, *to_cast)


def _softmax_rows(sc):
    e = jnp.exp(sc - jnp.max(sc, axis=-1, keepdims=True))
    return e / jnp.sum(e, axis=-1, keepdims=True)


def _attn_long_scores(q_ref, k_ref):
    scale = MEM_HEAD_DIM ** -0.5
    return [_dot(q_ref[:, h * MEM_HEAD_DIM:(h + 1) * MEM_HEAD_DIM], k_ref[0, h]) * scale
            for h in range(MEM_HEADS)]


def _attn_long_values(scores, v_ref):
    return jnp.concatenate([_dot(_softmax_rows(sc).astype(BF16), v_ref[0, h]).astype(BF16)
                            for h, sc in enumerate(scores)], axis=1)


def _attn_short_scores(q_ref, k_ref, *, nseq, lq):
    scale = MEM_HEAD_DIM ** -0.5
    kv_rows = N_MEM * MEM_HEADS
    shape = (MEM_HEADS * lq, kv_rows)
    same_head = (lax.broadcasted_iota(jnp.int32, shape, 0) // lq
                 == lax.broadcasted_iota(jnp.int32, shape, 1) % MEM_HEADS)
    scores = []
    for s in range(nseq):
        rows = slice(s * lq, (s + 1) * lq)
        q = jnp.concatenate([q_ref[rows, h * MEM_HEAD_DIM:(h + 1) * MEM_HEAD_DIM] for h in range(MEM_HEADS)],
                            axis=0)
        k = k_ref[s * kv_rows:(s + 1) * kv_rows, :]
        sc = lax.dot_general(q, k, (((1,), (1,)), ((), ())), preferred_element_type=F32) * scale
        scores.append(jnp.where(same_head, sc, -jnp.inf))
    return scores


def _attn_short_values(scores, v_ref, *, lq):
    kv_rows = N_MEM * MEM_HEADS
    out = []
    for s, sc in enumerate(scores):
        v = v_ref[s * kv_rows:(s + 1) * kv_rows, :]
        o = _dot(_softmax_rows(sc).astype(BF16), v)
        out.append(jnp.concatenate([o[h * lq:(h + 1) * lq, :] for h in range(MEM_HEADS)], axis=1))
    return jnp.concatenate(out, axis=0)


def _tail_kernel(x2a_ref, qa_ref, ka_ref, va_ref, x2b_ref, qb_ref, kb_ref, vb_ref,
                 wo_ref, g_ref, wg_ref, wu_ref, wd_ref, gf_ref, ya_ref, yb_ref, oa_scr, ob_scr,
                 *, a_steps, tm, nseq, lq):
    j = pl.program_id(0)
    in_b = j > a_steps

    @pl.when(j == 0)
    def _():
        oa_scr[...] = jnp.zeros_like(oa_scr)
        ob_scr[...] = jnp.zeros_like(ob_scr)

    b_tile = pl.ds(pl.multiple_of(jnp.maximum(j - a_steps - 1, 0) * tm, tm), tm)
    o = jnp.where(in_b, ob_scr[b_tile, :], oa_scr[...])
    x2 = jnp.where(in_b, x2b_ref[...], x2a_ref[...])

    x3 = x2 + _dot(o, wo_ref[...])
    hn = _rms(x3, g_ref[...]).astype(BF16)
    gate = _dot(hn, wg_ref[...])
    up = _dot(hn, wu_ref[...])
    sc_a = _attn_long_scores(qa_ref, ka_ref)
    sc_b = _attn_short_scores(qb_ref, kb_ref, nseq=nseq, lq=lq)
    act = (gate * jax.nn.sigmoid(gate) * up).astype(BF16)
    x4 = x3 + 0.5 * _dot(act, wd_ref[...])

    b_rows = nseq * lq
    blk = jnp.minimum(j, a_steps - 1)
    ob_scr[pl.ds(pl.multiple_of(blk * b_rows, b_rows), b_rows), :] = (
        _attn_short_values(sc_b, vb_ref, lq=lq).astype(BF16))
    oa_scr[...] = _attn_long_values(sc_a, va_ref)

    y = _rms(x4, gf_ref[...])

    @pl.when(jnp.logical_and(j >= 1, j <= a_steps))
    def _():
        ya_ref[...] = y

    @pl.when(in_b)
    def _():
        yb_ref[...] = y


def _tail(x2a, qa, ka_hm, va_hm, x2b, qb, kb2d, vb2d, wo, g, wg, wu, wd, gf, *, tm):
    a_rows, b_rows = x2a.shape[0], x2b.shape[0]
    a_steps, b_steps = a_rows // tm, b_rows // tm
    n_b = kb2d.shape[0] // (N_MEM * MEM_HEADS)
    lq = b_rows // n_b
    assert n_b % a_steps == 0
    nseq = n_b // a_steps
    tiles_per_seq = a_rows // ka_hm.shape[0] // tm
    attn_blk = lambda j: jnp.minimum(j, a_steps - 1)
    a_fin = lambda j: (jnp.clip(j - 1, 0, a_steps - 1), 0)
    b_fin = lambda j: (jnp.maximum(j - a_steps - 1, 0), 0)
    row = lambda m: pl.BlockSpec((tm, D_MODEL), m)
    ka_spec = pl.BlockSpec((1, MEM_HEADS, N_MEM, MEM_HEAD_DIM),
                           lambda j: (attn_blk(j) // tiles_per_seq, 0, 0, 0))
    kb_spec = pl.BlockSpec((nseq * N_MEM * MEM_HEADS, MEM_HEAD_DIM), lambda j: (attn_blk(j), 0))
    return pl.pallas_call(
        functools.partial(_tail_kernel, a_steps=a_steps, tm=tm, nseq=nseq, lq=lq),
        grid=(a_steps + 1 + b_steps,),
        in_specs=[row(a_fin), row(lambda j: (attn_blk(j), 0)), ka_spec, ka_spec,
                  row(b_fin), pl.BlockSpec((nseq * lq, D_MODEL), lambda j: (attn_blk(j), 0)), kb_spec, kb_spec,
                  _const_spec((D_MODEL, D_MODEL)), _const_spec((1, D_MODEL)),
                  _const_spec((D_MODEL, D_FF)), _const_spec((D_MODEL, D_FF)), _const_spec((D_FF, D_MODEL)),
                  _const_spec((1, D_MODEL))],
        out_specs=(row(a_fin), row(b_fin)),
        out_shape=(jax.ShapeDtypeStruct((a_rows, D_MODEL), F32), jax.ShapeDtypeStruct((b_rows, D_MODEL), F32)),
        scratch_shapes=[pltpu.VMEM((tm, D_MODEL), BF16), pltpu.VMEM((b_rows, D_MODEL), BF16)],
        compiler_params=_params(("arbitrary",), VMEM_LIMIT_HIGH),
        name="tail",
    )(x2a, qa, ka_hm, va_hm, x2b, qb, kb2d, vb2d, wo, g, wg, wu, wd, gf)


def _s5_input_and_conv(utm, vext, ytm, xr, xi, bblk_ref, cw_ref, *, ns, rows):
    def body(kb, carry):
        x = _dot(utm[kb].astype(BF16), bblk_ref[kb])
        for i in range(TILES_PER_CH):
            xr[kb * TILES_PER_CH + i] = x[:, _lane_tile(i)]
            xi[kb * TILES_PER_CH + i] = x[:, _lane_tile(TILES_PER_CH + i)]
        for b in range(rows // CONV_ROWS):
            _conv_rows(vext, ytm, cw_ref, kb, b * CONV_ROWS, ns=ns,
                       after=x[b * CONV_ROWS:(b + 1) * CONV_ROWS, :LANES])
        return carry

    lax.fori_loop(0, CH_TILES, body, 0)


SCAN_UNROLL = 8


def _s5_scan(xr, xi, hcr, hci, ar_ref, ai_ref, *, ns, tt, after=None):
    def step(row, carry, tiles, a_r, a_i):
        new_r, new_i = [], []
        for idx, j in enumerate(tiles):
            h_r, h_i = carry[idx], carry[SCAN_TILE_GROUP + idx]
            n_r = a_r[idx] * h_r - a_i[idx] * h_i + xr[j, row, :]
            n_i = a_r[idx] * h_i + a_i[idx] * h_r + xi[j, row, :]
            xr[j, row, :] = n_r
            xi[j, row, :] = n_i
            new_r.append(n_r)
            new_i.append(n_i)
        return tuple(new_r) + tuple(new_i)

    for i in range(ns // SUBLANES):
        srow = slice(i * SUBLANES, (i + 1) * SUBLANES)
        for jg in range(STATE_TILES // SCAN_TILE_GROUP):
            tiles = tuple(range(jg * SCAN_TILE_GROUP, (jg + 1) * SCAN_TILE_GROUP))
            a_r = [ar_ref[j] for j in tiles]
            a_i = [ai_ref[j] for j in tiles]
            carry = tuple(hcr[j, srow, :] for j in tiles) + tuple(hci[j, srow, :] for j in tiles)
            if after is None:
                def body(t, c, tiles=tiles, a_r=a_r, a_i=a_i, i=i):
                    row = pl.ds(pl.multiple_of(t * ns + i * SUBLANES, SUBLANES), SUBLANES)
                    return step(row, c, tiles, a_r, a_i)

                carry = lax.fori_loop(0, tt, body, carry, unroll=SCAN_UNROLL)
            else:
                for t in range(tt):
                    if t % SCAN_UNROLL == 0:
                        zero = _zero_after(after(t // SCAN_UNROLL, jg))
                        carry = tuple(c + zero for c in carry)
                    carry = step(pl.ds(t * ns + i * SUBLANES, SUBLANES), carry, tiles, a_r, a_i)
            for idx, j in enumerate(tiles):
                hcr[j, srow, :] = carry[idx]
                hci[j, srow, :] = carry[SCAN_TILE_GROUP + idx]


def _s5_readout(u_all, xr, xi, cre_ref, cim_ref, d_ref, wglu_ref):
    ys = []
    for ob in range(CH_TILES):
        h_re = jnp.concatenate([xr[ob * TILES_PER_CH + i] for i in range(TILES_PER_CH)], axis=1).astype(BF16)
        h_im = jnp.concatenate([xi[ob * TILES_PER_CH + i] for i in range(TILES_PER_CH)], axis=1).astype(BF16)
        ys.append(_dot(h_re, cre_ref[ob]) + _dot(h_im, cim_ref[ob]))
    y = jnp.concatenate(ys, axis=1) + d_ref[...] * u_all
    g = jax.nn.gelu(y)
    return g * jax.nn.sigmoid(_dot(g.astype(BF16), wglu_ref[...]))


def _project_delta(s_out, c_out, wout_ref):
    return _dot(jnp.concatenate([s_out, c_out], axis=1).astype(BF16), wout_ref[...])


def _finish_projection(delta, x1_ref, gx_ref, wq_ref, x2_ref, q_ref, dtm, xn, *, ns, tt):
    for j in range(MODEL_TILES):
        dtm[j] = delta[:, _lane_tile(j)]
    for s in range(ns):
        d_s = jnp.concatenate([dtm[j, pl.ds(s, tt, stride=ns), :] for j in range(MODEL_TILES)], axis=1)
        x2_s = x1_ref[s] + d_s
        x2_ref[s] = x2_s
        xn[s * tt:(s + 1) * tt, :] = _rms(x2_s, gx_ref[...])
    qv = _dot(xn[...].astype(BF16), wq_ref[...]).astype(BF16)
    for s in range(ns):
        q_ref[s] = qv[s * tt:(s + 1) * tt, :]


def _state_out(hre_out, him_out, hcr, hci):
    hre_out[...] = jnp.concatenate([hcr[j] for j in range(STATE_TILES)], axis=1)
    him_out[...] = jnp.concatenate([hci[j] for j in range(STATE_TILES)], axis=1)


_S5_WEIGHTS = ("bblk", "cre", "cim", "ab_re", "ab_im", "d", "w_glu")
_CONV_WEIGHTS = ("conv_w", "conv_b", "ln_g", "ln_b")
_OUT_WEIGHTS = ("w_out", "g_x", "w_q")


def _mixer_scratch(ns, rows):
    return [pltpu.VMEM((STATE_TILES, rows, LANES), F32),
            pltpu.VMEM((STATE_TILES, rows, LANES), F32),
            pltpu.VMEM((STATE_TILES, ns, LANES), F32),
            pltpu.VMEM((STATE_TILES, ns, LANES), F32),
            pltpu.VMEM((MODEL_TILES, rows, LANES), F32),
            pltpu.VMEM((rows, D_MODEL), F32)]


def _mixer_kernel(u_ref, val_ref, gate_ref, x1_ref, h0r_ref, h0i_ref, cache_ref,
                  bblk_ref, cre_ref, cim_ref, ar_ref, ai_ref, d_ref, wglu_ref,
                  cw_ref, cb_ref, lng_ref, lnb_ref, wout_ref, gx_ref, wq_ref,
                  x2_ref, q_ref, hre_out, him_out, hist_out,
                  xr, xi, hcr, hci, dtm, xn, utm, vext, ytm, *, ns, tt):
    rows = ns * tt
    hist_rows = HIST * ns

    for j in range(STATE_TILES):
        hcr[j] = h0r_ref[:, _lane_tile(j)]
        hci[j] = h0i_ref[:, _lane_tile(j)]
    for j in range(CH_TILES):
        vext[j, FIRST_TAP * ns:hist_rows, :] = (
            cache_ref[:, :, _lane_tile(j)].reshape((CONV_WIDTH - 1) * ns, LANES))

    for s in range(ns):
        u_s = u_ref[s]
        v_s = val_ref[s] * jax.nn.sigmoid(gate_ref[s])
        for j in range(CH_TILES):
            utm[j, pl.ds(s, tt, stride=ns), :] = u_s[:, _lane_tile(j)]
            vext[j, pl.ds(hist_rows + s, tt, stride=ns), :] = v_s[:, _lane_tile(j)]

    _s5_input_and_conv(utm, vext, ytm, xr, xi, bblk_ref, cw_ref, ns=ns, rows=rows)
    _s5_scan(xr, xi, hcr, hci, ar_ref, ai_ref, ns=ns, tt=tt)
    u_all = jnp.concatenate([utm[j] for j in range(CH_TILES)], axis=1)
    s_out = _s5_readout(u_all, xr, xi, cre_ref, cim_ref, d_ref, wglu_ref)
    c_out = _ln_swish(ytm, cb_ref, lng_ref, lnb_ref)
    delta = _project_delta(s_out, c_out, wout_ref)
    _finish_projection(delta, x1_ref, gx_ref, wq_ref, x2_ref, q_ref, dtm, xn, ns=ns, tt=tt)
    _state_out(hre_out, him_out, hcr, hci)
    for j in range(CH_TILES):
        hist_out[:, :, _lane_tile(j)] = (
            vext[j, (tt + FIRST_TAP) * ns:(tt + HIST) * ns, :].reshape(CONV_WIDTH - 1, ns, LANES))


def _mixer(z3, x13, h0_re, h0_im, hist, w, *, ns):
    nseq, tt, _ = z3.shape
    rows = ns * tt
    assert nseq % ns == 0 and rows % CONV_ROWS == 0 and ns % SUBLANES == 0
    st_spec = pl.BlockSpec((ns, N_STATE), lambda i: (i, 0))
    hist_spec = pl.BlockSpec((CONV_WIDTH - 1, ns, CONV_DIM), lambda i: (0, i, 0))
    row_spec = pl.BlockSpec((ns, tt, D_MODEL), lambda i: (i, 0, 0))
    z_specs = [pl.BlockSpec((ns, tt, SSM_DIM), (lambda c: (lambda i: (i, 0, c)))(c)) for c in range(3)]
    weights = [w[k] for k in _S5_WEIGHTS + _CONV_WEIGHTS + _OUT_WEIGHTS]
    return pl.pallas_call(
        functools.partial(_mixer_kernel, ns=ns, tt=tt),
        grid=(nseq // ns,),
        in_specs=z_specs + [row_spec, st_spec, st_spec, hist_spec] + [_const_spec(a.shape) for a in weights],
        out_specs=(row_spec, row_spec, st_spec, st_spec, hist_spec),
        out_shape=(jax.ShapeDtypeStruct((nseq, tt, D_MODEL), F32),
                   jax.ShapeDtypeStruct((nseq, tt, D_MODEL), BF16),
                   jax.ShapeDtypeStruct((nseq, N_STATE), F32), jax.ShapeDtypeStruct((nseq, N_STATE), F32),
                   jax.ShapeDtypeStruct((CONV_WIDTH - 1, nseq, CONV_DIM), F32)),
        scratch_shapes=_mixer_scratch(ns, rows) + [
            pltpu.VMEM((CH_TILES, rows, LANES), F32),
            pltpu.VMEM((CH_TILES, rows + HIST * ns, LANES), F32),
            pltpu.VMEM((CH_TILES, rows, LANES), F32)],
        compiler_params=_params(("arbitrary",)),
        name="mixer",
    )(z3, z3, z3, x13, h0_re, h0_im, hist, *weights)


def _mixer_tm_kernel(u_ref, c_ref, x1_ref, bblk_ref, cre_ref, cim_ref, ar_ref, ai_ref, d_ref, wglu_ref,
                     wout_ref, gx_ref, wq_ref, kf_ref, vf_ref, x2_ref, q_ref, hre_out, him_out, kb_ref, vb_ref,
                     xr, xi, hcr, hci, dtm, xn, s_prev, *, ns, tt, steps):
    i = pl.program_id(0)

    @pl.when(i == 0)
    def _():
        hcr[...] = jnp.zeros_like(hcr)
        hci[...] = jnp.zeros_like(hci)
        s_prev[...] = jnp.zeros_like(s_prev)

    for src, dst in ((kf_ref, kb_ref), (vf_ref, vb_ref)):
        dst[...] = src.reshape(dst.shape)[...].astype(BF16)

    u = u_ref[...]
    for kb in range(CH_TILES):
        x = _dot(u[:, _lane_tile(kb)].astype(BF16), bblk_ref[kb])
        for t in range(TILES_PER_CH):
            xr[kb * TILES_PER_CH + t] = x[:, _lane_tile(t)]
            xi[kb * TILES_PER_CH + t] = x[:, _lane_tile(TILES_PER_CH + t)]

    delta = _project_delta(s_prev[...], c_ref[...], wout_ref)
    group_rows = SCAN_UNROLL * ns
    tiles_per_group = MODEL_TILES // (STATE_TILES // SCAN_TILE_GROUP)

    def after(group, tile_group):
        r0 = group * group_rows
        return delta[r0:r0 + SUBLANES, _lane_tile(tile_group * tiles_per_group)]

    _s5_scan(xr, xi, hcr, hci, ar_ref, ai_ref, ns=ns, tt=tt, after=after)
    _finish_projection(delta, x1_ref, gx_ref, wq_ref, x2_ref, q_ref, dtm, xn, ns=ns, tt=tt)
    s_prev[...] = _s5_readout(u, xr, xi, cre_ref, cim_ref, d_ref, wglu_ref)

    @pl.when(i == steps - 1)
    def _():
        _state_out(hre_out, him_out, hcr, hci)


def _mixer_tm(u_tm, c_tm, x13, w, k_cache, v_cache, *, tt):
    ns, seq_len, _ = x13.shape
    steps = seq_len // tt
    rows = ns * tt
    weights = [w[k] for k in _S5_WEIGHTS + _OUT_WEIGHTS]
    cur = lambda i: jnp.minimum(i, steps - 1)
    prev = lambda i: jnp.clip(i - 1, 0, steps - 1)
    tm_spec = lambda m: pl.BlockSpec((rows, SSM_DIM), lambda i: (m(i), 0))
    row_spec = pl.BlockSpec((ns, tt, D_MODEL), lambda i: (0, prev(i), 0))
    st_spec = pl.BlockSpec((ns, N_STATE), lambda i: (0, 0))
    kv_tokens = k_cache.shape[0]
    assert kv_tokens % steps == 0
    kv_blk = kv_tokens // steps
    kf_spec = pl.BlockSpec((kv_blk, MEM_HEADS, MEM_HEAD_DIM), lambda i: (cur(i), 0, 0))
    kb_spec = pl.BlockSpec((kv_blk * MEM_HEADS, MEM_HEAD_DIM), lambda i: (cur(i), 0))
    kv_out = jax.ShapeDtypeStruct((kv_tokens * MEM_HEADS, MEM_HEAD_DIM), BF16)
    return pl.pallas_call(
        functools.partial(_mixer_tm_kernel, ns=ns, tt=tt, steps=steps),
        grid=(steps + 1,),
        in_specs=[tm_spec(cur), tm_spec(prev), row_spec] + [_const_spec(a.shape) for a in weights]
                 + [kf_spec, kf_spec],
        out_specs=(row_spec, row_spec, st_spec, st_spec, kb_spec, kb_spec),
        out_shape=(jax.ShapeDtypeStruct((ns, seq_len, D_MODEL), F32),
                   jax.ShapeDtypeStruct((ns, seq_len, D_MODEL), BF16),
                   jax.ShapeDtypeStruct((ns, N_STATE), F32), jax.ShapeDtypeStruct((ns, N_STATE), F32),
                   kv_out, kv_out),
        scratch_shapes=_mixer_scratch(ns, rows) + [pltpu.VMEM((rows, SSM_DIM), F32)],
        compiler_params=_params(("arbitrary",), VMEM_LIMIT_HIGH),
        name="mixer_tm",
    )(u_tm, c_tm, x13, *weights, k_cache, v_cache)


def _block_diag_tiles(x):
    _, a, b = x.shape
    x4 = x.reshape(CH_TILES, GROUPS_PER_TILE, a, b)
    eye = jnp.eye(GROUPS_PER_TILE, dtype=x.dtype)
    out = x4[:, :, :, None, :] * eye[None, :, None, :, None]
    return out.reshape(CH_TILES, GROUPS_PER_TILE * a, GROUPS_PER_TILE * b)


def _row(x):
    return x.reshape(1, -1)


def _state_slabs(x):
    return jnp.broadcast_to(x.reshape(STATE_TILES, 1, LANES), (STATE_TILES, SUBLANES, LANES))


def kernel(x_prompt, x_sample, state_ssm_re, state_ssm_im, cache_conv, cache_mem_k, cache_mem_v, mem_prompt,
           g_mem, w_mem_k, w_mem_v, g_ffn1, w_ffn1_gate, w_ffn1_up, w_ffn1_down, g_mix, w_in,
           ssm_a_re, ssm_a_im, ssm_log_dt, ssm_b_re, ssm_b_im, ssm_c_re, ssm_c_im, ssm_d, w_ssm_glu,
           conv_w, conv_b, conv_ln_g, conv_ln_b, w_out, g_xattn, w_mem_q, w_mem_o,
           g_ffn2, w_ffn2_gate, w_ffn2_up, w_ffn2_down, g_final):
    depth = g_ffn1.shape[0]
    assert depth == 1
    l = 0
    bp, seq, _ = x_prompt.shape
    bs, dseq, _ = x_sample.shape
    bf = lambda a: a.astype(BF16)

    ab_re, ab_im, bbar_re, bbar_im = _s5_params(ssm_a_re[l], ssm_a_im[l], ssm_log_dt[l],
                                                jnp.swapaxes(ssm_b_re[l], 1, 2), jnp.swapaxes(ssm_b_im[l], 1, 2))
    mixer_w = dict(
        bblk=bf(jnp.concatenate([_block_diag_tiles(bbar_re), _block_diag_tiles(bbar_im)], axis=-1)),
        cre=bf(_block_diag_tiles(jnp.swapaxes(ssm_c_re[l], 1, 2))),
        cim=bf(-_block_diag_tiles(jnp.swapaxes(ssm_c_im[l], 1, 2))),
        ab_re=_state_slabs(ab_re), ab_im=_state_slabs(ab_im),
        d=_row(ssm_d[l]),
        conv_w=jnp.swapaxes(conv_w[l].reshape(CONV_WIDTH, CH_TILES, LANES), 0, 1),
        conv_b=_row(conv_b[l]), ln_g=_row(conv_ln_g[l]), ln_b=_row(conv_ln_b[l]),
        g_x=_row(g_xattn[l]),
    )
    g_mix_r, g_final_r = _row(g_mix[l]), _row(g_final)

    first = [w_ffn1_gate[l], w_ffn1_up[l], w_ffn1_down[l], w_in[l], w_ssm_glu[l]]
    mk, mv, mk_hm, mv_hm, w1g_b, w1u_b, w1d_b, w_in_b, w_glu_b = _mem_kv(
        mem_prompt.reshape(bp * N_MEM, D_MODEL), _row(g_mem[l]), bf(w_mem_k[l]), bf(w_mem_v[l]), first)
    ffn1 = (_row(g_ffn1[l]), w1g_b, w1u_b, w1d_b)
    mixer_w["w_glu"] = w_glu_b

    p_rows, s_rows = bp * seq, bs * dseq
    prompt_tt = 64
    later = [w_out[l], w_mem_q[l], w_mem_o[l], w_ffn2_gate[l], w_ffn2_up[l], w_ffn2_down[l]]
    x1p, u_tm, c_tm, p_buf, w_out_b, w_q_b, w_o_b, w2g_b, w2u_b, w2d_b = _front(
        x_prompt, *ffn1, g_mix_r, w_in_b, *[mixer_w[k] for k in _CONV_WEIGHTS], later, tt=prompt_tt)
    mixer_w.update(w_out=w_out_b, w_q=w_q_b)
    ffn2 = (_row(g_ffn2[l]), w2g_b, w2u_b, w2d_b)
    kv_tokens = lambda a: a.reshape(bs * N_MEM, MEM_HEADS, MEM_HEAD_DIM)
    x2p, qp, p_re, p_im, ks_b, vs_b = _mixer_tm(u_tm, c_tm, x1p, mixer_w, kv_tokens(cache_mem_k),
                                                kv_tokens(cache_mem_v), tt=prompt_tt)

    x1s, zs = _ffn_in(x_sample.reshape(s_rows, D_MODEL), *ffn1, g_mix_r, w_in_b)
    x2s, qs, s_re, s_im, s_buf = _mixer(
        zs.reshape(bs, dseq, -1), x1s.reshape(bs, dseq, D_MODEL),
        state_ssm_re[l].reshape(bs, N_STATE), state_ssm_im[l].reshape(bs, N_STATE),
        jnp.swapaxes(cache_conv[l], 0, 1), mixer_w, ns=32)

    yp, ys = _tail(x2p.reshape(p_rows, D_MODEL), qp.reshape(p_rows, D_MODEL), mk_hm, mv_hm,
                   x2s.reshape(s_rows, D_MODEL), qs.reshape(s_rows, D_MODEL),
                   ks_b, vs_b, w_o_b, *ffn2, g_final_r, tm=512)
    yp = yp.reshape(bp, seq, D_MODEL)
    ys = ys.reshape(bs, dseq, D_MODEL)

    st = lambda a, n: a.reshape(1, n, SSM_GROUPS, SSM_STATE)
    kv = lambda a: a.reshape(1, bp, N_MEM, MEM_HEADS, MEM_HEAD_DIM)
    buf = lambda a: jnp.swapaxes(a, 0, 1)[None]
    return (yp, ys, st(p_re, bp), st(p_im, bp), buf(p_buf), kv(mk), kv(mv),
            st(s_re, bs), st(s_im, bs), buf(s_buf))
```

```python
import functools

import jax
import jax.numpy as jnp
from jax import lax
from jax.experimental import pallas as pl
from jax.experimental.pallas import tpu as pltpu

F32 = jnp.float32
BF16 = jnp.bfloat16

EPS = 1e-6
D_MODEL = 1024
D_FF = 2816
SSM_DIM = 512
CONV_DIM = 512
SSM_GROUPS = 32
SSM_GROUP_CH = 16
SSM_STATE = 64
N_STATE = SSM_GROUPS * SSM_STATE
CONV_WIDTH = 31
N_MEM = 256
MEM_HEADS = 4
MEM_HEAD_DIM = 256

LANES = 128
SUBLANES = 8
STATE_TILES = N_STATE // LANES
CH_TILES = SSM_DIM // LANES
MODEL_TILES = D_MODEL // LANES
GROUPS_PER_TILE = LANES // SSM_GROUP_CH
STATES_PER_CH_TILE = GROUPS_PER_TILE * SSM_STATE
TILES_PER_CH = STATES_PER_CH_TILE // LANES
HIST = 32
FIRST_TAP = HIST - (CONV_WIDTH - 1)
SCAN_TILE_GROUP = 4
CONV_ROWS = 64
VMEM_LIMIT = 56 * 1024 * 1024


def _dot(a, b):
    return jnp.dot(a, b, preferred_element_type=F32)


def _rms(x, g):
    return x * lax.rsqrt(jnp.mean(x * x, axis=-1, keepdims=True) + EPS) * g


def _lane_tile(j):
    return slice(j * LANES, (j + 1) * LANES)


def _const_spec(shape):
    zeros = (0,) * len(shape)
    return pl.BlockSpec(shape, lambda *_: zeros, pipeline_mode=pl.Buffered(1))


def _params(sem):
    return pltpu.CompilerParams(dimension_semantics=sem, vmem_limit_bytes=VMEM_LIMIT)


def _s5_params_kernel(ar_ref, ai_ref, ldt_ref, bre_ref, bim_ref, abr_ref, abi_ref, bbr_ref, bbi_ref):
    lr = ar_ref[...]
    li = ai_ref[...]
    dt = jnp.exp(ldt_ref[...])
    mag = jnp.exp(lr * dt)
    ab_re = mag * jnp.cos(li * dt)
    ab_im = mag * jnp.sin(li * dt)
    den = lr * lr + li * li
    nr = ab_re - 1.0
    ni = ab_im
    coef_re = (nr * lr + ni * li) / den
    coef_im = (ni * lr - nr * li) / den
    abr_ref[...] = ab_re
    abi_ref[...] = ab_im
    b_re = bre_ref[...]
    b_im = bim_ref[...]
    bbr_ref[...] = coef_re * b_re - coef_im * b_im
    bbi_ref[...] = coef_re * b_im + coef_im * b_re


def _s5_params(a_re, a_im, log_dt, b_re, b_im):
    a_shape = jax.ShapeDtypeStruct((SSM_GROUPS, 1, SSM_STATE), F32)
    b_shape = jax.ShapeDtypeStruct(b_re.shape, F32)
    return pl.pallas_call(
        _s5_params_kernel,
        out_shape=(a_shape, a_shape, b_shape, b_shape),
        name="s5_params",
    )(a_re.reshape(a_shape.shape), a_im.reshape(a_shape.shape), log_dt.reshape(SSM_GROUPS, 1, 1), b_re, b_im)


BF16_SUBLANES = 16


def _cast_blocking(w, max_steps):
    n_rows = w.shape[0]
    n_blocks = max(n for n in range(1, max_steps + 1)
                   if n_rows % n == 0 and (n_rows // n) % BF16_SUBLANES == 0)
    return n_rows // n_blocks, n_blocks


def _cast_specs(to_cast, max_steps):
    specs = []
    for w in to_cast:
        blk_rows, n_blocks = _cast_blocking(w, max_steps)
        specs.append(pl.BlockSpec((blk_rows, w.shape[1]),
                                  (lambda n: (lambda j: (jnp.minimum(j, n - 1), 0)))(n_blocks)))
    return specs


def _cast_blocks(cast_in, cast_out):
    for src, dst in zip(cast_in, cast_out):
        dst[...] = src[...].astype(BF16)


def _mem_kv_kernel(*refs, n_cast):
    m_ref, g_ref, wk_ref, wv_ref = refs[:4]
    k_ref, v_ref, kh_ref, vh_ref = refs[4 + n_cast:8 + n_cast]
    _cast_blocks(refs[4:4 + n_cast], refs[8 + n_cast:])
    m = _rms(m_ref[...], g_ref[...]).astype(BF16)
    k = _dot(m, wk_ref[...])
    v = _dot(m, wv_ref[...])
    for h in range(MEM_HEADS):
        cols = slice(h * MEM_HEAD_DIM, (h + 1) * MEM_HEAD_DIM)
        k_ref[:, h, :] = k[:, cols]
        v_ref[:, h, :] = v[:, cols]
        kh_ref[0, h] = k[:, cols].T.astype(BF16)
        vh_ref[0, h] = v[:, cols].astype(BF16)


def _mem_kv(mem2d, g, wk, wv, to_cast):
    rows = mem2d.shape[0]
    nb = rows // N_MEM
    cast_specs = _cast_specs(to_cast, nb)
    row_spec = pl.BlockSpec((N_MEM, D_MODEL), lambda i: (i, 0))
    head_spec = pl.BlockSpec((N_MEM, MEM_HEADS, MEM_HEAD_DIM), lambda i: (i, 0, 0))
    hm_spec = pl.BlockSpec((1, MEM_HEADS, N_MEM, MEM_HEAD_DIM), lambda i: (i, 0, 0, 0))
    out = jax.ShapeDtypeStruct((rows, MEM_HEADS, MEM_HEAD_DIM), F32)
    out_hm = jax.ShapeDtypeStruct((nb, MEM_HEADS, N_MEM, MEM_HEAD_DIM), BF16)
    return pl.pallas_call(
        functools.partial(_mem_kv_kernel, n_cast=len(to_cast)),
        grid=(nb,),
        in_specs=[row_spec, _const_spec((1, D_MODEL)),
                  _const_spec((D_MODEL, D_MODEL)), _const_spec((D_MODEL, D_MODEL))] + cast_specs,
        out_specs=(head_spec, head_spec, hm_spec, hm_spec) + tuple(cast_specs),
        out_shape=(out, out, out_hm, out_hm) + tuple(jax.ShapeDtypeStruct(w.shape, BF16) for w in to_cast),
        compiler_params=_params(("arbitrary",)),
        name="mem_kv",
    )(mem2d, g, wk, wv, *to_cast)


def _zero_after(piece):
    bits = pltpu.bitcast(piece, jnp.uint32)
    return pltpu.bitcast((bits >> 16) >> 16, F32)


def _conv_rows(vext, ytm, cw_ref, j, base, *, ns, after):
    acc = _zero_after(after)
    for k in range(CONV_WIDTH):
        acc = acc + cw_ref[j, k:k + 1, :] * vext[j, pl.ds(base + (FIRST_TAP + k) * ns, CONV_ROWS), :]
    ytm[j, pl.ds(base, CONV_ROWS), :] = acc


def _ln_swish(ytm, cb_ref, lng_ref, lnb_ref):
    yc = jnp.concatenate([ytm[j] for j in range(CH_TILES)], axis=1) + cb_ref[...]
    mu = jnp.mean(yc, axis=-1, keepdims=True)
    var = jnp.mean(jnp.square(yc - mu), axis=-1, keepdims=True)
    yn = (yc - mu) * lax.rsqrt(var + EPS) * lng_ref[...] + lnb_ref[...]
    return yn * jax.nn.sigmoid(yn)


def _swiglu_half(x, g, wg_ref, wu_ref, wd_ref):
    hn = _rms(x, g).astype(BF16)
    gate = _dot(hn, wg_ref[...])
    up = _dot(hn, wu_ref[...])
    act = (gate * jax.nn.sigmoid(gate) * up).astype(BF16)
    return x + 0.5 * _dot(act, wd_ref[...])


def _ffn_in_kernel(x_ref, g1_ref, wg_ref, wu_ref, wd_ref, g2_ref, win_ref, x1_ref, z_ref):
    x1 = _swiglu_half(x_ref[...], g1_ref[...], wg_ref, wu_ref, wd_ref)
    x1_ref[...] = x1
    z_ref[...] = _dot(_rms(x1, g2_ref[...]).astype(BF16), win_ref[...])


def _ffn_in(x2d, g1, wg, wu, wd, g2, win, tm=512):
    rows = x2d.shape[0]
    zdim = win.shape[1]
    row_spec = pl.BlockSpec((tm, D_MODEL), lambda i: (i, 0))
    return pl.pallas_call(
        _ffn_in_kernel,
        grid=(rows // tm,),
        in_specs=[row_spec, _const_spec((1, D_MODEL)),
                  _const_spec((D_MODEL, D_FF)), _const_spec((D_MODEL, D_FF)), _const_spec((D_FF, D_MODEL)),
                  _const_spec((1, D_MODEL)), _const_spec((D_MODEL, zdim))],
        out_specs=(row_spec, pl.BlockSpec((tm, zdim), lambda i: (i, 0))),
        out_shape=(jax.ShapeDtypeStruct((rows, D_MODEL), F32), jax.ShapeDtypeStruct((rows, zdim), F32)),
        compiler_params=_params(("arbitrary",)),
        name="ffn_in",
    )(x2d, g1, wg, wu, wd, g2, win)


def _front_kernel(*refs, ns, tt, steps, n_cast):
    (x_ref, g1_ref, wg_ref, wu_ref, wd_ref, g2_ref, win_ref, cw_ref, cb_ref, lng_ref, lnb_ref) = refs[:11]
    cast_in = refs[11:11 + n_cast]
    x1_ref, u_ref, c_ref, hist_ref = refs[11 + n_cast:15 + n_cast]
    cast_out = refs[15 + n_cast:15 + 2 * n_cast]
    vext, ytm, utm = refs[15 + 2 * n_cast:]
    j = pl.program_id(0)
    rows = ns * tt
    hist_rows = HIST * ns
    gate_tile_step = (D_FF // LANES) // CH_TILES

    @pl.when(j == 0)
    def _():
        vext[...] = jnp.zeros_like(vext)

    x = x_ref[...].reshape(rows, D_MODEL)
    hn = _rms(x, g1_ref[...]).astype(BF16)
    gate = _dot(hn, wg_ref[...])
    for jt in range(CH_TILES):
        for b in range(rows // CONV_ROWS):
            _conv_rows(vext, ytm, cw_ref, jt, b * CONV_ROWS, ns=ns,
                       after=gate[b * CONV_ROWS:(b + 1) * CONV_ROWS, _lane_tile(jt * gate_tile_step)])
    c_ref[...] = _ln_swish(ytm, cb_ref, lng_ref, lnb_ref)
    up = _dot(hn, wu_ref[...])
    act = (gate * jax.nn.sigmoid(gate) * up).astype(BF16)
    x1 = x + 0.5 * _dot(act, wd_ref[...])
    x1_ref[...] = x1.reshape(ns, tt, D_MODEL)
    z = _dot(_rms(x1, g2_ref[...]).astype(BF16), win_ref[...])
    u = z[:, :SSM_DIM]
    v = z[:, SSM_DIM:SSM_DIM + CONV_DIM] * jax.nn.sigmoid(z[:, SSM_DIM + CONV_DIM:])

    vext[:, :hist_rows, :] = vext[:, rows:rows + hist_rows, :]
    for s in range(ns):
        seq = slice(s * tt, (s + 1) * tt)
        for jt in range(CH_TILES):
            vext[jt, pl.ds(hist_rows + s, tt, stride=ns), :] = v[seq, _lane_tile(jt)]
            utm[jt, pl.ds(s, tt, stride=ns), :] = u[seq, _lane_tile(jt)]
    u_ref[...] = jnp.concatenate([utm[jt] for jt in range(CH_TILES)], axis=1)

    _cast_blocks(cast_in, cast_out)

    @pl.when(j == steps)
    def _():
        for jt in range(CH_TILES):
            hist_ref[:, :, _lane_tile(jt)] = (
                vext[jt, FIRST_TAP * ns:hist_rows, :].reshape(CONV_WIDTH - 1, ns, LANES))


def _front(x3, g1, wg, wu, wd, g2, win, cw, cb, lng, lnb, to_cast, *, tt):
    ns, seq_len, _ = x3.shape
    steps = seq_len // tt
    rows = ns * tt
    assert rows % CONV_ROWS == 0 and ns == SUBLANES
    zdim = win.shape[1]
    cur = lambda j: jnp.minimum(j, steps - 1)
    prev = lambda j: jnp.clip(j - 1, 0, steps - 1)
    x_spec = pl.BlockSpec((ns, tt, D_MODEL), lambda j: (0, cur(j), 0))
    tm_spec = lambda m: pl.BlockSpec((rows, SSM_DIM), lambda j: (m(j), 0))
    cast_specs = _cast_specs(to_cast, steps + 1)
    return pl.pallas_call(
        functools.partial(_front_kernel, ns=ns, tt=tt, steps=steps, n_cast=len(to_cast)),
        grid=(steps + 1,),
        in_specs=[x_spec, _const_spec((1, D_MODEL)),
                  _const_spec((D_MODEL, D_FF)), _const_spec((D_MODEL, D_FF)), _const_spec((D_FF, D_MODEL)),
                  _const_spec((1, D_MODEL)), _const_spec((D_MODEL, zdim)),
                  _const_spec(cw.shape), _const_spec(cb.shape), _const_spec(lng.shape), _const_spec(lnb.shape)]
                 + cast_specs,
        out_specs=(x_spec, tm_spec(cur), tm_spec(prev),
                   pl.BlockSpec((CONV_WIDTH - 1, ns, CONV_DIM), lambda j: (0, 0, 0))) + tuple(cast_specs),
        out_shape=(jax.ShapeDtypeStruct((ns, seq_len, D_MODEL), F32),
                   jax.ShapeDtypeStruct((seq_len * ns, SSM_DIM), F32),
                   jax.ShapeDtypeStruct((seq_len * ns, CONV_DIM), F32),
                   jax.ShapeDtypeStruct((CONV_WIDTH - 1, ns, CONV_DIM), F32))
                  + tuple(jax.ShapeDtypeStruct(w.shape, BF16) for w in to_cast),
        scratch_shapes=[pltpu.VMEM((CH_TILES, rows + HIST * ns, LANES), F32),
                        pltpu.VMEM((CH_TILES, rows, LANES), F32),
                        pltpu.VMEM((CH_TILES, rows, LANES), F32)],
        compiler_params=_params(("arbitrary",)),
        name="front",
    )(x3, g1, wg, wu, wd, g2, win, cw, cb, lng, lnb, *to_cast)


def _softmax_rows(sc):
    e = jnp.exp(sc - jnp.max(sc, axis=-1, keepdims=True))
    return e / jnp.sum(e, axis=-1, keepdims=True)


def _attn_long_scores(q_ref, k_ref):
    scale = MEM_HEAD_DIM ** -0.5
    return [_dot(q_ref[:, h * MEM_HEAD_DIM:(h + 1) * MEM_HEAD_DIM], k_ref[0, h]) * scale
            for h in range(MEM_HEADS)]


def _attn_long_values(scores, v_ref):
    return jnp.concatenate([_dot(_softmax_rows(sc).astype(BF16), v_ref[0, h]).astype(BF16)
                            for h, sc in enumerate(scores)], axis=1)


def _attn_short_scores(q_ref, k_ref, *, nseq, lq):
    scale = MEM_HEAD_DIM ** -0.5
    kv_rows = N_MEM * MEM_HEADS
    k2d = k_ref.reshape(nseq * kv_rows, MEM_HEAD_DIM)
    shape = (MEM_HEADS * lq, kv_rows)
    same_head = (lax.broadcasted_iota(jnp.int32, shape, 0) // lq
                 == lax.broadcasted_iota(jnp.int32, shape, 1) % MEM_HEADS)
    scores = []
    for s in range(nseq):
        rows = slice(s * lq, (s + 1) * lq)
        q = jnp.concatenate([q_ref[rows, h * MEM_HEAD_DIM:(h + 1) * MEM_HEAD_DIM] for h in range(MEM_HEADS)],
                            axis=0)
        k = k2d[s * kv_rows:(s + 1) * kv_rows, :].astype(BF16)
        sc = lax.dot_general(q, k, (((1,), (1,)), ((), ())), preferred_element_type=F32) * scale
        scores.append(jnp.where(same_head, sc, -jnp.inf))
    return scores


def _attn_short_values(scores, v_ref, *, lq):
    kv_rows = N_MEM * MEM_HEADS
    v2d = v_ref.reshape(len(scores) * kv_rows, MEM_HEAD_DIM)
    out = []
    for s, sc in enumerate(scores):
        v = v2d[s * kv_rows:(s + 1) * kv_rows, :].astype(BF16)
        o = _dot(_softmax_rows(sc).astype(BF16), v)
        out.append(jnp.concatenate([o[h * lq:(h + 1) * lq, :] for h in range(MEM_HEADS)], axis=1))
    return jnp.concatenate(out, axis=0)


def _tail_kernel(x2a_ref, qa_ref, ka_ref, va_ref, x2b_ref, qb_ref, kb_ref, vb_ref,
                 wo_ref, g_ref, wg_ref, wu_ref, wd_ref, gf_ref, ya_ref, yb_ref, oa_scr, ob_scr,
                 *, a_steps, tm, nseq, lq):
    j = pl.program_id(0)
    in_b = j > a_steps

    @pl.when(j == 0)
    def _():
        oa_scr[...] = jnp.zeros_like(oa_scr)
        ob_scr[...] = jnp.zeros_like(ob_scr)

    b_tile = pl.ds(pl.multiple_of(jnp.maximum(j - a_steps - 1, 0) * tm, tm), tm)
    o = jnp.where(in_b, ob_scr[b_tile, :], oa_scr[...])
    x2 = jnp.where(in_b, x2b_ref[...], x2a_ref[...])

    half = tm // 2
    x3s = [x2[h * half:(h + 1) * half] + _dot(o[h * half:(h + 1) * half], wo_ref[...]) for h in range(2)]
    hns = [_rms(x3h, g_ref[...]).astype(BF16) for x3h in x3s]
    gates = [_dot(hnh, wg_ref[...]) for hnh in hns]
    ups = [_dot(hnh, wu_ref[...]) for hnh in hns]
    sc_a = _attn_long_scores(qa_ref, ka_ref)
    sc_b = _attn_short_scores(qb_ref, kb_ref, nseq=nseq, lq=lq)
    acts = [(gt * jax.nn.sigmoid(gt) * upt).astype(BF16) for gt, upt in zip(gates, ups)]
    x4 = jnp.concatenate([x3h + 0.5 * _dot(acth, wd_ref[...]) for x3h, acth in zip(x3s, acts)], axis=0)

    b_rows = nseq * lq
    blk = jnp.minimum(j, a_steps - 1)
    ob_scr[pl.ds(pl.multiple_of(blk * b_rows, b_rows), b_rows), :] = (
        _attn_short_values(sc_b, vb_ref, lq=lq).astype(BF16))
    oa_scr[...] = _attn_long_values(sc_a, va_ref)

    y = _rms(x4, gf_ref[...])

    @pl.when(jnp.logical_and(j >= 1, j <= a_steps))
    def _():
        ya_ref[...] = y

    @pl.when(in_b)
    def _():
        yb_ref[...] = y


def _tail(x2a, qa, ka_hm, va_hm, x2b, qb, kb3d, vb3d, wo, g, wg, wu, wd, gf, *, tm):
    a_rows, b_rows = x2a.shape[0], x2b.shape[0]
    a_steps, b_steps = a_rows // tm, b_rows // tm
    n_b = kb3d.shape[0] // N_MEM
    lq = b_rows // n_b
    assert n_b % a_steps == 0
    nseq = n_b // a_steps
    tiles_per_seq = a_rows // ka_hm.shape[0] // tm
    attn_blk = lambda j: jnp.minimum(j, a_steps - 1)
    a_fin = lambda j: (jnp.clip(j - 1, 0, a_steps - 1), 0)
    b_fin = lambda j: (jnp.maximum(j - a_steps - 1, 0), 0)
    row = lambda m: pl.BlockSpec((tm, D_MODEL), m)
    ka_spec = pl.BlockSpec((1, MEM_HEADS, N_MEM, MEM_HEAD_DIM),
                           lambda j: (attn_blk(j) // tiles_per_seq, 0, 0, 0))
    kb_spec = pl.BlockSpec((nseq * N_MEM, MEM_HEADS, MEM_HEAD_DIM), lambda j: (attn_blk(j), 0, 0))
    return pl.pallas_call(
        functools.partial(_tail_kernel, a_steps=a_steps, tm=tm, nseq=nseq, lq=lq),
        grid=(a_steps + 1 + b_steps,),
        in_specs=[row(a_fin), row(lambda j: (attn_blk(j), 0)), ka_spec, ka_spec,
                  row(b_fin), pl.BlockSpec((nseq * lq, D_MODEL), lambda j: (attn_blk(j), 0)), kb_spec, kb_spec,
                  _const_spec((D_MODEL, D_MODEL)), _const_spec((1, D_MODEL)),
                  _const_spec((D_MODEL, D_FF)), _const_spec((D_MODEL, D_FF)), _const_spec((D_FF, D_MODEL)),
                  _const_spec((1, D_MODEL))],
        out_specs=(row(a_fin), row(b_fin)),
        out_shape=(jax.ShapeDtypeStruct((a_rows, D_MODEL), F32), jax.ShapeDtypeStruct((b_rows, D_MODEL), F32)),
        scratch_shapes=[pltpu.VMEM((tm, D_MODEL), BF16), pltpu.VMEM((b_rows, D_MODEL), BF16)],
        compiler_params=_params(("arbitrary",)),
        name="tail",
    )(x2a, qa, ka_hm, va_hm, x2b, qb, kb3d, vb3d, wo, g, wg, wu, wd, gf)


def _s5_input_and_conv(utm, vext, ytm, xr, xi, bblk_ref, cw_ref, *, ns, rows):
    def body(kb, carry):
        x = _dot(utm[kb].astype(BF16), bblk_ref[kb])
        for i in range(TILES_PER_CH):
            xr[kb * TILES_PER_CH + i] = x[:, _lane_tile(i)]
            xi[kb * TILES_PER_CH + i] = x[:, _lane_tile(TILES_PER_CH + i)]
        for b in range(rows // CONV_ROWS):
            _conv_rows(vext, ytm, cw_ref, kb, b * CONV_ROWS, ns=ns,
                       after=x[b * CONV_ROWS:(b + 1) * CONV_ROWS, :LANES])
        return carry

    lax.fori_loop(0, CH_TILES, body, 0)


SCAN_UNROLL = 8


def _s5_scan(xr, xi, hcr, hci, ar_ref, ai_ref, *, ns, tt, after=None):
    def step(row, carry, tiles, a_r, a_i):
        new_r, new_i = [], []
        for idx, j in enumerate(tiles):
            h_r, h_i = carry[idx], carry[SCAN_TILE_GROUP + idx]
            n_r = a_r[idx] * h_r - a_i[idx] * h_i + xr[j, row, :]
            n_i = a_r[idx] * h_i + a_i[idx] * h_r + xi[j, row, :]
            xr[j, row, :] = n_r
            xi[j, row, :] = n_i
            new_r.append(n_r)
            new_i.append(n_i)
        return tuple(new_r) + tuple(new_i)

    for i in range(ns // SUBLANES):
        srow = slice(i * SUBLANES, (i + 1) * SUBLANES)
        for jg in range(STATE_TILES // SCAN_TILE_GROUP):
            tiles = tuple(range(jg * SCAN_TILE_GROUP, (jg + 1) * SCAN_TILE_GROUP))
            a_r = [ar_ref[j] for j in tiles]
            a_i = [ai_ref[j] for j in tiles]
            carry = tuple(hcr[j, srow, :] for j in tiles) + tuple(hci[j, srow, :] for j in tiles)
            if after is None:
                def body(t, c, tiles=tiles, a_r=a_r, a_i=a_i, i=i):
                    row = pl.ds(pl.multiple_of(t * ns + i * SUBLANES, SUBLANES), SUBLANES)
                    return step(row, c, tiles, a_r, a_i)

                carry = lax.fori_loop(0, tt, body, carry, unroll=SCAN_UNROLL)
            else:
                for t in range(tt):
                    if t % SCAN_UNROLL == 0:
                        zero = _zero_after(after(t // SCAN_UNROLL, jg))
                        carry = tuple(c + zero for c in carry)
                    carry = step(pl.ds(t * ns + i * SUBLANES, SUBLANES), carry, tiles, a_r, a_i)
            for idx, j in enumerate(tiles):
                hcr[j, srow, :] = carry[idx]
                hci[j, srow, :] = carry[SCAN_TILE_GROUP + idx]


def _s5_readout(u_all, xr, xi, cre_ref, cim_ref, d_ref, wglu_ref):
    ys = []
    for ob in range(CH_TILES):
        h_re = jnp.concatenate([xr[ob * TILES_PER_CH + i] for i in range(TILES_PER_CH)], axis=1).astype(BF16)
        h_im = jnp.concatenate([xi[ob * TILES_PER_CH + i] for i in range(TILES_PER_CH)], axis=1).astype(BF16)
        ys.append(_dot(h_re, cre_ref[ob]) + _dot(h_im, cim_ref[ob]))
    y = jnp.concatenate(ys, axis=1) + d_ref[...] * u_all
    g = jax.nn.gelu(y)
    return g * jax.nn.sigmoid(_dot(g.astype(BF16), wglu_ref[...]))


def _project_delta(s_out, c_out, wout_ref):
    return _dot(jnp.concatenate([s_out, c_out], axis=1).astype(BF16), wout_ref[...])


def _finish_projection(delta, x1_ref, gx_ref, wq_ref, x2_ref, q_ref, dtm, xn, *, ns, tt):
    for j in range(MODEL_TILES):
        dtm[j] = delta[:, _lane_tile(j)]
    for s in range(ns):
        d_s = jnp.concatenate([dtm[j, pl.ds(s, tt, stride=ns), :] for j in range(MODEL_TILES)], axis=1)
        x2_s = x1_ref[s] + d_s
        x2_ref[s] = x2_s
        xn[s * tt:(s + 1) * tt, :] = _rms(x2_s, gx_ref[...])
    qv = _dot(xn[...].astype(BF16), wq_ref[...]).astype(BF16)
    for s in range(ns):
        q_ref[s] = qv[s * tt:(s + 1) * tt, :]


def _state_out(hre_out, him_out, hcr, hci):
    hre_out[...] = jnp.concatenate([hcr[j] for j in range(STATE_TILES)], axis=1)
    him_out[...] = jnp.concatenate([hci[j] for j in range(STATE_TILES)], axis=1)


_S5_WEIGHTS = ("bblk", "cre", "cim", "ab_re", "ab_im", "d", "w_glu")
_CONV_WEIGHTS = ("conv_w", "conv_b", "ln_g", "ln_b")
_OUT_WEIGHTS = ("w_out", "g_x", "w_q")


def _mixer_scratch(ns, rows):
    return [pltpu.VMEM((STATE_TILES, rows, LANES), F32),
            pltpu.VMEM((STATE_TILES, rows, LANES), F32),
            pltpu.VMEM((STATE_TILES, ns, LANES), F32),
            pltpu.VMEM((STATE_TILES, ns, LANES), F32),
            pltpu.VMEM((MODEL_TILES, rows, LANES), F32),
            pltpu.VMEM((rows, D_MODEL), F32)]


def _mixer_kernel(u_ref, val_ref, gate_ref, x1_ref, h0r_ref, h0i_ref, cache_ref,
                  bblk_ref, cre_ref, cim_ref, ar_ref, ai_ref, d_ref, wglu_ref,
                  cw_ref, cb_ref, lng_ref, lnb_ref, wout_ref, gx_ref, wq_ref,
                  x2_ref, q_ref, hre_out, him_out, hist_out,
                  xr, xi, hcr, hci, dtm, xn, utm, vext, ytm, *, ns, tt):
    rows = ns * tt
    hist_rows = HIST * ns

    for j in range(STATE_TILES):
        hcr[j] = h0r_ref[:, _lane_tile(j)]
        hci[j] = h0i_ref[:, _lane_tile(j)]
    for j in range(CH_TILES):
        vext[j, FIRST_TAP * ns:hist_rows, :] = (
            cache_ref[:, :, _lane_tile(j)].reshape((CONV_WIDTH - 1) * ns, LANES))

    for s in range(ns):
        u_s = u_ref[s]
        v_s = val_ref[s] * jax.nn.sigmoid(gate_ref[s])
        for j in range(CH_TILES):
            utm[j, pl.ds(s, tt, stride=ns), :] = u_s[:, _lane_tile(j)]
            vext[j, pl.ds(hist_rows + s, tt, stride=ns), :] = v_s[:, _lane_tile(j)]

    _s5_input_and_conv(utm, vext, ytm, xr, xi, bblk_ref, cw_ref, ns=ns, rows=rows)
    _s5_scan(xr, xi, hcr, hci, ar_ref, ai_ref, ns=ns, tt=tt)
    u_all = jnp.concatenate([utm[j] for j in range(CH_TILES)], axis=1)
    s_out = _s5_readout(u_all, xr, xi, cre_ref, cim_ref, d_ref, wglu_ref)
    c_out = _ln_swish(ytm, cb_ref, lng_ref, lnb_ref)
    delta = _project_delta(s_out, c_out, wout_ref)
    _finish_projection(delta, x1_ref, gx_ref, wq_ref, x2_ref, q_ref, dtm, xn, ns=ns, tt=tt)
    _state_out(hre_out, him_out, hcr, hci)
    for j in range(CH_TILES):
        hist_out[:, :, _lane_tile(j)] = (
            vext[j, (tt + FIRST_TAP) * ns:(tt + HIST) * ns, :].reshape(CONV_WIDTH - 1, ns, LANES))


def _mixer(z3, x13, h0_re, h0_im, hist, w, *, ns):
    nseq, tt, _ = z3.shape
    rows = ns * tt
    assert nseq % ns == 0 and rows % CONV_ROWS == 0 and ns % SUBLANES == 0
    st_spec = pl.BlockSpec((ns, N_STATE), lambda i: (i, 0))
    hist_spec = pl.BlockSpec((CONV_WIDTH - 1, ns, CONV_DIM), lambda i: (0, i, 0))
    row_spec = pl.BlockSpec((ns, tt, D_MODEL), lambda i: (i, 0, 0))
    z_specs = [pl.BlockSpec((ns, tt, SSM_DIM), (lambda c: (lambda i: (i, 0, c)))(c)) for c in range(3)]
    weights = [w[k] for k in _S5_WEIGHTS + _CONV_WEIGHTS + _OUT_WEIGHTS]
    return pl.pallas_call(
        functools.partial(_mixer_kernel, ns=ns, tt=tt),
        grid=(nseq // ns,),
        in_specs=z_specs + [row_spec, st_spec, st_spec, hist_spec] + [_const_spec(a.shape) for a in weights],
        out_specs=(row_spec, row_spec, st_spec, st_spec, hist_spec),
        out_shape=(jax.ShapeDtypeStruct((nseq, tt, D_MODEL), F32),
                   jax.ShapeDtypeStruct((nseq, tt, D_MODEL), BF16),
                   jax.ShapeDtypeStruct((nseq, N_STATE), F32), jax.ShapeDtypeStruct((nseq, N_STATE), F32),
                   jax.ShapeDtypeStruct((CONV_WIDTH - 1, nseq, CONV_DIM), F32)),
        scratch_shapes=_mixer_scratch(ns, rows) + [
            pltpu.VMEM((CH_TILES, rows, LANES), F32),
            pltpu.VMEM((CH_TILES, rows + HIST * ns, LANES), F32),
            pltpu.VMEM((CH_TILES, rows, LANES), F32)],
        compiler_params=_params(("arbitrary",)),
        name="mixer",
    )(z3, z3, z3, x13, h0_re, h0_im, hist, *weights)


def _mixer_tm_kernel(u_ref, c_ref, x1_ref, bblk_ref, cre_ref, cim_ref, ar_ref, ai_ref, d_ref, wglu_ref,
                     wout_ref, gx_ref, wq_ref, x2_ref, q_ref, hre_out, him_out,
                     xr, xi, hcr, hci, dtm, xn, s_prev, *, ns, tt, steps):
    i = pl.program_id(0)

    @pl.when(i == 0)
    def _():
        hcr[...] = jnp.zeros_like(hcr)
        hci[...] = jnp.zeros_like(hci)
        s_prev[...] = jnp.zeros_like(s_prev)

    u = u_ref[...]
    for kb in range(CH_TILES):
        x = _dot(u[:, _lane_tile(kb)].astype(BF16), bblk_ref[kb])
        for t in range(TILES_PER_CH):
            xr[kb * TILES_PER_CH + t] = x[:, _lane_tile(t)]
            xi[kb * TILES_PER_CH + t] = x[:, _lane_tile(TILES_PER_CH + t)]

    delta = _project_delta(s_prev[...], c_ref[...], wout_ref)
    group_rows = SCAN_UNROLL * ns
    tiles_per_group = MODEL_TILES // (STATE_TILES // SCAN_TILE_GROUP)

    def after(group, tile_group):
        r0 = group * group_rows
        return delta[r0:r0 + SUBLANES, _lane_tile(tile_group * tiles_per_group)]

    _s5_scan(xr, xi, hcr, hci, ar_ref, ai_ref, ns=ns, tt=tt, after=after)
    _finish_projection(delta, x1_ref, gx_ref, wq_ref, x2_ref, q_ref, dtm, xn, ns=ns, tt=tt)
    s_prev[...] = _s5_readout(u, xr, xi, cre_ref, cim_ref, d_ref, wglu_ref)

    @pl.when(i == steps - 1)
    def _():
        _state_out(hre_out, him_out, hcr, hci)


def _mixer_tm(u_tm, c_tm, x13, w, *, tt):
    ns, seq_len, _ = x13.shape
    steps = seq_len // tt
    rows = ns * tt
    weights = [w[k] for k in _S5_WEIGHTS + _OUT_WEIGHTS]
    cur = lambda i: jnp.minimum(i, steps - 1)
    prev = lambda i: jnp.clip(i - 1, 0, steps - 1)
    tm_spec = lambda m: pl.BlockSpec((rows, SSM_DIM), lambda i: (m(i), 0))
    row_spec = pl.BlockSpec((ns, tt, D_MODEL), lambda i: (0, prev(i), 0))
    st_spec = pl.BlockSpec((ns, N_STATE), lambda i: (0, 0))
    return pl.pallas_call(
        functools.partial(_mixer_tm_kernel, ns=ns, tt=tt, steps=steps),
        grid=(steps + 1,),
        in_specs=[tm_spec(cur), tm_spec(prev), row_spec] + [_const_spec(a.shape) for a in weights],
        out_specs=(row_spec, row_spec, st_spec, st_spec),
        out_shape=(jax.ShapeDtypeStruct((ns, seq_len, D_MODEL), F32),
                   jax.ShapeDtypeStruct((ns, seq_len, D_MODEL), BF16),
                   jax.ShapeDtypeStruct((ns, N_STATE), F32), jax.ShapeDtypeStruct((ns, N_STATE), F32)),
        scratch_shapes=_mixer_scratch(ns, rows) + [pltpu.VMEM((rows, SSM_DIM), F32)],
        compiler_params=_params(("arbitrary",)),
        name="mixer_tm",
    )(u_tm, c_tm, x13, *weights)


def _block_diag_tiles(x):
    _, a, b = x.shape
    x4 = x.reshape(CH_TILES, GROUPS_PER_TILE, a, b)
    eye = jnp.eye(GROUPS_PER_TILE, dtype=x.dtype)
    out = x4[:, :, :, None, :] * eye[None, :, None, :, None]
    return out.reshape(CH_TILES, GROUPS_PER_TILE * a, GROUPS_PER_TILE * b)


def _row(x):
    return x.reshape(1, -1)


def _state_slabs(x):
    return jnp.broadcast_to(x.reshape(STATE_TILES, 1, LANES), (STATE_TILES, SUBLANES, LANES))


def kernel(x_prompt, x_sample, state_ssm_re, state_ssm_im, cache_conv, cache_mem_k, cache_mem_v, mem_prompt,
           g_mem, w_mem_k, w_mem_v, g_ffn1, w_ffn1_gate, w_ffn1_up, w_ffn1_down, g_mix, w_in,
           ssm_a_re, ssm_a_im, ssm_log_dt, ssm_b_re, ssm_b_im, ssm_c_re, ssm_c_im, ssm_d, w_ssm_glu,
           conv_w, conv_b, conv_ln_g, conv_ln_b, w_out, g_xattn, w_mem_q, w_mem_o,
           g_ffn2, w_ffn2_gate, w_ffn2_up, w_ffn2_down, g_final):
    depth = g_ffn1.shape[0]
    assert depth == 1
    l = 0
    bp, seq, _ = x_prompt.shape
    bs, dseq, _ = x_sample.shape
    bf = lambda a: a.astype(BF16)

    ab_re, ab_im, bbar_re, bbar_im = _s5_params(ssm_a_re[l], ssm_a_im[l], ssm_log_dt[l],
                                                jnp.swapaxes(ssm_b_re[l], 1, 2), jnp.swapaxes(ssm_b_im[l], 1, 2))
    mixer_w = dict(
        bblk=bf(jnp.concatenate([_block_diag_tiles(bbar_re), _block_diag_tiles(bbar_im)], axis=-1)),
        cre=bf(_block_diag_tiles(jnp.swapaxes(ssm_c_re[l], 1, 2))),
        cim=bf(-_block_diag_tiles(jnp.swapaxes(ssm_c_im[l], 1, 2))),
        ab_re=_state_slabs(ab_re), ab_im=_state_slabs(ab_im),
        d=_row(ssm_d[l]),
        conv_w=jnp.swapaxes(conv_w[l].reshape(CONV_WIDTH, CH_TILES, LANES), 0, 1),
        conv_b=_row(conv_b[l]), ln_g=_row(conv_ln_g[l]), ln_b=_row(conv_ln_b[l]),
        g_x=_row(g_xattn[l]),
    )
    g_mix_r, g_final_r = _row(g_mix[l]), _row(g_final)

    first = [w_ffn1_gate[l], w_ffn1_up[l], w_ffn1_down[l], w_in[l], w_ssm_glu[l]]
    mk, mv, mk_hm, mv_hm, w1g_b, w1u_b, w1d_b, w_in_b, w_glu_b = _mem_kv(
        mem_prompt.reshape(bp * N_MEM, D_MODEL), _row(g_mem[l]), bf(w_mem_k[l]), bf(w_mem_v[l]), first)
    ffn1 = (_row(g_ffn1[l]), w1g_b, w1u_b, w1d_b)
    mixer_w["w_glu"] = w_glu_b

    p_rows, s_rows = bp * seq, bs * dseq
    prompt_tt = 64
    later = [w_out[l], w_mem_q[l], w_mem_o[l], w_ffn2_gate[l], w_ffn2_up[l], w_ffn2_down[l]]
    x1p, u_tm, c_tm, p_buf, w_out_b, w_q_b, w_o_b, w2g_b, w2u_b, w2d_b = _front(
        x_prompt, *ffn1, g_mix_r, w_in_b, *[mixer_w[k] for k in _CONV_WEIGHTS], later, tt=prompt_tt)
    mixer_w.update(w_out=w_out_b, w_q=w_q_b)
    ffn2 = (_row(g_ffn2[l]), w2g_b, w2u_b, w2d_b)
    x2p, qp, p_re, p_im = _mixer_tm(u_tm, c_tm, x1p, mixer_w, tt=prompt_tt)

    x1s, zs = _ffn_in(x_sample.reshape(s_rows, D_MODEL), *ffn1, g_mix_r, w_in_b)
    x2s, qs, s_re, s_im, s_buf = _mixer(
        zs.reshape(bs, dseq, -1), x1s.reshape(bs, dseq, D_MODEL),
        state_ssm_re[l].reshape(bs, N_STATE), state_ssm_im[l].reshape(bs, N_STATE),
        jnp.swapaxes(cache_conv[l], 0, 1), mixer_w, ns=32)

    kv_rows = lambda a: a.reshape(bs * N_MEM, MEM_HEADS, MEM_HEAD_DIM)
    yp, ys = _tail(x2p.reshape(p_rows, D_MODEL), qp.reshape(p_rows, D_MODEL), mk_hm, mv_hm,
                   x2s.reshape(s_rows, D_MODEL), qs.reshape(s_rows, D_MODEL),
                   kv_rows(cache_mem_k), kv_rows(cache_mem_v), w_o_b, *ffn2, g_final_r, tm=256)
    yp = yp.reshape(bp, seq, D_MODEL)
    ys = ys.reshape(bs, dseq, D_MODEL)

    st = lambda a, n: a.reshape(1, n, SSM_GROUPS, SSM_STATE)
    kv = lambda a: a.reshape(1, bp, N_MEM, MEM_HEADS, MEM_HEAD_DIM)
    buf = lambda a: jnp.swapaxes(a, 0, 1)[None]
    return (yp, ys, st(p_re, bp), st(p_im, bp), buf(p_buf), kv(mk), kv(mv),
            st(s_re, bs), st(s_im, bs), buf(s_buf))
```

```python
import functools

import jax
import jax.numpy as jnp
from jax import lax
from jax.experimental import pallas as pl
from jax.experimental.pallas import tpu as pltpu

F32 = jnp.float32
BF16 = jnp.bfloat16

EPS = 1e-6
D_MODEL = 1024
D_FF = 2816
SSM_DIM = 512
CONV_DIM = 512
SSM_GROUPS = 32
SSM_GROUP_CH = 16
SSM_STATE = 64
N_STATE = SSM_GROUPS * SSM_STATE
CONV_WIDTH = 31
N_MEM = 256
MEM_HEADS = 4
MEM_HEAD_DIM = 256

LANES = 128
SUBLANES = 8
STATE_TILES = N_STATE // LANES
CH_TILES = SSM_DIM // LANES
MODEL_TILES = D_MODEL // LANES
GROUPS_PER_TILE = LANES // SSM_GROUP_CH
STATES_PER_CH_TILE = GROUPS_PER_TILE * SSM_STATE
TILES_PER_CH = STATES_PER_CH_TILE // LANES
HIST = 32
FIRST_TAP = HIST - (CONV_WIDTH - 1)
SCAN_TILE_GROUP = 4
CONV_ROWS = 64
VMEM_LIMIT = 56 * 1024 * 1024


def _dot(a, b):
    return jnp.dot(a, b, preferred_element_type=F32)


def _rms(x, g):
    return x * lax.rsqrt(jnp.mean(x * x, axis=-1, keepdims=True) + EPS) * g


def _lane_tile(j):
    return slice(j * LANES, (j + 1) * LANES)


def _const_spec(shape):
    zeros = (0,) * len(shape)
    return pl.BlockSpec(shape, lambda *_: zeros, pipeline_mode=pl.Buffered(1))


def _params(sem):
    return pltpu.CompilerParams(dimension_semantics=sem, vmem_limit_bytes=VMEM_LIMIT)


def _spread_exact(x, sel):
    hi = x.astype(BF16)
    r1 = x - hi.astype(F32)
    mid = r1.astype(BF16)
    lo = (r1 - mid.astype(F32)).astype(BF16)
    return _dot(hi, sel) + _dot(mid, sel) + _dot(lo, sel)


def _s5_params_kernel(ar_ref, ai_ref, ldt_ref, bre_ref, bim_ref, cre_ref, cim_ref,
                      abr_ref, abi_ref, bblk_ref, cro_ref, cio_ref):
    iota = lambda shape, d: lax.broadcasted_iota(jnp.int32, shape, d)
    sc_shape, cs_shape = (STATES_PER_CH_TILE, LANES), (LANES, STATES_PER_CH_TILE)
    own_sc = (iota(sc_shape, 0) // SSM_STATE) == (iota(sc_shape, 1) // SSM_GROUP_CH)
    own_cs = (iota(cs_shape, 0) // SSM_GROUP_CH) == (iota(cs_shape, 1) // SSM_STATE)
    spread_c = (iota((SSM_GROUP_CH, LANES), 1) % SSM_GROUP_CH == iota((SSM_GROUP_CH, LANES), 0)).astype(BF16)
    spread_n = (iota((SSM_STATE, STATES_PER_CH_TILE), 1) % SSM_STATE
                == iota((SSM_STATE, STATES_PER_CH_TILE), 0)).astype(BF16)
    for t in range(CH_TILES):
        lr = ar_ref[t]
        li = ai_ref[t]
        dt = jnp.exp(ldt_ref[t])
        mag = jnp.exp(lr * dt)
        ab_re = mag * jnp.cos(li * dt)
        ab_im = mag * jnp.sin(li * dt)
        den = lr * lr + li * li
        nr = ab_re - 1.0
        ni = ab_im
        coef_re = (nr * lr + ni * li) / den
        coef_im = (ni * lr - nr * li) / den
        for k in range(TILES_PER_CH):
            abr_ref[t * TILES_PER_CH + k] = jnp.broadcast_to(ab_re[:, _lane_tile(k)], (SUBLANES, LANES))
            abi_ref[t * TILES_PER_CH + k] = jnp.broadcast_to(ab_im[:, _lane_tile(k)], (SUBLANES, LANES))
        b_re = jnp.where(own_sc, _spread_exact(bre_ref[t], spread_c), 0.0).T
        b_im = jnp.where(own_sc, _spread_exact(bim_ref[t], spread_c), 0.0).T
        bblk_ref[t, :, :STATES_PER_CH_TILE] = (coef_re * b_re - coef_im * b_im).astype(BF16)
        bblk_ref[t, :, STATES_PER_CH_TILE:] = (coef_re * b_im + coef_im * b_re).astype(BF16)
        c_re = jnp.where(own_cs, _dot(cre_ref[t].astype(BF16), spread_n), 0.0).T
        c_im = jnp.where(own_cs, _dot(cim_ref[t].astype(BF16), spread_n), 0.0).T
        cro_ref[t] = c_re.astype(BF16)
        cio_ref[t] = (-c_im).astype(BF16)


def _s5_params(a_re, a_im, log_dt, b_re, b_im, c_re, c_im):
    states = lambda a: a.reshape(CH_TILES, 1, STATES_PER_CH_TILE)
    a_shape = jax.ShapeDtypeStruct((STATE_TILES, SUBLANES, LANES), F32)
    c_shape = jax.ShapeDtypeStruct((CH_TILES, STATES_PER_CH_TILE, LANES), BF16)
    b_tiles = lambda a: a.reshape(CH_TILES, STATES_PER_CH_TILE, SSM_GROUP_CH)
    c_tiles = lambda a: a.reshape(CH_TILES, LANES, SSM_STATE)
    return pl.pallas_call(
        _s5_params_kernel,
        out_shape=(a_shape, a_shape, jax.ShapeDtypeStruct((CH_TILES, LANES, 2 * STATES_PER_CH_TILE), BF16),
                   c_shape, c_shape),
        name="s5_params",
    )(states(a_re), states(a_im), states(jnp.repeat(log_dt, SSM_STATE)),
      b_tiles(b_re), b_tiles(b_im), c_tiles(c_re), c_tiles(c_im))


BF16_SUBLANES = 16


def _cast_blocking(w, max_steps):
    n_rows = w.shape[0]
    n_blocks = max(n for n in range(1, max_steps + 1)
                   if n_rows % n == 0 and (n_rows // n) % BF16_SUBLANES == 0)
    return n_rows // n_blocks, n_blocks


def _cast_specs(to_cast, max_steps):
    specs = []
    for w in to_cast:
        blk_rows, n_blocks = _cast_blocking(w, max_steps)
        specs.append(pl.BlockSpec((blk_rows, w.shape[1]),
                                  (lambda n: (lambda j: (jnp.minimum(j, n - 1), 0)))(n_blocks)))
    return specs


def _cast_blocks(cast_in, cast_out):
    for src, dst in zip(cast_in, cast_out):
        dst[...] = src[...].astype(BF16)


def _mem_kv_kernel(*refs, n_cast):
    m_ref, g_ref, wk_ref, wv_ref = refs[:4]
    k_ref, v_ref, kh_ref, vh_ref = refs[4 + n_cast:8 + n_cast]
    _cast_blocks(refs[4:4 + n_cast], refs[8 + n_cast:])
    m = _rms(m_ref[...], g_ref[...]).astype(BF16)
    k = _dot(m, wk_ref[...])
    v = _dot(m, wv_ref[...])
    for h in range(MEM_HEADS):
        cols = slice(h * MEM_HEAD_DIM, (h + 1) * MEM_HEAD_DIM)
        k_ref[:, h, :] = k[:, cols]
        v_ref[:, h, :] = v[:, cols]
        kh_ref[0, h] = k[:, cols].T.astype(BF16)
        vh_ref[0, h] = v[:, cols].astype(BF16)


def _mem_kv(mem2d, g, wk, wv, to_cast):
    rows = mem2d.shape[0]
    nb = rows // N_MEM
    cast_specs = _cast_specs(to_cast, nb)
    row_spec = pl.BlockSpec((N_MEM, D_MODEL), lambda i: (i, 0))
    head_spec = pl.BlockSpec((N_MEM, MEM_HEADS, MEM_HEAD_DIM), lambda i: (i, 0, 0))
    hm_spec = pl.BlockSpec((1, MEM_HEADS, N_MEM, MEM_HEAD_DIM), lambda i: (i, 0, 0, 0))
    out = jax.ShapeDtypeStruct((rows, MEM_HEADS, MEM_HEAD_DIM), F32)
    out_hm = jax.ShapeDtypeStruct((nb, MEM_HEADS, N_MEM, MEM_HEAD_DIM), BF16)
    return pl.pallas_call(
        functools.partial(_mem_kv_kernel, n_cast=len(to_cast)),
        grid=(nb,),
        in_specs=[row_spec, _const_spec((1, D_MODEL)),
                  _const_spec((D_MODEL, D_MODEL)), _const_spec((D_MODEL, D_MODEL))] + cast_specs,
        out_specs=(head_spec, head_spec, hm_spec, hm_spec) + tuple(cast_specs),
        out_shape=(out, out, out_hm, out_hm) + tuple(jax.ShapeDtypeStruct(w.shape, BF16) for w in to_cast),
        compiler_params=_params(("arbitrary",)),
        name="mem_kv",
    )(mem2d, g, wk, wv, *to_cast)


def _zero_after(piece):
    bits = pltpu.bitcast(piece, jnp.uint32)
    return pltpu.bitcast((bits >> 16) >> 16, F32)


def _conv_rows(vext, ytm, cw_ref, j, base, *, ns, after):
    acc = _zero_after(after)
    for k in range(CONV_WIDTH):
        acc = acc + cw_ref[j, k:k + 1, :] * vext[j, pl.ds(base + (FIRST_TAP + k) * ns, CONV_ROWS), :]
    ytm[j, pl.ds(base, CONV_ROWS), :] = acc


def _ln_swish(ytm, cb_ref, lng_ref, lnb_ref):
    yc = jnp.concatenate([ytm[j] for j in range(CH_TILES)], axis=1) + cb_ref[...]
    mu = jnp.mean(yc, axis=-1, keepdims=True)
    var = jnp.mean(jnp.square(yc - mu), axis=-1, keepdims=True)
    yn = (yc - mu) * lax.rsqrt(var + EPS) * lng_ref[...] + lnb_ref[...]
    return yn * jax.nn.sigmoid(yn)


def _swiglu_half(x, g, wg_ref, wu_ref, wd_ref):
    hn = _rms(x, g).astype(BF16)
    gate = _dot(hn, wg_ref[...])
    up = _dot(hn, wu_ref[...])
    act = (gate * jax.nn.sigmoid(gate) * up).astype(BF16)
    return x + 0.5 * _dot(act, wd_ref[...])


def _ffn_in_kernel(x_ref, g1_ref, wg_ref, wu_ref, wd_ref, g2_ref, win_ref, x1_ref, z_ref):
    x1 = _swiglu_half(x_ref[...], g1_ref[...], wg_ref, wu_ref, wd_ref)
    x1_ref[...] = x1
    z_ref[...] = _dot(_rms(x1, g2_ref[...]).astype(BF16), win_ref[...])


def _ffn_in(x2d, g1, wg, wu, wd, g2, win, tm=512):
    rows = x2d.shape[0]
    zdim = win.shape[1]
    row_spec = pl.BlockSpec((tm, D_MODEL), lambda i: (i, 0))
    return pl.pallas_call(
        _ffn_in_kernel,
        grid=(rows // tm,),
        in_specs=[row_spec, _const_spec((1, D_MODEL)),
                  _const_spec((D_MODEL, D_FF)), _const_spec((D_MODEL, D_FF)), _const_spec((D_FF, D_MODEL)),
                  _const_spec((1, D_MODEL)), _const_spec((D_MODEL, zdim))],
        out_specs=(row_spec, pl.BlockSpec((tm, zdim), lambda i: (i, 0))),
        out_shape=(jax.ShapeDtypeStruct((rows, D_MODEL), F32), jax.ShapeDtypeStruct((rows, zdim), F32)),
        compiler_params=_params(("arbitrary",)),
        name="ffn_in",
    )(x2d, g1, wg, wu, wd, g2, win)


def _front_kernel(*refs, ns, tt, steps, n_cast):
    (x_ref, g1_ref, wg_ref, wu_ref, wd_ref, g2_ref, win_ref, cw_ref, cb_ref, lng_ref, lnb_ref) = refs[:11]
    cast_in = refs[11:11 + n_cast]
    x1_ref, u_ref, c_ref, hist_ref = refs[11 + n_cast:15 + n_cast]
    cast_out = refs[15 + n_cast:15 + 2 * n_cast]
    vext, ytm, utm = refs[15 + 2 * n_cast:]
    j = pl.program_id(0)
    rows = ns * tt
    hist_rows = HIST * ns
    gate_tile_step = (D_FF // LANES) // CH_TILES

    @pl.when(j == 0)
    def _():
        vext[...] = jnp.zeros_like(vext)

    x = x_ref[...].reshape(rows, D_MODEL)
    hn = _rms(x, g1_ref[...]).astype(BF16)
    gate = _dot(hn, wg_ref[...])
    for jt in range(CH_TILES):
        for b in range(rows // CONV_ROWS):
            _conv_rows(vext, ytm, cw_ref, jt, b * CONV_ROWS, ns=ns,
                       after=gate[b * CONV_ROWS:(b + 1) * CONV_ROWS, _lane_tile(jt * gate_tile_step)])
    c_ref[...] = _ln_swish(ytm, cb_ref, lng_ref, lnb_ref)
    up = _dot(hn, wu_ref[...])
    act = (gate * jax.nn.sigmoid(gate) * up).astype(BF16)
    x1 = x + 0.5 * _dot(act, wd_ref[...])
    x1_ref[...] = x1.reshape(ns, tt, D_MODEL)
    z = _dot(_rms(x1, g2_ref[...]).astype(BF16), win_ref[...])
    u = z[:, :SSM_DIM]
    v = z[:, SSM_DIM:SSM_DIM + CONV_DIM] * jax.nn.sigmoid(z[:, SSM_DIM + CONV_DIM:])

    vext[:, :hist_rows, :] = vext[:, rows:rows + hist_rows, :]
    for s in range(ns):
        seq = slice(s * tt, (s + 1) * tt)
        for jt in range(CH_TILES):
            vext[jt, pl.ds(hist_rows + s, tt, stride=ns), :] = v[seq, _lane_tile(jt)]
            utm[jt, pl.ds(s, tt, stride=ns), :] = u[seq, _lane_tile(jt)]
    u_ref[...] = jnp.concatenate([utm[jt] for jt in range(CH_TILES)], axis=1)

    _cast_blocks(cast_in, cast_out)

    @pl.when(j == steps)
    def _():
        for jt in range(CH_TILES):
            hist_ref[:, :, _lane_tile(jt)] = (
                vext[jt, FIRST_TAP * ns:hist_rows, :].reshape(CONV_WIDTH - 1, ns, LANES))


def _front(x3, g1, wg, wu, wd, g2, win, cw, cb, lng, lnb, to_cast, *, tt):
    ns, seq_len, _ = x3.shape
    steps = seq_len // tt
    rows = ns * tt
    assert rows % CONV_ROWS == 0 and ns == SUBLANES
    zdim = win.shape[1]
    cur = lambda j: jnp.minimum(j, steps - 1)
    prev = lambda j: jnp.clip(j - 1, 0, steps - 1)
    x_spec = pl.BlockSpec((ns, tt, D_MODEL), lambda j: (0, cur(j), 0))
    tm_spec = lambda m: pl.BlockSpec((rows, SSM_DIM), lambda j: (m(j), 0))
    cast_specs = _cast_specs(to_cast, steps + 1)
    return pl.pallas_call(
        functools.partial(_front_kernel, ns=ns, tt=tt, steps=steps, n_cast=len(to_cast)),
        grid=(steps + 1,),
        in_specs=[x_spec, _const_spec((1, D_MODEL)),
                  _const_spec((D_MODEL, D_FF)), _const_spec((D_MODEL, D_FF)), _const_spec((D_FF, D_MODEL)),
                  _const_spec((1, D_MODEL)), _const_spec((D_MODEL, zdim)),
                  _const_spec(cw.shape), _const_spec(cb.shape), _const_spec(lng.shape), _const_spec(lnb.shape)]
                 + cast_specs,
        out_specs=(x_spec, tm_spec(cur), tm_spec(prev),
                   pl.BlockSpec((CONV_WIDTH - 1, ns, CONV_DIM), lambda j: (0, 0, 0))) + tuple(cast_specs),
        out_shape=(jax.ShapeDtypeStruct((ns, seq_len, D_MODEL), F32),
                   jax.ShapeDtypeStruct((seq_len * ns, SSM_DIM), F32),
                   jax.ShapeDtypeStruct((seq_len * ns, CONV_DIM), F32),
                   jax.ShapeDtypeStruct((CONV_WIDTH - 1, ns, CONV_DIM), F32))
                  + tuple(jax.ShapeDtypeStruct(w.shape, BF16) for w in to_cast),
        scratch_shapes=[pltpu.VMEM((CH_TILES, rows + HIST * ns, LANES), F32),
                        pltpu.VMEM((CH_TILES, rows, LANES), F32),
                        pltpu.VMEM((CH_TILES, rows, LANES), F32)],
        compiler_params=_params(("arbitrary",)),
        name="front",
    )(x3, g1, wg, wu, wd, g2, win, cw, cb, lng, lnb, *to_cast)


def _softmax_rows(sc):
    e = jnp.exp(sc - jnp.max(sc, axis=-1, keepdims=True))
    return e / jnp.sum(e, axis=-1, keepdims=True)


def _attn_long_scores(q_ref, k_ref):
    scale = MEM_HEAD_DIM ** -0.5
    return [_dot(q_ref[:, h * MEM_HEAD_DIM:(h + 1) * MEM_HEAD_DIM], k_ref[0, h]) * scale
            for h in range(MEM_HEADS)]


def _attn_long_values(scores, v_ref):
    return jnp.concatenate([_dot(_softmax_rows(sc).astype(BF16), v_ref[0, h]).astype(BF16)
                            for h, sc in enumerate(scores)], axis=1)


def _attn_short_scores(q_ref, k_ref, *, nseq, lq):
    scale = MEM_HEAD_DIM ** -0.5
    kv_rows = N_MEM * MEM_HEADS
    k2d = k_ref.reshape(nseq * kv_rows, MEM_HEAD_DIM)
    shape = (MEM_HEADS * lq, kv_rows)
    same_head = (lax.broadcasted_iota(jnp.int32, shape, 0) // lq
                 == lax.broadcasted_iota(jnp.int32, shape, 1) % MEM_HEADS)
    scores = []
    for s in range(nseq):
        rows = slice(s * lq, (s + 1) * lq)
        q = jnp.concatenate([q_ref[rows, h * MEM_HEAD_DIM:(h + 1) * MEM_HEAD_DIM] for h in range(MEM_HEADS)],
                            axis=0)
        k = k2d[s * kv_rows:(s + 1) * kv_rows, :].astype(BF16)
        sc = lax.dot_general(q, k, (((1,), (1,)), ((), ())), preferred_element_type=F32) * scale
        scores.append(jnp.where(same_head, sc, -jnp.inf))
    return scores


def _attn_short_values(scores, v_ref, *, lq):
    kv_rows = N_MEM * MEM_HEADS
    v2d = v_ref.reshape(len(scores) * kv_rows, MEM_HEAD_DIM)
    out = []
    for s, sc in enumerate(scores):
        v = v2d[s * kv_rows:(s + 1) * kv_rows, :].astype(BF16)
        o = _dot(_softmax_rows(sc).astype(BF16), v)
        out.append(jnp.concatenate([o[h * lq:(h + 1) * lq, :] for h in range(MEM_HEADS)], axis=1))
    return jnp.concatenate(out, axis=0)


def _tail_kernel(x2a_ref, qa_ref, ka_ref, va_ref, x2b_ref, qb_ref, kb_ref, vb_ref,
                 wo_ref, g_ref, wg_ref, wu_ref, wd_ref, gf_ref, ya_ref, yb_ref, oa_scr, ob_scr,
                 *, a_steps, tm, nseq, lq):
    j = pl.program_id(0)
    in_b = j > a_steps

    @pl.when(j == 0)
    def _():
        oa_scr[...] = jnp.zeros_like(oa_scr)
        ob_scr[...] = jnp.zeros_like(ob_scr)

    b_tile = pl.ds(pl.multiple_of(jnp.maximum(j - a_steps - 1, 0) * tm, tm), tm)
    o = jnp.where(in_b, ob_scr[b_tile, :], oa_scr[...])
    x2 = jnp.where(in_b, x2b_ref[...], x2a_ref[...])

    x3 = x2 + _dot(o, wo_ref[...])
    hn = _rms(x3, g_ref[...]).astype(BF16)
    gate = _dot(hn, wg_ref[...])
    up = _dot(hn, wu_ref[...])
    sc_a = _attn_long_scores(qa_ref, ka_ref)
    sc_b = _attn_short_scores(qb_ref, kb_ref, nseq=nseq, lq=lq)
    act = (gate * jax.nn.sigmoid(gate) * up).astype(BF16)
    x4 = x3 + 0.5 * _dot(act, wd_ref[...])

    b_rows = nseq * lq
    blk = jnp.minimum(j, a_steps - 1)
    ob_scr[pl.ds(pl.multiple_of(blk * b_rows, b_rows), b_rows), :] = (
        _attn_short_values(sc_b, vb_ref, lq=lq).astype(BF16))
    oa_scr[...] = _attn_long_values(sc_a, va_ref)

    y = _rms(x4, gf_ref[...])

    @pl.when(jnp.logical_and(j >= 1, j <= a_steps))
    def _():
        ya_ref[...] = y

    @pl.when(in_b)
    def _():
        yb_ref[...] = y


def _tail(x2a, qa, ka_hm, va_hm, x2b, qb, kb3d, vb3d, wo, g, wg, wu, wd, gf, *, tm):
    a_rows, b_rows = x2a.shape[0], x2b.shape[0]
    a_steps, b_steps = a_rows // tm, b_rows // tm
    n_b = kb3d.shape[0] // N_MEM
    lq = b_rows // n_b
    assert n_b % a_steps == 0
    nseq = n_b // a_steps
    tiles_per_seq = a_rows // ka_hm.shape[0] // tm
    attn_blk = lambda j: jnp.minimum(j, a_steps - 1)
    a_fin = lambda j: (jnp.clip(j - 1, 0, a_steps - 1), 0)
    b_fin = lambda j: (jnp.maximum(j - a_steps - 1, 0), 0)
    row = lambda m: pl.BlockSpec((tm, D_MODEL), m)
    ka_spec = pl.BlockSpec((1, MEM_HEADS, N_MEM, MEM_HEAD_DIM),
                           lambda j: (attn_blk(j) // tiles_per_seq, 0, 0, 0))
    kb_spec = pl.BlockSpec((nseq * N_MEM, MEM_HEADS, MEM_HEAD_DIM), lambda j: (attn_blk(j), 0, 0))
    return pl.pallas_call(
        functools.partial(_tail_kernel, a_steps=a_steps, tm=tm, nseq=nseq, lq=lq),
        grid=(a_steps + 1 + b_steps,),
        in_specs=[row(a_fin), row(lambda j: (attn_blk(j), 0)), ka_spec, ka_spec,
                  row(b_fin), pl.BlockSpec((nseq * lq, D_MODEL), lambda j: (attn_blk(j), 0)), kb_spec, kb_spec,
                  _const_spec((D_MODEL, D_MODEL)), _const_spec((1, D_MODEL)),
                  _const_spec((D_MODEL, D_FF)), _const_spec((D_MODEL, D_FF)), _const_spec((D_FF, D_MODEL)),
                  _const_spec((1, D_MODEL))],
        out_specs=(row(a_fin), row(b_fin)),
        out_shape=(jax.ShapeDtypeStruct((a_rows, D_MODEL), F32), jax.ShapeDtypeStruct((b_rows, D_MODEL), F32)),
        scratch_shapes=[pltpu.VMEM((tm, D_MODEL), BF16), pltpu.VMEM((b_rows, D_MODEL), BF16)],
        compiler_params=_params(("arbitrary",)),
        name="tail",
    )(x2a, qa, ka_hm, va_hm, x2b, qb, kb3d, vb3d, wo, g, wg, wu, wd, gf)


def _s5_input_and_conv(utm, vext, ytm, xr, xi, bblk_ref, cw_ref, *, ns, rows):
    def body(kb, carry):
        x = _dot(utm[kb].astype(BF16), bblk_ref[kb])
        for i in range(TILES_PER_CH):
            xr[kb * TILES_PER_CH + i] = x[:, _lane_tile(i)]
            xi[kb * TILES_PER_CH + i] = x[:, _lane_tile(TILES_PER_CH + i)]
        for b in range(rows // CONV_ROWS):
            _conv_rows(vext, ytm, cw_ref, kb, b * CONV_ROWS, ns=ns,
                       after=x[b * CONV_ROWS:(b + 1) * CONV_ROWS, :LANES])
        return carry

    lax.fori_loop(0, CH_TILES, body, 0)


SCAN_UNROLL = 8


def _s5_scan(xr, xi, hcr, hci, ar_ref, ai_ref, *, ns, tt, after=None):
    def step(row, carry, tiles, a_r, a_i):
        new_r, new_i = [], []
        for idx, j in enumerate(tiles):
            h_r, h_i = carry[idx], carry[SCAN_TILE_GROUP + idx]
            n_r = a_r[idx] * h_r - a_i[idx] * h_i + xr[j, row, :]
            n_i = a_r[idx] * h_i + a_i[idx] * h_r + xi[j, row, :]
            xr[j, row, :] = n_r
            xi[j, row, :] = n_i
            new_r.append(n_r)
            new_i.append(n_i)
        return tuple(new_r) + tuple(new_i)

    for i in range(ns // SUBLANES):
        srow = slice(i * SUBLANES, (i + 1) * SUBLANES)
        for jg in range(STATE_TILES // SCAN_TILE_GROUP):
            tiles = tuple(range(jg * SCAN_TILE_GROUP, (jg + 1) * SCAN_TILE_GROUP))
            a_r = [ar_ref[j] for j in tiles]
            a_i = [ai_ref[j] for j in tiles]
            carry = tuple(hcr[j, srow, :] for j in tiles) + tuple(hci[j, srow, :] for j in tiles)
            if after is None:
                def body(t, c, tiles=tiles, a_r=a_r, a_i=a_i, i=i):
                    row = pl.ds(pl.multiple_of(t * ns + i * SUBLANES, SUBLANES), SUBLANES)
                    return step(row, c, tiles, a_r, a_i)

                carry = lax.fori_loop(0, tt, body, carry, unroll=SCAN_UNROLL)
            else:
                for t in range(tt):
                    if t % SCAN_UNROLL == 0:
                        zero = _zero_after(after(t // SCAN_UNROLL, jg))
                        carry = tuple(c + zero for c in carry)
                    carry = step(pl.ds(t * ns + i * SUBLANES, SUBLANES), carry, tiles, a_r, a_i)
            for idx, j in enumerate(tiles):
                hcr[j, srow, :] = carry[idx]
                hci[j, srow, :] = carry[SCAN_TILE_GROUP + idx]


def _s5_readout(u_all, xr, xi, cre_ref, cim_ref, d_ref, wglu_ref):
    ys = []
    for ob in range(CH_TILES):
        h_re = jnp.concatenate([xr[ob * TILES_PER_CH + i] for i in range(TILES_PER_CH)], axis=1).astype(BF16)
        h_im = jnp.concatenate([xi[ob * TILES_PER_CH + i] for i in range(TILES_PER_CH)], axis=1).astype(BF16)
        ys.append(_dot(h_re, cre_ref[ob]) + _dot(h_im, cim_ref[ob]))
    y = jnp.concatenate(ys, axis=1) + d_ref[...] * u_all
    g = jax.nn.gelu(y)
    return g * jax.nn.sigmoid(_dot(g.astype(BF16), wglu_ref[...]))


def _project_delta(s_out, c_out, wout_ref):
    return _dot(jnp.concatenate([s_out, c_out], axis=1).astype(BF16), wout_ref[...])


def _finish_projection(delta, x1_ref, gx_ref, wq_ref, x2_ref, q_ref, dtm, xn, *, ns, tt):
    for j in range(MODEL_TILES):
        dtm[j] = delta[:, _lane_tile(j)]
    for s in range(ns):
        d_s = jnp.concatenate([dtm[j, pl.ds(s, tt, stride=ns), :] for j in range(MODEL_TILES)], axis=1)
        x2_s = x1_ref[s] + d_s
        x2_ref[s] = x2_s
        xn[s * tt:(s + 1) * tt, :] = _rms(x2_s, gx_ref[...])
    qv = _dot(xn[...].astype(BF16), wq_ref[...]).astype(BF16)
    for s in range(ns):
        q_ref[s] = qv[s * tt:(s + 1) * tt, :]


def _state_out(hre_out, him_out, hcr, hci):
    hre_out[...] = jnp.concatenate([hcr[j] for j in range(STATE_TILES)], axis=1)
    him_out[...] = jnp.concatenate([hci[j] for j in range(STATE_TILES)], axis=1)


_S5_WEIGHTS = ("bblk", "cre", "cim", "ab_re", "ab_im", "d", "w_glu")
_CONV_WEIGHTS = ("conv_w", "conv_b", "ln_g", "ln_b")
_OUT_WEIGHTS = ("w_out", "g_x", "w_q")


def _mixer_scratch(ns, rows):
    return [pltpu.VMEM((STATE_TILES, rows, LANES), F32),
            pltpu.VMEM((STATE_TILES, rows, LANES), F32),
            pltpu.VMEM((STATE_TILES, ns, LANES), F32),
            pltpu.VMEM((STATE_TILES, ns, LANES), F32),
            pltpu.VMEM((MODEL_TILES, rows, LANES), F32),
            pltpu.VMEM((rows, D_MODEL), F32)]


def _mixer_kernel(u_ref, val_ref, gate_ref, x1_ref, h0r_ref, h0i_ref, cache_ref,
                  bblk_ref, cre_ref, cim_ref, ar_ref, ai_ref, d_ref, wglu_ref,
                  cw_ref, cb_ref, lng_ref, lnb_ref, wout_ref, gx_ref, wq_ref,
                  x2_ref, q_ref, hre_out, him_out, hist_out,
                  xr, xi, hcr, hci, dtm, xn, utm, vext, ytm, *, ns, tt):
    rows = ns * tt
    hist_rows = HIST * ns

    for j in range(STATE_TILES):
        hcr[j] = h0r_ref[:, _lane_tile(j)]
        hci[j] = h0i_ref[:, _lane_tile(j)]
    for j in range(CH_TILES):
        vext[j, FIRST_TAP * ns:hist_rows, :] = (
            cache_ref[:, :, _lane_tile(j)].reshape((CONV_WIDTH - 1) * ns, LANES))

    for s in range(ns):
        u_s = u_ref[s]
        v_s = val_ref[s] * jax.nn.sigmoid(gate_ref[s])
        for j in range(CH_TILES):
            utm[j, pl.ds(s, tt, stride=ns), :] = u_s[:, _lane_tile(j)]
            vext[j, pl.ds(hist_rows + s, tt, stride=ns), :] = v_s[:, _lane_tile(j)]

    _s5_input_and_conv(utm, vext, ytm, xr, xi, bblk_ref, cw_ref, ns=ns, rows=rows)
    _s5_scan(xr, xi, hcr, hci, ar_ref, ai_ref, ns=ns, tt=tt)
    u_all = jnp.concatenate([utm[j] for j in range(CH_TILES)], axis=1)
    s_out = _s5_readout(u_all, xr, xi, cre_ref, cim_ref, d_ref, wglu_ref)
    c_out = _ln_swish(ytm, cb_ref, lng_ref, lnb_ref)
    delta = _project_delta(s_out, c_out, wout_ref)
    _finish_projection(delta, x1_ref, gx_ref, wq_ref, x2_ref, q_ref, dtm, xn, ns=ns, tt=tt)
    _state_out(hre_out, him_out, hcr, hci)
    for j in range(CH_TILES):
        hist_out[:, :, _lane_tile(j)] = (
            vext[j, (tt + FIRST_TAP) * ns:(tt + HIST) * ns, :].reshape(CONV_WIDTH - 1, ns, LANES))


def _mixer(z3, x13, h0_re, h0_im, hist, w, *, ns):
    nseq, tt, _ = z3.shape
    rows = ns * tt
    assert nseq % ns == 0 and rows % CONV_ROWS == 0 and ns % SUBLANES == 0
    st_spec = pl.BlockSpec((ns, N_STATE), lambda i: (i, 0))
    hist_spec = pl.BlockSpec((CONV_WIDTH - 1, ns, CONV_DIM), lambda i: (0, i, 0))
    row_spec = pl.BlockSpec((ns, tt, D_MODEL), lambda i: (i, 0, 0))
    z_specs = [pl.BlockSpec((ns, tt, SSM_DIM), (lambda c: (lambda i: (i, 0, c)))(c)) for c in range(3)]
    weights = [w[k] for k in _S5_WEIGHTS + _CONV_WEIGHTS + _OUT_WEIGHTS]
    return pl.pallas_call(
        functools.partial(_mixer_kernel, ns=ns, tt=tt),
        grid=(nseq // ns,),
        in_specs=z_specs + [row_spec, st_spec, st_spec, hist_spec] + [_const_spec(a.shape) for a in weights],
        out_specs=(row_spec, row_spec, st_spec, st_spec, hist_spec),
        out_shape=(jax.ShapeDtypeStruct((nseq, tt, D_MODEL), F32),
                   jax.ShapeDtypeStruct((nseq, tt, D_MODEL), BF16),
                   jax.ShapeDtypeStruct((nseq, N_STATE), F32), jax.ShapeDtypeStruct((nseq, N_STATE), F32),
                   jax.ShapeDtypeStruct((CONV_WIDTH - 1, nseq, CONV_DIM), F32)),
        scratch_shapes=_mixer_scratch(ns, rows) + [
            pltpu.VMEM((CH_TILES, rows, LANES), F32),
            pltpu.VMEM((CH_TILES, rows + HIST * ns, LANES), F32),
            pltpu.VMEM((CH_TILES, rows, LANES), F32)],
        compiler_params=_params(("arbitrary",)),
        name="mixer",
    )(z3, z3, z3, x13, h0_re, h0_im, hist, *weights)


def _mixer_tm_kernel(u_ref, c_ref, x1_ref, bblk_ref, cre_ref, cim_ref, ar_ref, ai_ref, d_ref, wglu_ref,
                     wout_ref, gx_ref, wq_ref, x2_ref, q_ref, hre_out, him_out,
                     xr, xi, hcr, hci, dtm, xn, s_prev, *, ns, tt, steps):
    i = pl.program_id(0)

    @pl.when(i == 0)
    def _():
        hcr[...] = jnp.zeros_like(hcr)
        hci[...] = jnp.zeros_like(hci)
        s_prev[...] = jnp.zeros_like(s_prev)

    u = u_ref[...]
    for kb in range(CH_TILES):
        x = _dot(u[:, _lane_tile(kb)].astype(BF16), bblk_ref[kb])
        for t in range(TILES_PER_CH):
            xr[kb * TILES_PER_CH + t] = x[:, _lane_tile(t)]
            xi[kb * TILES_PER_CH + t] = x[:, _lane_tile(TILES_PER_CH + t)]

    delta = _project_delta(s_prev[...], c_ref[...], wout_ref)
    group_rows = SCAN_UNROLL * ns
    tiles_per_group = MODEL_TILES // (STATE_TILES // SCAN_TILE_GROUP)

    def after(group, tile_group):
        r0 = group * group_rows
        return delta[r0:r0 + SUBLANES, _lane_tile(tile_group * tiles_per_group)]

    _s5_scan(xr, xi, hcr, hci, ar_ref, ai_ref, ns=ns, tt=tt, after=after)
    _finish_projection(delta, x1_ref, gx_ref, wq_ref, x2_ref, q_ref, dtm, xn, ns=ns, tt=tt)
    s_prev[...] = _s5_readout(u, xr, xi, cre_ref, cim_ref, d_ref, wglu_ref)

    @pl.when(i == steps - 1)
    def _():
        _state_out(hre_out, him_out, hcr, hci)


def _mixer_tm(u_tm, c_tm, x13, w, *, tt):
    ns, seq_len, _ = x13.shape
    steps = seq_len // tt
    rows = ns * tt
    weights = [w[k] for k in _S5_WEIGHTS + _OUT_WEIGHTS]
    cur = lambda i: jnp.minimum(i, steps - 1)
    prev = lambda i: jnp.clip(i - 1, 0, steps - 1)
    tm_spec = lambda m: pl.BlockSpec((rows, SSM_DIM), lambda i: (m(i), 0))
    row_spec = pl.BlockSpec((ns, tt, D_MODEL), lambda i: (0, prev(i), 0))
    st_spec = pl.BlockSpec((ns, N_STATE), lambda i: (0, 0))
    return pl.pallas_call(
        functools.partial(_mixer_tm_kernel, ns=ns, tt=tt, steps=steps),
        grid=(steps + 1,),
        in_specs=[tm_spec(cur), tm_spec(prev), row_spec] + [_const_spec(a.shape) for a in weights],
        out_specs=(row_spec, row_spec, st_spec, st_spec),
        out_shape=(jax.ShapeDtypeStruct((ns, seq_len, D_MODEL), F32),
                   jax.ShapeDtypeStruct((ns, seq_len, D_MODEL), BF16),
                   jax.ShapeDtypeStruct((ns, N_STATE), F32), jax.ShapeDtypeStruct((ns, N_STATE), F32)),
        scratch_shapes=_mixer_scratch(ns, rows) + [pltpu.VMEM((rows, SSM_DIM), F32)],
        compiler_params=_params(("arbitrary",)),
        name="mixer_tm",
    )(u_tm, c_tm, x13, *weights)


def _row(x):
    return x.reshape(1, -1)


def kernel(x_prompt, x_sample, state_ssm_re, state_ssm_im, cache_conv, cache_mem_k, cache_mem_v, mem_prompt,
           g_mem, w_mem_k, w_mem_v, g_ffn1, w_ffn1_gate, w_ffn1_up, w_ffn1_down, g_mix, w_in,
           ssm_a_re, ssm_a_im, ssm_log_dt, ssm_b_re, ssm_b_im, ssm_c_re, ssm_c_im, ssm_d, w_ssm_glu,
           conv_w, conv_b, conv_ln_g, conv_ln_b, w_out, g_xattn, w_mem_q, w_mem_o,
           g_ffn2, w_ffn2_gate, w_ffn2_up, w_ffn2_down, g_final):
    depth = g_ffn1.shape[0]
    assert depth == 1
    l = 0
    bp, seq, _ = x_prompt.shape
    bs, dseq, _ = x_sample.shape
    bf = lambda a: a.astype(BF16)

    ab_re, ab_im, bblk, cre, cim = _s5_params(ssm_a_re[l], ssm_a_im[l], ssm_log_dt[l],
                                              ssm_b_re[l], ssm_b_im[l], ssm_c_re[l], ssm_c_im[l])
    mixer_w = dict(
        bblk=bblk, cre=cre, cim=cim, ab_re=ab_re, ab_im=ab_im,
        d=_row(ssm_d[l]),
        conv_w=jnp.swapaxes(conv_w[l].reshape(CONV_WIDTH, CH_TILES, LANES), 0, 1),
        conv_b=_row(conv_b[l]), ln_g=_row(conv_ln_g[l]), ln_b=_row(conv_ln_b[l]),
        g_x=_row(g_xattn[l]),
    )
    g_mix_r, g_final_r = _row(g_mix[l]), _row(g_final)

    first = [w_ffn1_gate[l], w_ffn1_up[l], w_ffn1_down[l], w_in[l], w_ssm_glu[l]]
    mk, mv, mk_hm, mv_hm, w1g_b, w1u_b, w1d_b, w_in_b, w_glu_b = _mem_kv(
        mem_prompt.reshape(bp * N_MEM, D_MODEL), _row(g_mem[l]), bf(w_mem_k[l]), bf(w_mem_v[l]), first)
    ffn1 = (_row(g_ffn1[l]), w1g_b, w1u_b, w1d_b)
    mixer_w["w_glu"] = w_glu_b

    p_rows, s_rows = bp * seq, bs * dseq
    prompt_tt = 64
    later = [w_out[l], w_mem_q[l], w_mem_o[l], w_ffn2_gate[l], w_ffn2_up[l], w_ffn2_down[l]]
    x1p, u_tm, c_tm, p_buf, w_out_b, w_q_b, w_o_b, w2g_b, w2u_b, w2d_b = _front(
        x_prompt, *ffn1, g_mix_r, w_in_b, *[mixer_w[k] for k in _CONV_WEIGHTS], later, tt=prompt_tt)
    mixer_w.update(w_out=w_out_b, w_q=w_q_b)
    ffn2 = (_row(g_ffn2[l]), w2g_b, w2u_b, w2d_b)
    x2p, qp, p_re, p_im = _mixer_tm(u_tm, c_tm, x1p, mixer_w, tt=prompt_tt)

    x1s, zs = _ffn_in(x_sample.reshape(s_rows, D_MODEL), *ffn1, g_mix_r, w_in_b)
    x2s, qs, s_re, s_im, s_buf = _mixer(
        zs.reshape(bs, dseq, -1), x1s.reshape(bs, dseq, D_MODEL),
        state_ssm_re[l].reshape(bs, N_STATE), state_ssm_im[l].reshape(bs, N_STATE),
        jnp.swapaxes(cache_conv[l], 0, 1), mixer_w, ns=32)

    kv_rows = lambda a: a.reshape(bs * N_MEM, MEM_HEADS, MEM_HEAD_DIM)
    yp, ys = _tail(x2p.reshape(p_rows, D_MODEL), qp.reshape(p_rows, D_MODEL), mk_hm, mv_hm,
                   x2s.reshape(s_rows, D_MODEL), qs.reshape(s_rows, D_MODEL),
                   kv_rows(cache_mem_k), kv_rows(cache_mem_v), w_o_b, *ffn2, g_final_r, tm=256)
    yp = yp.reshape(bp, seq, D_MODEL)
    ys = ys.reshape(bs, dseq, D_MODEL)

    st = lambda a, n: a.reshape(1, n, SSM_GROUPS, SSM_STATE)
    kv = lambda a: a.reshape(1, bp, N_MEM, MEM_HEADS, MEM_HEAD_DIM)
    buf = lambda a: jnp.swapaxes(a, 0, 1)[None]
    return (yp, ys, st(p_re, bp), st(p_im, bp), buf(p_buf), kv(mk), kv(mv),
            st(s_re, bs), st(s_im, bs), buf(s_buf))
```

```python
import functools

import jax
import jax.numpy as jnp
from jax import lax
from jax.experimental import pallas as pl
from jax.experimental.pallas import tpu as pltpu

F32 = jnp.float32
BF16 = jnp.bfloat16

EPS = 1e-6
D_MODEL = 1024
D_FF = 2816
SSM_DIM = 512
CONV_DIM = 512
SSM_GROUPS = 32
SSM_GROUP_CH = 16
SSM_STATE = 64
N_STATE = SSM_GROUPS * SSM_STATE
CONV_WIDTH = 31
N_MEM = 256
MEM_HEADS = 4
MEM_HEAD_DIM = 256

LANES = 128
SUBLANES = 8
STATE_TILES = N_STATE // LANES
CH_TILES = SSM_DIM // LANES
MODEL_TILES = D_MODEL // LANES
GROUPS_PER_TILE = LANES // SSM_GROUP_CH
STATES_PER_CH_TILE = GROUPS_PER_TILE * SSM_STATE
TILES_PER_CH = STATES_PER_CH_TILE // LANES
HIST = 32
FIRST_TAP = HIST - (CONV_WIDTH - 1)
SCAN_TILE_GROUP = 4
CONV_ROWS = 64
VMEM_LIMIT = 56 * 1024 * 1024


def _dot(a, b):
    return jnp.dot(a, b, preferred_element_type=F32)


def _rms(x, g):
    return x * lax.rsqrt(jnp.mean(x * x, axis=-1, keepdims=True) + EPS) * g


def _lane_tile(j):
    return slice(j * LANES, (j + 1) * LANES)


def _const_spec(shape):
    zeros = (0,) * len(shape)
    return pl.BlockSpec(shape, lambda *_: zeros, pipeline_mode=pl.Buffered(1))


def _params(sem):
    return pltpu.CompilerParams(dimension_semantics=sem, vmem_limit_bytes=VMEM_LIMIT)


def _spread_exact(x, sel):
    hi = x.astype(BF16)
    r1 = x - hi.astype(F32)
    mid = r1.astype(BF16)
    lo = (r1 - mid.astype(F32)).astype(BF16)
    return _dot(hi, sel) + _dot(mid, sel) + _dot(lo, sel)


def _s5_params_kernel(ar_ref, ai_ref, ldt_ref, bre_ref, bim_ref, cre_ref, cim_ref,
                      abr_ref, abi_ref, bblk_ref, cro_ref, cio_ref):
    iota = lambda shape, d: lax.broadcasted_iota(jnp.int32, shape, d)
    sc_shape, cs_shape = (STATES_PER_CH_TILE, LANES), (LANES, STATES_PER_CH_TILE)
    own_sc = (iota(sc_shape, 0) // SSM_STATE) == (iota(sc_shape, 1) // SSM_GROUP_CH)
    own_cs = (iota(cs_shape, 0) // SSM_GROUP_CH) == (iota(cs_shape, 1) // SSM_STATE)
    spread_c = (iota((SSM_GROUP_CH, LANES), 1) % SSM_GROUP_CH == iota((SSM_GROUP_CH, LANES), 0)).astype(BF16)
    spread_n = (iota((SSM_STATE, STATES_PER_CH_TILE), 1) % SSM_STATE
                == iota((SSM_STATE, STATES_PER_CH_TILE), 0)).astype(BF16)
    gs_shape = (GROUPS_PER_TILE, STATES_PER_CH_TILE)
    own_gs = (iota(gs_shape, 1) // SSM_STATE) == iota(gs_shape, 0)

    def states(x):
        return jnp.sum(jnp.where(own_gs, x, 0.0), axis=0, keepdims=True)

    for t in range(CH_TILES):
        lr = states(_spread_exact(ar_ref[t], spread_n))
        li = states(_spread_exact(ai_ref[t], spread_n))
        dt = jnp.exp(states(jnp.broadcast_to(ldt_ref[t], gs_shape)))
        mag = jnp.exp(lr * dt)
        ab_re = mag * jnp.cos(li * dt)
        ab_im = mag * jnp.sin(li * dt)
        den = lr * lr + li * li
        nr = ab_re - 1.0
        ni = ab_im
        coef_re = (nr * lr + ni * li) / den
        coef_im = (ni * lr - nr * li) / den
        for k in range(TILES_PER_CH):
            abr_ref[t * TILES_PER_CH + k] = jnp.broadcast_to(ab_re[:, _lane_tile(k)], (SUBLANES, LANES))
            abi_ref[t * TILES_PER_CH + k] = jnp.broadcast_to(ab_im[:, _lane_tile(k)], (SUBLANES, LANES))
        b_re = jnp.where(own_sc, _spread_exact(bre_ref[t], spread_c), 0.0).T
        b_im = jnp.where(own_sc, _spread_exact(bim_ref[t], spread_c), 0.0).T
        bblk_ref[t, :, :STATES_PER_CH_TILE] = (coef_re * b_re - coef_im * b_im).astype(BF16)
        bblk_ref[t, :, STATES_PER_CH_TILE:] = (coef_re * b_im + coef_im * b_re).astype(BF16)
        c_re = jnp.where(own_cs, _dot(cre_ref[t].astype(BF16), spread_n), 0.0).T
        c_im = jnp.where(own_cs, _dot(cim_ref[t].astype(BF16), spread_n), 0.0).T
        cro_ref[t] = c_re.astype(BF16)
        cio_ref[t] = (-c_im).astype(BF16)


def _s5_params(a_re, a_im, log_dt, b_re, b_im, c_re, c_im):
    a_tiles = lambda a: a.reshape(CH_TILES, GROUPS_PER_TILE, -1)
    a_shape = jax.ShapeDtypeStruct((STATE_TILES, SUBLANES, LANES), F32)
    c_shape = jax.ShapeDtypeStruct((CH_TILES, STATES_PER_CH_TILE, LANES), BF16)
    b_tiles = lambda a: a.reshape(CH_TILES, STATES_PER_CH_TILE, SSM_GROUP_CH)
    c_tiles = lambda a: a.reshape(CH_TILES, LANES, SSM_STATE)
    return pl.pallas_call(
        _s5_params_kernel,
        out_shape=(a_shape, a_shape, jax.ShapeDtypeStruct((CH_TILES, LANES, 2 * STATES_PER_CH_TILE), BF16),
                   c_shape, c_shape),
        name="s5_params",
    )(a_tiles(a_re), a_tiles(a_im), a_tiles(log_dt),
      b_tiles(b_re), b_tiles(b_im), c_tiles(c_re), c_tiles(c_im))


BF16_SUBLANES = 16


def _cast_blocking(w, max_steps):
    n_rows = w.shape[0]
    n_blocks = max(n for n in range(1, max_steps + 1)
                   if n_rows % n == 0 and (n_rows // n) % BF16_SUBLANES == 0)
    return n_rows // n_blocks, n_blocks


def _cast_specs(to_cast, max_steps):
    specs = []
    for w in to_cast:
        blk_rows, n_blocks = _cast_blocking(w, max_steps)
        specs.append(pl.BlockSpec((blk_rows, w.shape[1]),
                                  (lambda n: (lambda j: (jnp.minimum(j, n - 1), 0)))(n_blocks)))
    return specs


def _cast_blocks(cast_in, cast_out):
    for src, dst in zip(cast_in, cast_out):
        dst[...] = src[...].astype(BF16)


def _mem_kv_kernel(*refs, n_cast):
    m_ref, g_ref, wk_ref, wv_ref = refs[:4]
    k_ref, v_ref, kh_ref, vh_ref = refs[4 + n_cast:8 + n_cast]
    _cast_blocks(refs[4:4 + n_cast], refs[8 + n_cast:])
    m = _rms(m_ref[...], g_ref[...]).astype(BF16)
    k = _dot(m, wk_ref[...])
    v = _dot(m, wv_ref[...])
    for h in range(MEM_HEADS):
        cols = slice(h * MEM_HEAD_DIM, (h + 1) * MEM_HEAD_DIM)
        k_ref[:, h, :] = k[:, cols]
        v_ref[:, h, :] = v[:, cols]
        kh_ref[0, h] = k[:, cols].T.astype(BF16)
        vh_ref[0, h] = v[:, cols].astype(BF16)


def _mem_kv(mem2d, g, wk, wv, to_cast):
    rows = mem2d.shape[0]
    nb = rows // N_MEM
    cast_specs = _cast_specs(to_cast, nb)
    row_spec = pl.BlockSpec((N_MEM, D_MODEL), lambda i: (i, 0))
    head_spec = pl.BlockSpec((N_MEM, MEM_HEADS, MEM_HEAD_DIM), lambda i: (i, 0, 0))
    hm_spec = pl.BlockSpec((1, MEM_HEADS, N_MEM, MEM_HEAD_DIM), lambda i: (i, 0, 0, 0))
    out = jax.ShapeDtypeStruct((rows, MEM_HEADS, MEM_HEAD_DIM), F32)
    out_hm = jax.ShapeDtypeStruct((nb, MEM_HEADS, N_MEM, MEM_HEAD_DIM), BF16)
    return pl.pallas_call(
        functools.partial(_mem_kv_kernel, n_cast=len(to_cast)),
        grid=(nb,),
        in_specs=[row_spec, _const_spec((1, D_MODEL)),
                  _const_spec((D_MODEL, D_MODEL)), _const_spec((D_MODEL, D_MODEL))] + cast_specs,
        out_specs=(head_spec, head_spec, hm_spec, hm_spec) + tuple(cast_specs),
        out_shape=(out, out, out_hm, out_hm) + tuple(jax.ShapeDtypeStruct(w.shape, BF16) for w in to_cast),
        compiler_params=_params(("arbitrary",)),
        name="mem_kv",
    )(mem2d, g, wk, wv, *to_cast)


def _zero_after(piece):
    bits = pltpu.bitcast(piece, jnp.uint32)
    return pltpu.bitcast((bits >> 16) >> 16, F32)


def _conv_rows(vext, ytm, cw_ref, j, base, *, ns, after):
    acc = _zero_after(after)
    for k in range(CONV_WIDTH):
        acc = acc + cw_ref[j, k:k + 1, :] * vext[j, pl.ds(base + (FIRST_TAP + k) * ns, CONV_ROWS), :]
    ytm[j, pl.ds(base, CONV_ROWS), :] = acc


def _ln_swish(ytm, cb_ref, lng_ref, lnb_ref):
    yc = jnp.concatenate([ytm[j] for j in range(CH_TILES)], axis=1) + cb_ref[...]
    mu = jnp.mean(yc, axis=-1, keepdims=True)
    var = jnp.mean(jnp.square(yc - mu), axis=-1, keepdims=True)
    yn = (yc - mu) * lax.rsqrt(var + EPS) * lng_ref[...] + lnb_ref[...]
    return yn * jax.nn.sigmoid(yn)


def _swiglu_half(x, g, wg_ref, wu_ref, wd_ref):
    hn = _rms(x, g).astype(BF16)
    gate = _dot(hn, wg_ref[...])
    up = _dot(hn, wu_ref[...])
    act = (gate * jax.nn.sigmoid(gate) * up).astype(BF16)
    return x + 0.5 * _dot(act, wd_ref[...])


def _ffn_in_kernel(x_ref, g1_ref, wg_ref, wu_ref, wd_ref, g2_ref, win_ref, x1_ref, z_ref):
    x1 = _swiglu_half(x_ref[...], g1_ref[...], wg_ref, wu_ref, wd_ref)
    x1_ref[...] = x1
    z_ref[...] = _dot(_rms(x1, g2_ref[...]).astype(BF16), win_ref[...])


def _ffn_in(x2d, g1, wg, wu, wd, g2, win, tm=512):
    rows = x2d.shape[0]
    zdim = win.shape[1]
    row_spec = pl.BlockSpec((tm, D_MODEL), lambda i: (i, 0))
    return pl.pallas_call(
        _ffn_in_kernel,
        grid=(rows // tm,),
        in_specs=[row_spec, _const_spec((1, D_MODEL)),
                  _const_spec((D_MODEL, D_FF)), _const_spec((D_MODEL, D_FF)), _const_spec((D_FF, D_MODEL)),
                  _const_spec((1, D_MODEL)), _const_spec((D_MODEL, zdim))],
        out_specs=(row_spec, pl.BlockSpec((tm, zdim), lambda i: (i, 0))),
        out_shape=(jax.ShapeDtypeStruct((rows, D_MODEL), F32), jax.ShapeDtypeStruct((rows, zdim), F32)),
        compiler_params=_params(("arbitrary",)),
        name="ffn_in",
    )(x2d, g1, wg, wu, wd, g2, win)


def _front_kernel(*refs, ns, tt, steps, n_cast):
    (x_ref, g1_ref, wg_ref, wu_ref, wd_ref, g2_ref, win_ref, cw_ref, cb_ref, lng_ref, lnb_ref) = refs[:11]
    cast_in = refs[11:11 + n_cast]
    x1_ref, u_ref, c_ref, hist_ref = refs[11 + n_cast:15 + n_cast]
    cast_out = refs[15 + n_cast:15 + 2 * n_cast]
    vext, ytm, utm = refs[15 + 2 * n_cast:]
    j = pl.program_id(0)
    rows = ns * tt
    hist_rows = HIST * ns
    gate_tile_step = (D_FF // LANES) // CH_TILES

    @pl.when(j == 0)
    def _():
        vext[...] = jnp.zeros_like(vext)

    x = x_ref[...].reshape(rows, D_MODEL)
    hn = _rms(x, g1_ref[...]).astype(BF16)
    gate = _dot(hn, wg_ref[...])
    for jt in range(CH_TILES):
        for b in range(rows // CONV_ROWS):
            _conv_rows(vext, ytm, cw_ref, jt, b * CONV_ROWS, ns=ns,
                       after=gate[b * CONV_ROWS:(b + 1) * CONV_ROWS, _lane_tile(jt * gate_tile_step)])
    c_ref[...] = _ln_swish(ytm, cb_ref, lng_ref, lnb_ref)
    up = _dot(hn, wu_ref[...])
    act = (gate * jax.nn.sigmoid(gate) * up).astype(BF16)
    x1 = x + 0.5 * _dot(act, wd_ref[...])
    x1_ref[...] = x1.reshape(ns, tt, D_MODEL)
    z = _dot(_rms(x1, g2_ref[...]).astype(BF16), win_ref[...])
    u = z[:, :SSM_DIM]
    v = z[:, SSM_DIM:SSM_DIM + CONV_DIM] * jax.nn.sigmoid(z[:, SSM_DIM + CONV_DIM:])

    vext[:, :hist_rows, :] = vext[:, rows:rows + hist_rows, :]
    for s in range(ns):
        seq = slice(s * tt, (s + 1) * tt)
        for jt in range(CH_TILES):
            vext[jt, pl.ds(hist_rows + s, tt, stride=ns), :] = v[seq, _lane_tile(jt)]
            utm[jt, pl.ds(s, tt, stride=ns), :] = u[seq, _lane_tile(jt)]
    u_ref[...] = jnp.concatenate([utm[jt] for jt in range(CH_TILES)], axis=1)

    _cast_blocks(cast_in, cast_out)

    @pl.when(j == steps)
    def _():
        for jt in range(CH_TILES):
            hist_ref[:, :, _lane_tile(jt)] = (
                vext[jt, FIRST_TAP * ns:hist_rows, :].reshape(CONV_WIDTH - 1, ns, LANES))


def _front(x3, g1, wg, wu, wd, g2, win, cw, cb, lng, lnb, to_cast, *, tt):
    ns, seq_len, _ = x3.shape
    steps = seq_len // tt
    rows = ns * tt
    assert rows % CONV_ROWS == 0 and ns == SUBLANES
    zdim = win.shape[1]
    cur = lambda j: jnp.minimum(j, steps - 1)
    prev = lambda j: jnp.clip(j - 1, 0, steps - 1)
    x_spec = pl.BlockSpec((ns, tt, D_MODEL), lambda j: (0, cur(j), 0))
    tm_spec = lambda m: pl.BlockSpec((rows, SSM_DIM), lambda j: (m(j), 0))
    cast_specs = _cast_specs(to_cast, steps + 1)
    return pl.pallas_call(
        functools.partial(_front_kernel, ns=ns, tt=tt, steps=steps, n_cast=len(to_cast)),
        grid=(steps + 1,),
        in_specs=[x_spec, _const_spec((1, D_MODEL)),
                  _const_spec((D_MODEL, D_FF)), _const_spec((D_MODEL, D_FF)), _const_spec((D_FF, D_MODEL)),
                  _const_spec((1, D_MODEL)), _const_spec((D_MODEL, zdim)),
                  _const_spec(cw.shape), _const_spec(cb.shape), _const_spec(lng.shape), _const_spec(lnb.shape)]
                 + cast_specs,
        out_specs=(x_spec, tm_spec(cur), tm_spec(prev),
                   pl.BlockSpec((CONV_WIDTH - 1, ns, CONV_DIM), lambda j: (0, 0, 0))) + tuple(cast_specs),
        out_shape=(jax.ShapeDtypeStruct((ns, seq_len, D_MODEL), F32),
                   jax.ShapeDtypeStruct((seq_len * ns, SSM_DIM), F32),
                   jax.ShapeDtypeStruct((seq_len * ns, CONV_DIM), F32),
                   jax.ShapeDtypeStruct((CONV_WIDTH - 1, ns, CONV_DIM), F32))
                  + tuple(jax.ShapeDtypeStruct(w.shape, BF16) for w in to_cast),
        scratch_shapes=[pltpu.VMEM((CH_TILES, rows + HIST * ns, LANES), F32),
                        pltpu.VMEM((CH_TILES, rows, LANES), F32),
                        pltpu.VMEM((CH_TILES, rows, LANES), F32)],
        compiler_params=_params(("arbitrary",)),
        name="front",
    )(x3, g1, wg, wu, wd, g2, win, cw, cb, lng, lnb, *to_cast)


def _softmax_rows(sc):
    e = jnp.exp(sc - jnp.max(sc, axis=-1, keepdims=True))
    return e / jnp.sum(e, axis=-1, keepdims=True)


def _attn_long_scores(q_ref, k_ref):
    scale = MEM_HEAD_DIM ** -0.5
    return [_dot(q_ref[:, h * MEM_HEAD_DIM:(h + 1) * MEM_HEAD_DIM], k_ref[0, h]) * scale
            for h in range(MEM_HEADS)]


def _attn_long_values(scores, v_ref):
    return jnp.concatenate([_dot(_softmax_rows(sc).astype(BF16), v_ref[0, h]).astype(BF16)
                            for h, sc in enumerate(scores)], axis=1)


def _attn_short_scores(q_ref, k_ref, *, nseq, lq):
    scale = MEM_HEAD_DIM ** -0.5
    kv_rows = N_MEM * MEM_HEADS
    k2d = k_ref.reshape(nseq * kv_rows, MEM_HEAD_DIM)
    shape = (MEM_HEADS * lq, kv_rows)
    same_head = (lax.broadcasted_iota(jnp.int32, shape, 0) // lq
                 == lax.broadcasted_iota(jnp.int32, shape, 1) % MEM_HEADS)
    scores = []
    for s in range(nseq):
        rows = slice(s * lq, (s + 1) * lq)
        q = jnp.concatenate([q_ref[rows, h * MEM_HEAD_DIM:(h + 1) * MEM_HEAD_DIM] for h in range(MEM_HEADS)],
                            axis=0)
        k = k2d[s * kv_rows:(s + 1) * kv_rows, :].astype(BF16)
        sc = lax.dot_general(q, k, (((1,), (1,)), ((), ())), preferred_element_type=F32) * scale
        scores.append(jnp.where(same_head, sc, -jnp.inf))
    return scores


def _attn_short_values(scores, v_ref, *, lq):
    kv_rows = N_MEM * MEM_HEADS
    v2d = v_ref.reshape(len(scores) * kv_rows, MEM_HEAD_DIM)
    out = []
    for s, sc in enumerate(scores):
        v = v2d[s * kv_rows:(s + 1) * kv_rows, :].astype(BF16)
        o = _dot(_softmax_rows(sc).astype(BF16), v)
        out.append(jnp.concatenate([o[h * lq:(h + 1) * lq, :] for h in range(MEM_HEADS)], axis=1))
    return jnp.concatenate(out, axis=0)


def _tail_kernel(x2a_ref, qa_ref, ka_ref, va_ref, x2b_ref, qb_ref, kb_ref, vb_ref,
                 wo_ref, g_ref, wg_ref, wu_ref, wd_ref, gf_ref, ya_ref, yb_ref, oa_scr, ob_scr,
                 *, a_steps, tm, nseq, lq):
    j = pl.program_id(0)
    in_b = j > a_steps

    @pl.when(j == 0)
    def _():
        oa_scr[...] = jnp.zeros_like(oa_scr)
        ob_scr[...] = jnp.zeros_like(ob_scr)

    b_tile = pl.ds(pl.multiple_of(jnp.maximum(j - a_steps - 1, 0) * tm, tm), tm)
    o = jnp.where(in_b, ob_scr[b_tile, :], oa_scr[...])
    x2 = jnp.where(in_b, x2b_ref[...], x2a_ref[...])

    x3 = x2 + _dot(o, wo_ref[...])
    hn = _rms(x3, g_ref[...]).astype(BF16)
    gate = _dot(hn, wg_ref[...])
    up = _dot(hn, wu_ref[...])
    sc_a = _attn_long_scores(qa_ref, ka_ref)
    sc_b = _attn_short_scores(qb_ref, kb_ref, nseq=nseq, lq=lq)
    act = (gate * jax.nn.sigmoid(gate) * up).astype(BF16)
    x4 = x3 + 0.5 * _dot(act, wd_ref[...])

    b_rows = nseq * lq
    blk = jnp.minimum(j, a_steps - 1)
    ob_scr[pl.ds(pl.multiple_of(blk * b_rows, b_rows), b_rows), :] = (
        _attn_short_values(sc_b, vb_ref, lq=lq).astype(BF16))
    oa_scr[...] = _attn_long_values(sc_a, va_ref)

    y = _rms(x4, gf_ref[...])

    @pl.when(jnp.logical_and(j >= 1, j <= a_steps))
    def _():
        ya_ref[...] = y

    @pl.when(in_b)
    def _():
        yb_ref[...] = y


def _tail(x2a, qa, ka_hm, va_hm, x2b, qb, kb3d, vb3d, wo, g, wg, wu, wd, gf, *, tm):
    a_rows, b_rows = x2a.shape[0], x2b.shape[0]
    a_steps, b_steps = a_rows // tm, b_rows // tm
    n_b = kb3d.shape[0] // N_MEM
    lq = b_rows // n_b
    assert n_b % a_steps == 0
    nseq = n_b // a_steps
    tiles_per_seq = a_rows // ka_hm.shape[0] // tm
    attn_blk = lambda j: jnp.minimum(j, a_steps - 1)
    a_fin = lambda j: (jnp.clip(j - 1, 0, a_steps - 1), 0)
    b_fin = lambda j: (jnp.maximum(j - a_steps - 1, 0), 0)
    row = lambda m: pl.BlockSpec((tm, D_MODEL), m)
    ka_spec = pl.BlockSpec((1, MEM_HEADS, N_MEM, MEM_HEAD_DIM),
                           lambda j: (attn_blk(j) // tiles_per_seq, 0, 0, 0))
    kb_spec = pl.BlockSpec((nseq * N_MEM, MEM_HEADS, MEM_HEAD_DIM), lambda j: (attn_blk(j), 0, 0))
    return pl.pallas_call(
        functools.partial(_tail_kernel, a_steps=a_steps, tm=tm, nseq=nseq, lq=lq),
        grid=(a_steps + 1 + b_steps,),
        in_specs=[row(a_fin), row(lambda j: (attn_blk(j), 0)), ka_spec, ka_spec,
                  row(b_fin), pl.BlockSpec((nseq * lq, D_MODEL), lambda j: (attn_blk(j), 0)), kb_spec, kb_spec,
                  _const_spec((D_MODEL, D_MODEL)), _const_spec((1, D_MODEL)),
                  _const_spec((D_MODEL, D_FF)), _const_spec((D_MODEL, D_FF)), _const_spec((D_FF, D_MODEL)),
                  _const_spec((1, D_MODEL))],
        out_specs=(row(a_fin), row(b_fin)),
        out_shape=(jax.ShapeDtypeStruct((a_rows, D_MODEL), F32), jax.ShapeDtypeStruct((b_rows, D_MODEL), F32)),
        scratch_shapes=[pltpu.VMEM((tm, D_MODEL), BF16), pltpu.VMEM((b_rows, D_MODEL), BF16)],
        compiler_params=_params(("arbitrary",)),
        name="tail",
    )(x2a, qa, ka_hm, va_hm, x2b, qb, kb3d, vb3d, wo, g, wg, wu, wd, gf)


def _s5_input_and_conv(utm, vext, ytm, xr, xi, bblk_ref, cw_ref, *, ns, rows):
    def body(kb, carry):
        x = _dot(utm[kb].astype(BF16), bblk_ref[kb])
        for i in range(TILES_PER_CH):
            xr[kb * TILES_PER_CH + i] = x[:, _lane_tile(i)]
            xi[kb * TILES_PER_CH + i] = x[:, _lane_tile(TILES_PER_CH + i)]
        for b in range(rows // CONV_ROWS):
            _conv_rows(vext, ytm, cw_ref, kb, b * CONV_ROWS, ns=ns,
                       after=x[b * CONV_ROWS:(b + 1) * CONV_ROWS, :LANES])
        return carry

    lax.fori_loop(0, CH_TILES, body, 0)


SCAN_UNROLL = 8


def _s5_scan(xr, xi, hcr, hci, ar_ref, ai_ref, *, ns, tt, after=None):
    def step(row, carry, tiles, a_r, a_i):
        new_r, new_i = [], []
        for idx, j in enumerate(tiles):
            h_r, h_i = carry[idx], carry[SCAN_TILE_GROUP + idx]
            n_r = a_r[idx] * h_r - a_i[idx] * h_i + xr[j, row, :]
            n_i = a_r[idx] * h_i + a_i[idx] * h_r + xi[j, row, :]
            xr[j, row, :] = n_r
            xi[j, row, :] = n_i
            new_r.append(n_r)
            new_i.append(n_i)
        return tuple(new_r) + tuple(new_i)

    for i in range(ns // SUBLANES):
        srow = slice(i * SUBLANES, (i + 1) * SUBLANES)
        for jg in range(STATE_TILES // SCAN_TILE_GROUP):
            tiles = tuple(range(jg * SCAN_TILE_GROUP, (jg + 1) * SCAN_TILE_GROUP))
            a_r = [ar_ref[j] for j in tiles]
            a_i = [ai_ref[j] for j in tiles]
            carry = tuple(hcr[j, srow, :] for j in tiles) + tuple(hci[j, srow, :] for j in tiles)
            if after is None:
                def body(t, c, tiles=tiles, a_r=a_r, a_i=a_i, i=i):
                    row = pl.ds(pl.multiple_of(t * ns + i * SUBLANES, SUBLANES), SUBLANES)
                    return step(row, c, tiles, a_r, a_i)

                carry = lax.fori_loop(0, tt, body, carry, unroll=SCAN_UNROLL)
            else:
                for t in range(tt):
                    if t % SCAN_UNROLL == 0:
                        zero = _zero_after(after(t // SCAN_UNROLL, jg))
                        carry = tuple(c + zero for c in carry)
                    carry = step(pl.ds(t * ns + i * SUBLANES, SUBLANES), carry, tiles, a_r, a_i)
            for idx, j in enumerate(tiles):
                hcr[j, srow, :] = carry[idx]
                hci[j, srow, :] = carry[SCAN_TILE_GROUP + idx]


def _s5_readout(u_all, xr, xi, cre_ref, cim_ref, d_ref, wglu_ref):
    ys = []
    for ob in range(CH_TILES):
        h_re = jnp.concatenate([xr[ob * TILES_PER_CH + i] for i in range(TILES_PER_CH)], axis=1).astype(BF16)
        h_im = jnp.concatenate([xi[ob * TILES_PER_CH + i] for i in range(TILES_PER_CH)], axis=1).astype(BF16)
        ys.append(_dot(h_re, cre_ref[ob]) + _dot(h_im, cim_ref[ob]))
    y = jnp.concatenate(ys, axis=1) + d_ref[...] * u_all
    g = jax.nn.gelu(y)
    return g * jax.nn.sigmoid(_dot(g.astype(BF16), wglu_ref[...]))


def _project_delta(s_out, c_out, wout_ref):
    return _dot(jnp.concatenate([s_out, c_out], axis=1).astype(BF16), wout_ref[...])


def _finish_projection(delta, x1_ref, gx_ref, wq_ref, x2_ref, q_ref, dtm, xn, *, ns, tt):
    for j in range(MODEL_TILES):
        dtm[j] = delta[:, _lane_tile(j)]
    for s in range(ns):
        d_s = jnp.concatenate([dtm[j, pl.ds(s, tt, stride=ns), :] for j in range(MODEL_TILES)], axis=1)
        x2_s = x1_ref[s] + d_s
        x2_ref[s] = x2_s
        xn[s * tt:(s + 1) * tt, :] = _rms(x2_s, gx_ref[...])
    qv = _dot(xn[...].astype(BF16), wq_ref[...]).astype(BF16)
    for s in range(ns):
        q_ref[s] = qv[s * tt:(s + 1) * tt, :]


def _state_out(hre_out, him_out, hcr, hci):
    hre_out[...] = jnp.concatenate([hcr[j] for j in range(STATE_TILES)], axis=1)
    him_out[...] = jnp.concatenate([hci[j] for j in range(STATE_TILES)], axis=1)


_S5_WEIGHTS = ("bblk", "cre", "cim", "ab_re", "ab_im", "d", "w_glu")
_CONV_WEIGHTS = ("conv_w", "conv_b", "ln_g", "ln_b")
_OUT_WEIGHTS = ("w_out", "g_x", "w_q")


def _mixer_scratch(ns, rows):
    return [pltpu.VMEM((STATE_TILES, rows, LANES), F32),
            pltpu.VMEM((STATE_TILES, rows, LANES), F32),
            pltpu.VMEM((STATE_TILES, ns, LANES), F32),
            pltpu.VMEM((STATE_TILES, ns, LANES), F32),
            pltpu.VMEM((MODEL_TILES, rows, LANES), F32),
            pltpu.VMEM((rows, D_MODEL), F32)]


def _mixer_kernel(u_ref, val_ref, gate_ref, x1_ref, h0r_ref, h0i_ref, cache_ref,
                  bblk_ref, cre_ref, cim_ref, ar_ref, ai_ref, d_ref, wglu_ref,
                  cw_ref, cb_ref, lng_ref, lnb_ref, wout_ref, gx_ref, wq_ref,
                  x2_ref, q_ref, hre_out, him_out, hist_out,
                  xr, xi, hcr, hci, dtm, xn, utm, vext, ytm, h0r_s, h0i_s, hro_s, hio_s, *, ns, tt, steps):
    rows = ns * tt
    hist_rows = HIST * ns
    i = pl.program_id(0)
    seqs = pl.ds(pl.multiple_of(i * ns, ns), ns)

    @pl.when(i == 0)
    def _():
        for j in range(STATE_TILES):
            h0r_s[:, _lane_tile(j)] = h0r_ref[_lane_tile(j), :].T
            h0i_s[:, _lane_tile(j)] = h0i_ref[_lane_tile(j), :].T

    for j in range(STATE_TILES):
        hcr[j] = h0r_s[seqs, _lane_tile(j)]
        hci[j] = h0i_s[seqs, _lane_tile(j)]
    for j in range(CH_TILES):
        vext[j, FIRST_TAP * ns:hist_rows, :] = (
            cache_ref[:, :, _lane_tile(j)].reshape((CONV_WIDTH - 1) * ns, LANES))

    for s in range(ns):
        u_s = u_ref[s]
        v_s = val_ref[s] * jax.nn.sigmoid(gate_ref[s])
        for j in range(CH_TILES):
            utm[j, pl.ds(s, tt, stride=ns), :] = u_s[:, _lane_tile(j)]
            vext[j, pl.ds(hist_rows + s, tt, stride=ns), :] = v_s[:, _lane_tile(j)]

    _s5_input_and_conv(utm, vext, ytm, xr, xi, bblk_ref, cw_ref, ns=ns, rows=rows)
    _s5_scan(xr, xi, hcr, hci, ar_ref, ai_ref, ns=ns, tt=tt)
    u_all = jnp.concatenate([utm[j] for j in range(CH_TILES)], axis=1)
    s_out = _s5_readout(u_all, xr, xi, cre_ref, cim_ref, d_ref, wglu_ref)
    c_out = _ln_swish(ytm, cb_ref, lng_ref, lnb_ref)
    delta = _project_delta(s_out, c_out, wout_ref)
    _finish_projection(delta, x1_ref, gx_ref, wq_ref, x2_ref, q_ref, dtm, xn, ns=ns, tt=tt)
    for j in range(CH_TILES):
        hist_out[:, :, _lane_tile(j)] = (
            vext[j, (tt + FIRST_TAP) * ns:(tt + HIST) * ns, :].reshape(CONV_WIDTH - 1, ns, LANES))
    for j in range(STATE_TILES):
        hro_s[seqs, _lane_tile(j)] = hcr[j]
        hio_s[seqs, _lane_tile(j)] = hci[j]

    @pl.when(i == steps - 1)
    def _():
        for j in range(STATE_TILES):
            hre_out[_lane_tile(j), :] = hro_s[:, _lane_tile(j)].T
            him_out[_lane_tile(j), :] = hio_s[:, _lane_tile(j)].T


def _mixer(z3, x13, h0_re, h0_im, hist, w, *, ns):
    nseq, tt, _ = z3.shape
    rows = ns * tt
    assert nseq % ns == 0 and rows % CONV_ROWS == 0 and ns % SUBLANES == 0 and nseq % LANES == 0
    st_spec = pl.BlockSpec((N_STATE, nseq), lambda i: (0, 0))
    hist_spec = pl.BlockSpec((CONV_WIDTH - 1, ns, CONV_DIM), lambda i: (0, i, 0))
    row_spec = pl.BlockSpec((ns, tt, D_MODEL), lambda i: (i, 0, 0))
    z_specs = [pl.BlockSpec((ns, tt, SSM_DIM), (lambda c: (lambda i: (i, 0, c)))(c)) for c in range(3)]
    weights = [w[k] for k in _S5_WEIGHTS + _CONV_WEIGHTS + _OUT_WEIGHTS]
    return pl.pallas_call(
        functools.partial(_mixer_kernel, ns=ns, tt=tt, steps=nseq // ns),
        grid=(nseq // ns,),
        in_specs=z_specs + [row_spec, _const_spec((N_STATE, nseq)), _const_spec((N_STATE, nseq)), hist_spec]
                 + [_const_spec(a.shape) for a in weights],
        out_specs=(row_spec, row_spec, st_spec, st_spec, hist_spec),
        out_shape=(jax.ShapeDtypeStruct((nseq, tt, D_MODEL), F32),
                   jax.ShapeDtypeStruct((nseq, tt, D_MODEL), BF16),
                   jax.ShapeDtypeStruct((N_STATE, nseq), F32), jax.ShapeDtypeStruct((N_STATE, nseq), F32),
                   jax.ShapeDtypeStruct((CONV_WIDTH - 1, nseq, CONV_DIM), F32)),
        scratch_shapes=_mixer_scratch(ns, rows) + [
            pltpu.VMEM((CH_TILES, rows, LANES), F32),
            pltpu.VMEM((CH_TILES, rows + HIST * ns, LANES), F32),
            pltpu.VMEM((CH_TILES, rows, LANES), F32)]
        + [pltpu.VMEM((nseq, N_STATE), F32)] * 4,
        compiler_params=_params(("arbitrary",)),
        name="mixer",
    )(z3, z3, z3, x13, h0_re, h0_im, hist, *weights)


def _mixer_tm_kernel(u_ref, c_ref, x1_ref, bblk_ref, cre_ref, cim_ref, ar_ref, ai_ref, d_ref, wglu_ref,
                     wout_ref, gx_ref, wq_ref, x2_ref, q_ref, hre_out, him_out,
                     xr, xi, hcr, hci, dtm, xn, s_prev, *, ns, tt, steps):
    i = pl.program_id(0)

    @pl.when(i == 0)
    def _():
        hcr[...] = jnp.zeros_like(hcr)
        hci[...] = jnp.zeros_like(hci)
        s_prev[...] = jnp.zeros_like(s_prev)

    u = u_ref[...]
    for kb in range(CH_TILES):
        x = _dot(u[:, _lane_tile(kb)].astype(BF16), bblk_ref[kb])
        for t in range(TILES_PER_CH):
            xr[kb * TILES_PER_CH + t] = x[:, _lane_tile(t)]
            xi[kb * TILES_PER_CH + t] = x[:, _lane_tile(TILES_PER_CH + t)]

    delta = _project_delta(s_prev[...], c_ref[...], wout_ref)
    group_rows = SCAN_UNROLL * ns
    tiles_per_group = MODEL_TILES // (STATE_TILES // SCAN_TILE_GROUP)

    def after(group, tile_group):
        r0 = group * group_rows
        return delta[r0:r0 + SUBLANES, _lane_tile(tile_group * tiles_per_group)]

    _s5_scan(xr, xi, hcr, hci, ar_ref, ai_ref, ns=ns, tt=tt, after=after)
    _finish_projection(delta, x1_ref, gx_ref, wq_ref, x2_ref, q_ref, dtm, xn, ns=ns, tt=tt)
    s_prev[...] = _s5_readout(u, xr, xi, cre_ref, cim_ref, d_ref, wglu_ref)

    @pl.when(i == steps - 1)
    def _():
        _state_out(hre_out, him_out, hcr, hci)


def _mixer_tm(u_tm, c_tm, x13, w, *, tt):
    ns, seq_len, _ = x13.shape
    steps = seq_len // tt
    rows = ns * tt
    weights = [w[k] for k in _S5_WEIGHTS + _OUT_WEIGHTS]
    cur = lambda i: jnp.minimum(i, steps - 1)
    prev = lambda i: jnp.clip(i - 1, 0, steps - 1)
    tm_spec = lambda m: pl.BlockSpec((rows, SSM_DIM), lambda i: (m(i), 0))
    row_spec = pl.BlockSpec((ns, tt, D_MODEL), lambda i: (0, prev(i), 0))
    st_spec = pl.BlockSpec((ns, N_STATE), lambda i: (0, 0))
    return pl.pallas_call(
        functools.partial(_mixer_tm_kernel, ns=ns, tt=tt, steps=steps),
        grid=(steps + 1,),
        in_specs=[tm_spec(cur), tm_spec(prev), row_spec] + [_const_spec(a.shape) for a in weights],
        out_specs=(row_spec, row_spec, st_spec, st_spec),
        out_shape=(jax.ShapeDtypeStruct((ns, seq_len, D_MODEL), F32),
                   jax.ShapeDtypeStruct((ns, seq_len, D_MODEL), BF16),
                   jax.ShapeDtypeStruct((ns, N_STATE), F32), jax.ShapeDtypeStruct((ns, N_STATE), F32)),
        scratch_shapes=_mixer_scratch(ns, rows) + [pltpu.VMEM((rows, SSM_DIM), F32)],
        compiler_params=_params(("arbitrary",)),
        name="mixer_tm",
    )(u_tm, c_tm, x13, *weights)


def _row(x):
    return x.reshape(1, -1)


def kernel(x_prompt, x_sample, state_ssm_re, state_ssm_im, cache_conv, cache_mem_k, cache_mem_v, mem_prompt,
           g_mem, w_mem_k, w_mem_v, g_ffn1, w_ffn1_gate, w_ffn1_up, w_ffn1_down, g_mix, w_in,
           ssm_a_re, ssm_a_im, ssm_log_dt, ssm_b_re, ssm_b_im, ssm_c_re, ssm_c_im, ssm_d, w_ssm_glu,
           conv_w, conv_b, conv_ln_g, conv_ln_b, w_out, g_xattn, w_mem_q, w_mem_o,
           g_ffn2, w_ffn2_gate, w_ffn2_up, w_ffn2_down, g_final):
    depth = g_ffn1.shape[0]
    assert depth == 1
    l = 0
    bp, seq, _ = x_prompt.shape
    bs, dseq, _ = x_sample.shape
    bf = lambda a: a.astype(BF16)

    ab_re, ab_im, bblk, cre, cim = _s5_params(ssm_a_re[l], ssm_a_im[l], ssm_log_dt[l],
                                              ssm_b_re[l], ssm_b_im[l], ssm_c_re[l], ssm_c_im[l])
    mixer_w = dict(
        bblk=bblk, cre=cre, cim=cim, ab_re=ab_re, ab_im=ab_im,
        d=_row(ssm_d[l]),
        conv_w=jnp.swapaxes(conv_w[l].reshape(CONV_WIDTH, CH_TILES, LANES), 0, 1),
        conv_b=_row(conv_b[l]), ln_g=_row(conv_ln_g[l]), ln_b=_row(conv_ln_b[l]),
        g_x=_row(g_xattn[l]),
    )
    g_mix_r, g_final_r = _row(g_mix[l]), _row(g_final)

    first = [w_ffn1_gate[l], w_ffn1_up[l], w_ffn1_down[l], w_in[l], w_ssm_glu[l]]
    mk, mv, mk_hm, mv_hm, w1g_b, w1u_b, w1d_b, w_in_b, w_glu_b = _mem_kv(
        mem_prompt.reshape(bp * N_MEM, D_MODEL), _row(g_mem[l]), bf(w_mem_k[l]), bf(w_mem_v[l]), first)
    ffn1 = (_row(g_ffn1[l]), w1g_b, w1u_b, w1d_b)
    mixer_w["w_glu"] = w_glu_b

    p_rows, s_rows = bp * seq, bs * dseq
    prompt_tt = 64
    later = [w_out[l], w_mem_q[l], w_mem_o[l], w_ffn2_gate[l], w_ffn2_up[l], w_ffn2_down[l]]
    x1p, u_tm, c_tm, p_buf, w_out_b, w_q_b, w_o_b, w2g_b, w2u_b, w2d_b = _front(
        x_prompt, *ffn1, g_mix_r, w_in_b, *[mixer_w[k] for k in _CONV_WEIGHTS], later, tt=prompt_tt)
    mixer_w.update(w_out=w_out_b, w_q=w_q_b)
    ffn2 = (_row(g_ffn2[l]), w2g_b, w2u_b, w2d_b)
    x2p, qp, p_re, p_im = _mixer_tm(u_tm, c_tm, x1p, mixer_w, tt=prompt_tt)

    x1s, zs = _ffn_in(x_sample.reshape(s_rows, D_MODEL), *ffn1, g_mix_r, w_in_b)
    x2s, qs, s_re, s_im, s_buf = _mixer(
        zs.reshape(bs, dseq, -1), x1s.reshape(bs, dseq, D_MODEL),
        state_ssm_re[l].reshape(bs, N_STATE).T, state_ssm_im[l].reshape(bs, N_STATE).T,
        jnp.swapaxes(cache_conv[l], 0, 1), mixer_w, ns=32)

    kv_rows = lambda a: a.reshape(bs * N_MEM, MEM_HEADS, MEM_HEAD_DIM)
    yp, ys = _tail(x2p.reshape(p_rows, D_MODEL), qp.reshape(p_rows, D_MODEL), mk_hm, mv_hm,
                   x2s.reshape(s_rows, D_MODEL), qs.reshape(s_rows, D_MODEL),
                   kv_rows(cache_mem_k), kv_rows(cache_mem_v), w_o_b, *ffn2, g_final_r, tm=256)
    yp = yp.reshape(bp, seq, D_MODEL)
    ys = ys.reshape(bs, dseq, D_MODEL)

    st = lambda a, n: a.reshape(1, n, SSM_GROUPS, SSM_STATE)
    kv = lambda a: a.reshape(1, bp, N_MEM, MEM_HEADS, MEM_HEAD_DIM)
    buf = lambda a: jnp.swapaxes(a, 0, 1)[None]
    return (yp, ys, st(p_re, bp), st(p_im, bp), buf(p_buf), kv(mk), kv(mv),
            st(s_re.T, bs), st(s_im.T, bs), buf(s_buf))
```

```python
import functools

import jax
import jax.numpy as jnp
from jax import lax
from jax.experimental import pallas as pl
from jax.experimental.pallas import tpu as pltpu

F32 = jnp.float32
BF16 = jnp.bfloat16

EPS = 1e-6
D_MODEL = 1024
D_FF = 2816
SSM_DIM = 512
CONV_DIM = 512
SSM_GROUPS = 32
SSM_GROUP_CH = 16
SSM_STATE = 64
N_STATE = SSM_GROUPS * SSM_STATE
CONV_WIDTH = 31
N_MEM = 256
MEM_HEADS = 4
MEM_HEAD_DIM = 256

LANES = 128
SUBLANES = 8
STATE_TILES = N_STATE // LANES
CH_TILES = SSM_DIM // LANES
MODEL_TILES = D_MODEL // LANES
GROUPS_PER_TILE = LANES // SSM_GROUP_CH
STATES_PER_CH_TILE = GROUPS_PER_TILE * SSM_STATE
TILES_PER_CH = STATES_PER_CH_TILE // LANES
HIST = 32
FIRST_TAP = HIST - (CONV_WIDTH - 1)
SCAN_TILE_GROUP = 4
CONV_ROWS = 64
VMEM_LIMIT = 56 * 1024 * 1024


def _dot(a, b):
    return jnp.dot(a, b, preferred_element_type=F32)


def _rms(x, g):
    return x * lax.rsqrt(jnp.mean(x * x, axis=-1, keepdims=True) + EPS) * g


def _lane_tile(j):
    return slice(j * LANES, (j + 1) * LANES)


def _const_spec(shape):
    zeros = (0,) * len(shape)
    return pl.BlockSpec(shape, lambda *_: zeros, pipeline_mode=pl.Buffered(1))


def _params(sem):
    return pltpu.CompilerParams(dimension_semantics=sem, vmem_limit_bytes=VMEM_LIMIT)


def _spread_exact(x, sel):
    hi = x.astype(BF16)
    r1 = x - hi.astype(F32)
    mid = r1.astype(BF16)
    lo = (r1 - mid.astype(F32)).astype(BF16)
    return _dot(hi, sel) + _dot(mid, sel) + _dot(lo, sel)


def _s5_params_kernel(ar_ref, ai_ref, ldt_ref, bre_ref, bim_ref, cre_ref, cim_ref,
                      abr_ref, abi_ref, bblk_ref, cro_ref, cio_ref):
    iota = lambda shape, d: lax.broadcasted_iota(jnp.int32, shape, d)
    sc_shape, cs_shape = (STATES_PER_CH_TILE, LANES), (LANES, STATES_PER_CH_TILE)
    own_sc = (iota(sc_shape, 0) // SSM_STATE) == (iota(sc_shape, 1) // SSM_GROUP_CH)
    own_cs = (iota(cs_shape, 0) // SSM_GROUP_CH) == (iota(cs_shape, 1) // SSM_STATE)
    spread_c = (iota((SSM_GROUP_CH, LANES), 1) % SSM_GROUP_CH == iota((SSM_GROUP_CH, LANES), 0)).astype(BF16)
    spread_n = (iota((SSM_STATE, STATES_PER_CH_TILE), 1) % SSM_STATE
                == iota((SSM_STATE, STATES_PER_CH_TILE), 0)).astype(BF16)
    gs_shape = (GROUPS_PER_TILE, STATES_PER_CH_TILE)
    own_gs = (iota(gs_shape, 1) // SSM_STATE) == iota(gs_shape, 0)

    def states(x):
        return jnp.sum(jnp.where(own_gs, x, 0.0), axis=0, keepdims=True)

    ldt8 = jnp.broadcast_to(ldt_ref[...], (SUBLANES, SSM_GROUPS))
    pick_g = iota((SSM_GROUPS, STATES_PER_CH_TILE), 0)
    pick_j = iota((SSM_GROUPS, STATES_PER_CH_TILE), 1) // SSM_STATE

    for t in range(CH_TILES):
        lr = states(_spread_exact(ar_ref[t], spread_n))
        li = states(_spread_exact(ai_ref[t], spread_n))
        dt = jnp.exp(_spread_exact(ldt8, (pick_g == t * GROUPS_PER_TILE + pick_j).astype(BF16))[:1])
        mag = jnp.exp(lr * dt)
        ab_re = mag * jnp.cos(li * dt)
        ab_im = mag * jnp.sin(li * dt)
        den = lr * lr + li * li
        nr = ab_re - 1.0
        ni = ab_im
        coef_re = (nr * lr + ni * li) / den
        coef_im = (ni * lr - nr * li) / den
        for k in range(TILES_PER_CH):
            abr_ref[t * TILES_PER_CH + k] = jnp.broadcast_to(ab_re[:, _lane_tile(k)], (SUBLANES, LANES))
            abi_ref[t * TILES_PER_CH + k] = jnp.broadcast_to(ab_im[:, _lane_tile(k)], (SUBLANES, LANES))
        b_re = jnp.where(own_sc, _spread_exact(bre_ref[t], spread_c), 0.0).T
        b_im = jnp.where(own_sc, _spread_exact(bim_ref[t], spread_c), 0.0).T
        bblk_ref[t, :, :STATES_PER_CH_TILE] = (coef_re * b_re - coef_im * b_im).astype(BF16)
        bblk_ref[t, :, STATES_PER_CH_TILE:] = (coef_re * b_im + coef_im * b_re).astype(BF16)
        c_re = jnp.where(own_cs, _dot(cre_ref[t].astype(BF16), spread_n), 0.0).T
        c_im = jnp.where(own_cs, _dot(cim_ref[t].astype(BF16), spread_n), 0.0).T
        cro_ref[t] = c_re.astype(BF16)
        cio_ref[t] = (-c_im).astype(BF16)


def _s5_params(a_re, a_im, log_dt, b_re, b_im, c_re, c_im):
    a_tiles = lambda a: a.reshape(CH_TILES, GROUPS_PER_TILE, -1)
    a_shape = jax.ShapeDtypeStruct((STATE_TILES, SUBLANES, LANES), F32)
    c_shape = jax.ShapeDtypeStruct((CH_TILES, STATES_PER_CH_TILE, LANES), BF16)
    b_tiles = lambda a: a.reshape(CH_TILES, STATES_PER_CH_TILE, SSM_GROUP_CH)
    c_tiles = lambda a: a.reshape(CH_TILES, LANES, SSM_STATE)
    return pl.pallas_call(
        _s5_params_kernel,
        out_shape=(a_shape, a_shape, jax.ShapeDtypeStruct((CH_TILES, LANES, 2 * STATES_PER_CH_TILE), BF16),
                   c_shape, c_shape),
        name="s5_params",
    )(a_tiles(a_re), a_tiles(a_im), log_dt,
      b_tiles(b_re), b_tiles(b_im), c_tiles(c_re), c_tiles(c_im))


BF16_SUBLANES = 16


def _cast_blocking(w, max_steps):
    n_rows = w.shape[0]
    n_blocks = max(n for n in range(1, max_steps + 1)
                   if n_rows % n == 0 and (n_rows // n) % BF16_SUBLANES == 0)
    return n_rows // n_blocks, n_blocks


def _cast_specs(to_cast, max_steps):
    specs = []
    for w in to_cast:
        blk_rows, n_blocks = _cast_blocking(w, max_steps)
        specs.append(pl.BlockSpec((blk_rows, w.shape[1]),
                                  (lambda n: (lambda j: (jnp.minimum(j, n - 1), 0)))(n_blocks)))
    return specs


def _cast_blocks(cast_in, cast_out):
    for src, dst in zip(cast_in, cast_out):
        dst[...] = src[...].astype(BF16)


def _mem_kv_kernel(*refs, n_cast):
    m_ref, g_ref, wk_ref, wv_ref = refs[:4]
    k_ref, v_ref, kh_ref, vh_ref = refs[4 + n_cast:8 + n_cast]
    wk_s, wv_s = refs[8 + 2 * n_cast:]

    @pl.when(pl.program_id(0) == 0)
    def _():
        wk_s[...] = wk_ref[...].astype(BF16)
        wv_s[...] = wv_ref[...].astype(BF16)

    _cast_blocks(refs[4:4 + n_cast], refs[8 + n_cast:8 + 2 * n_cast])
    m = _rms(m_ref[...], g_ref[...]).astype(BF16)
    k = _dot(m, wk_s[...])
    v = _dot(m, wv_s[...])
    for h in range(MEM_HEADS):
        cols = slice(h * MEM_HEAD_DIM, (h + 1) * MEM_HEAD_DIM)
        k_ref[:, h, :] = k[:, cols]
        v_ref[:, h, :] = v[:, cols]
        kh_ref[0, h] = k[:, cols].T.astype(BF16)
        vh_ref[0, h] = v[:, cols].astype(BF16)


def _mem_kv(mem2d, g, wk, wv, to_cast):
    rows = mem2d.shape[0]
    nb = rows // N_MEM
    cast_specs = _cast_specs(to_cast, nb)
    row_spec = pl.BlockSpec((N_MEM, D_MODEL), lambda i: (i, 0))
    head_spec = pl.BlockSpec((N_MEM, MEM_HEADS, MEM_HEAD_DIM), lambda i: (i, 0, 0))
    hm_spec = pl.BlockSpec((1, MEM_HEADS, N_MEM, MEM_HEAD_DIM), lambda i: (i, 0, 0, 0))
    out = jax.ShapeDtypeStruct((rows, MEM_HEADS, MEM_HEAD_DIM), F32)
    out_hm = jax.ShapeDtypeStruct((nb, MEM_HEADS, N_MEM, MEM_HEAD_DIM), BF16)
    return pl.pallas_call(
        functools.partial(_mem_kv_kernel, n_cast=len(to_cast)),
        grid=(nb,),
        in_specs=[row_spec, _const_spec((1, D_MODEL)),
                  _const_spec((D_MODEL, D_MODEL)), _const_spec((D_MODEL, D_MODEL))] + cast_specs,
        out_specs=(head_spec, head_spec, hm_spec, hm_spec) + tuple(cast_specs),
        out_shape=(out, out, out_hm, out_hm) + tuple(jax.ShapeDtypeStruct(w.shape, BF16) for w in to_cast),
        scratch_shapes=[pltpu.VMEM((D_MODEL, D_MODEL), BF16)] * 2,
        compiler_params=_params(("arbitrary",)),
        name="mem_kv",
    )(mem2d, g, wk, wv, *to_cast)


def _zero_after(piece):
    bits = pltpu.bitcast(piece, jnp.uint32)
    return pltpu.bitcast((bits >> 16) >> 16, F32)


def _conv_rows(vext, ytm, cw_ref, j, base, *, ns, after):
    acc = _zero_after(after)
    for k in range(CONV_WIDTH):
        acc = acc + cw_ref[j, k:k + 1, :] * vext[j, pl.ds(base + (FIRST_TAP + k) * ns, CONV_ROWS), :]
    ytm[j, pl.ds(base, CONV_ROWS), :] = acc


def _ln_swish(ytm, cb_ref, lng_ref, lnb_ref):
    yc = jnp.concatenate([ytm[j] for j in range(CH_TILES)], axis=1) + cb_ref[...]
    mu = jnp.mean(yc, axis=-1, keepdims=True)
    var = jnp.mean(jnp.square(yc - mu), axis=-1, keepdims=True)
    yn = (yc - mu) * lax.rsqrt(var + EPS) * lng_ref[...] + lnb_ref[...]
    return yn * jax.nn.sigmoid(yn)


def _swiglu_half(x, g, wg_ref, wu_ref, wd_ref):
    hn = _rms(x, g).astype(BF16)
    gate = _dot(hn, wg_ref[...])
    up = _dot(hn, wu_ref[...])
    act = (gate * jax.nn.sigmoid(gate) * up).astype(BF16)
    return x + 0.5 * _dot(act, wd_ref[...])


def _ffn_in_kernel(x_ref, g1_ref, wg_ref, wu_ref, wd_ref, g2_ref, win_ref, x1_ref, z_ref):
    x1 = _swiglu_half(x_ref[...], g1_ref[...], wg_ref, wu_ref, wd_ref)
    x1_ref[...] = x1
    z_ref[...] = _dot(_rms(x1, g2_ref[...]).astype(BF16), win_ref[...])


def _ffn_in(x2d, g1, wg, wu, wd, g2, win, tm=512):
    rows = x2d.shape[0]
    zdim = win.shape[1]
    row_spec = pl.BlockSpec((tm, D_MODEL), lambda i: (i, 0))
    return pl.pallas_call(
        _ffn_in_kernel,
        grid=(rows // tm,),
        in_specs=[row_spec, _const_spec((1, D_MODEL)),
                  _const_spec((D_MODEL, D_FF)), _const_spec((D_MODEL, D_FF)), _const_spec((D_FF, D_MODEL)),
                  _const_spec((1, D_MODEL)), _const_spec((D_MODEL, zdim))],
        out_specs=(row_spec, pl.BlockSpec((tm, zdim), lambda i: (i, 0))),
        out_shape=(jax.ShapeDtypeStruct((rows, D_MODEL), F32), jax.ShapeDtypeStruct((rows, zdim), F32)),
        compiler_params=_params(("arbitrary",)),
        name="ffn_in",
    )(x2d, g1, wg, wu, wd, g2, win)


def _front_kernel(*refs, ns, tt, steps, n_cast):
    (x_ref, g1_ref, wg_ref, wu_ref, wd_ref, g2_ref, win_ref, cw_ref, cb_ref, lng_ref, lnb_ref) = refs[:11]
    cast_in = refs[11:11 + n_cast]
    x1_ref, u_ref, c_ref, hist_ref = refs[11 + n_cast:15 + n_cast]
    cast_out = refs[15 + n_cast:15 + 2 * n_cast]
    vext, ytm, utm = refs[15 + 2 * n_cast:]
    j = pl.program_id(0)
    rows = ns * tt
    hist_rows = HIST * ns
    gate_tile_step = (D_FF // LANES) // CH_TILES

    @pl.when(j == 0)
    def _():
        vext[...] = jnp.zeros_like(vext)

    x = x_ref[...].reshape(rows, D_MODEL)
    hn = _rms(x, g1_ref[...]).astype(BF16)
    gate = _dot(hn, wg_ref[...])
    for jt in range(CH_TILES):
        for b in range(rows // CONV_ROWS):
            _conv_rows(vext, ytm, cw_ref, jt, b * CONV_ROWS, ns=ns,
                       after=gate[b * CONV_ROWS:(b + 1) * CONV_ROWS, _lane_tile(jt * gate_tile_step)])
    c_ref[...] = _ln_swish(ytm, cb_ref, lng_ref, lnb_ref)
    up = _dot(hn, wu_ref[...])
    act = (gate * jax.nn.sigmoid(gate) * up).astype(BF16)
    x1 = x + 0.5 * _dot(act, wd_ref[...])
    x1_ref[...] = x1.reshape(ns, tt, D_MODEL)
    z = _dot(_rms(x1, g2_ref[...]).astype(BF16), win_ref[...])
    u = z[:, :SSM_DIM]
    v = z[:, SSM_DIM:SSM_DIM + CONV_DIM] * jax.nn.sigmoid(z[:, SSM_DIM + CONV_DIM:])

    vext[:, :hist_rows, :] = vext[:, rows:rows + hist_rows, :]
    for s in range(ns):
        seq = slice(s * tt, (s + 1) * tt)
        for jt in range(CH_TILES):
            vext[jt, pl.ds(hist_rows + s, tt, stride=ns), :] = v[seq, _lane_tile(jt)]
            utm[jt, pl.ds(s, tt, stride=ns), :] = u[seq, _lane_tile(jt)]
    u_ref[...] = jnp.concatenate([utm[jt] for jt in range(CH_TILES)], axis=1)

    _cast_blocks(cast_in, cast_out)

    @pl.when(j == steps)
    def _():
        for jt in range(CH_TILES):
            hist_ref[:, :, _lane_tile(jt)] = (
                vext[jt, FIRST_TAP * ns:hist_rows, :].reshape(CONV_WIDTH - 1, ns, LANES))


def _front(x3, g1, wg, wu, wd, g2, win, cw, cb, lng, lnb, to_cast, *, tt):
    ns, seq_len, _ = x3.shape
    steps = seq_len // tt
    rows = ns * tt
    assert rows % CONV_ROWS == 0 and ns == SUBLANES
    zdim = win.shape[1]
    cur = lambda j: jnp.minimum(j, steps - 1)
    prev = lambda j: jnp.clip(j - 1, 0, steps - 1)
    x_spec = pl.BlockSpec((ns, tt, D_MODEL), lambda j: (0, cur(j), 0))
    tm_spec = lambda m: pl.BlockSpec((rows, SSM_DIM), lambda j: (m(j), 0))
    cast_specs = _cast_specs(to_cast, steps + 1)
    return pl.pallas_call(
        functools.partial(_front_kernel, ns=ns, tt=tt, steps=steps, n_cast=len(to_cast)),
        grid=(steps + 1,),
        in_specs=[x_spec, _const_spec((1, D_MODEL)),
                  _const_spec((D_MODEL, D_FF)), _const_spec((D_MODEL, D_FF)), _const_spec((D_FF, D_MODEL)),
                  _const_spec((1, D_MODEL)), _const_spec((D_MODEL, zdim)),
                  _const_spec(cw.shape), _const_spec(cb.shape), _const_spec(lng.shape), _const_spec(lnb.shape)]
                 + cast_specs,
        out_specs=(x_spec, tm_spec(cur), tm_spec(prev),
                   pl.BlockSpec((CONV_WIDTH - 1, ns, CONV_DIM), lambda j: (0, 0, 0))) + tuple(cast_specs),
        out_shape=(jax.ShapeDtypeStruct((ns, seq_len, D_MODEL), F32),
                   jax.ShapeDtypeStruct((seq_len * ns, SSM_DIM), F32),
                   jax.ShapeDtypeStruct((seq_len * ns, CONV_DIM), F32),
                   jax.ShapeDtypeStruct((CONV_WIDTH - 1, ns, CONV_DIM), F32))
                  + tuple(jax.ShapeDtypeStruct(w.shape, BF16) for w in to_cast),
        scratch_shapes=[pltpu.VMEM((CH_TILES, rows + HIST * ns, LANES), F32),
                        pltpu.VMEM((CH_TILES, rows, LANES), F32),
                        pltpu.VMEM((CH_TILES, rows, LANES), F32)],
        compiler_params=_params(("arbitrary",)),
        name="front",
    )(x3, g1, wg, wu, wd, g2, win, cw, cb, lng, lnb, *to_cast)


def _softmax_rows(sc):
    e = jnp.exp(sc - jnp.max(sc, axis=-1, keepdims=True))
    return e / jnp.sum(e, axis=-1, keepdims=True)


def _attn_long_scores(q_ref, k_ref):
    scale = MEM_HEAD_DIM ** -0.5
    return [_dot(q_ref[:, h * MEM_HEAD_DIM:(h + 1) * MEM_HEAD_DIM], k_ref[0, h]) * scale
            for h in range(MEM_HEADS)]


def _attn_long_values(scores, v_ref):
    return jnp.concatenate([_dot(_softmax_rows(sc).astype(BF16), v_ref[0, h]).astype(BF16)
                            for h, sc in enumerate(scores)], axis=1)


def _attn_short_scores(q_ref, k_ref, *, nseq, lq):
    scale = MEM_HEAD_DIM ** -0.5
    kv_rows = N_MEM * MEM_HEADS
    k2d = k_ref.reshape(nseq * kv_rows, MEM_HEAD_DIM)
    shape = (MEM_HEADS * lq, kv_rows)
    same_head = (lax.broadcasted_iota(jnp.int32, shape, 0) // lq
                 == lax.broadcasted_iota(jnp.int32, shape, 1) % MEM_HEADS)
    scores = []
    for s in range(nseq):
        rows = slice(s * lq, (s + 1) * lq)
        q = jnp.concatenate([q_ref[rows, h * MEM_HEAD_DIM:(h + 1) * MEM_HEAD_DIM] for h in range(MEM_HEADS)],
                            axis=0)
        k = k2d[s * kv_rows:(s + 1) * kv_rows, :].astype(BF16)
        sc = lax.dot_general(q, k, (((1,), (1,)), ((), ())), preferred_element_type=F32) * scale
        scores.append(jnp.where(same_head, sc, -jnp.inf))
    return scores


def _attn_short_values(scores, v_ref, *, lq):
    kv_rows = N_MEM * MEM_HEADS
    v2d = v_ref.reshape(len(scores) * kv_rows, MEM_HEAD_DIM)
    out = []
    for s, sc in enumerate(scores):
        v = v2d[s * kv_rows:(s + 1) * kv_rows, :].astype(BF16)
        o = _dot(_softmax_rows(sc).astype(BF16), v)
        out.append(jnp.concatenate([o[h * lq:(h + 1) * lq, :] for h in range(MEM_HEADS)], axis=1))
    return jnp.concatenate(out, axis=0)


def _tail_kernel(x2a_ref, qa_ref, ka_ref, va_ref, x2b_ref, qb_ref, kb_ref, vb_ref,
                 wo_ref, g_ref, wg_ref, wu_ref, wd_ref, gf_ref, ya_ref, yb_ref, oa_scr, ob_scr,
                 *, a_steps, tm, nseq, lq):
    j = pl.program_id(0)
    in_b = j > a_steps

    @pl.when(j == 0)
    def _():
        oa_scr[...] = jnp.zeros_like(oa_scr)
        ob_scr[...] = jnp.zeros_like(ob_scr)

    b_tile = pl.ds(pl.multiple_of(jnp.maximum(j - a_steps - 1, 0) * tm, tm), tm)
    o = jnp.where(in_b, ob_scr[b_tile, :], oa_scr[...])
    x2 = jnp.where(in_b, x2b_ref[...], x2a_ref[...])

    x3 = x2 + _dot(o, wo_ref[...])
    hn = _rms(x3, g_ref[...]).astype(BF16)
    gate = _dot(hn, wg_ref[...])
    up = _dot(hn, wu_ref[...])
    sc_a = _attn_long_scores(qa_ref, ka_ref)
    sc_b = _attn_short_scores(qb_ref, kb_ref, nseq=nseq, lq=lq)
    act = (gate * jax.nn.sigmoid(gate) * up).astype(BF16)
    x4 = x3 + 0.5 * _dot(act, wd_ref[...])

    b_rows = nseq * lq
    blk = jnp.minimum(j, a_steps - 1)
    ob_scr[pl.ds(pl.multiple_of(blk * b_rows, b_rows), b_rows), :] = (
        _attn_short_values(sc_b, vb_ref, lq=lq).astype(BF16))
    oa_scr[...] = _attn_long_values(sc_a, va_ref)

    y = _rms(x4, gf_ref[...])

    @pl.when(jnp.logical_and(j >= 1, j <= a_steps))
    def _():
        ya_ref[...] = y

    @pl.when(in_b)
    def _():
        yb_ref[...] = y


def _tail(x2a, qa, ka_hm, va_hm, x2b, qb, kb3d, vb3d, wo, g, wg, wu, wd, gf, *, tm):
    a_rows, b_rows = x2a.shape[0], x2b.shape[0]
    a_steps, b_steps = a_rows // tm, b_rows // tm
    n_b = kb3d.shape[0] // N_MEM
    lq = b_rows // n_b
    assert n_b % a_steps == 0
    nseq = n_b // a_steps
    tiles_per_seq = a_rows // ka_hm.shape[0] // tm
    attn_blk = lambda j: jnp.minimum(j, a_steps - 1)
    a_fin = lambda j: (jnp.clip(j - 1, 0, a_steps - 1), 0)
    b_fin = lambda j: (jnp.maximum(j - a_steps - 1, 0), 0)
    row = lambda m: pl.BlockSpec((tm, D_MODEL), m)
    ka_spec = pl.BlockSpec((1, MEM_HEADS, N_MEM, MEM_HEAD_DIM),
                           lambda j: (attn_blk(j) // tiles_per_seq, 0, 0, 0))
    kb_spec = pl.BlockSpec((nseq * N_MEM, MEM_HEADS, MEM_HEAD_DIM), lambda j: (attn_blk(j), 0, 0))
    return pl.pallas_call(
        functools.partial(_tail_kernel, a_steps=a_steps, tm=tm, nseq=nseq, lq=lq),
        grid=(a_steps + 1 + b_steps,),
        in_specs=[row(a_fin), row(lambda j: (attn_blk(j), 0)), ka_spec, ka_spec,
                  row(b_fin), pl.BlockSpec((nseq * lq, D_MODEL), lambda j: (attn_blk(j), 0)), kb_spec, kb_spec,
                  _const_spec((D_MODEL, D_MODEL)), _const_spec((1, D_MODEL)),
                  _const_spec((D_MODEL, D_FF)), _const_spec((D_MODEL, D_FF)), _const_spec((D_FF, D_MODEL)),
                  _const_spec((1, D_MODEL))],
        out_specs=(row(a_fin), row(b_fin)),
        out_shape=(jax.ShapeDtypeStruct((a_rows, D_MODEL), F32), jax.ShapeDtypeStruct((b_rows, D_MODEL), F32)),
        scratch_shapes=[pltpu.VMEM((tm, D_MODEL), BF16), pltpu.VMEM((b_rows, D_MODEL), BF16)],
        compiler_params=_params(("arbitrary",)),
        name="tail",
    )(x2a, qa, ka_hm, va_hm, x2b, qb, kb3d, vb3d, wo, g, wg, wu, wd, gf)


def _s5_input_and_conv(utm, vext, ytm, xr, xi, bblk_ref, cw_ref, *, ns, rows):
    def body(kb, carry):
        x = _dot(utm[kb].astype(BF16), bblk_ref[kb])
        for i in range(TILES_PER_CH):
            xr[kb * TILES_PER_CH + i] = x[:, _lane_tile(i)]
            xi[kb * TILES_PER_CH + i] = x[:, _lane_tile(TILES_PER_CH + i)]
        for b in range(rows // CONV_ROWS):
            _conv_rows(vext, ytm, cw_ref, kb, b * CONV_ROWS, ns=ns,
                       after=x[b * CONV_ROWS:(b + 1) * CONV_ROWS, :LANES])
        return carry

    lax.fori_loop(0, CH_TILES, body, 0)


SCAN_UNROLL = 8


def _s5_scan(xr, xi, hcr, hci, ar_ref, ai_ref, *, ns, tt, after=None):
    def step(row, carry, tiles, a_r, a_i):
        new_r, new_i = [], []
        for idx, j in enumerate(tiles):
            h_r, h_i = carry[idx], carry[SCAN_TILE_GROUP + idx]
            n_r = a_r[idx] * h_r - a_i[idx] * h_i + xr[j, row, :]
            n_i = a_r[idx] * h_i + a_i[idx] * h_r + xi[j, row, :]
            xr[j, row, :] = n_r
            xi[j, row, :] = n_i
            new_r.append(n_r)
            new_i.append(n_i)
        return tuple(new_r) + tuple(new_i)

    for i in range(ns // SUBLANES):
        srow = slice(i * SUBLANES, (i + 1) * SUBLANES)
        for jg in range(STATE_TILES // SCAN_TILE_GROUP):
            tiles = tuple(range(jg * SCAN_TILE_GROUP, (jg + 1) * SCAN_TILE_GROUP))
            a_r = [ar_ref[j] for j in tiles]
            a_i = [ai_ref[j] for j in tiles]
            carry = tuple(hcr[j, srow, :] for j in tiles) + tuple(hci[j, srow, :] for j in tiles)
            if after is None:
                def body(t, c, tiles=tiles, a_r=a_r, a_i=a_i, i=i):
                    row = pl.ds(pl.multiple_of(t * ns + i * SUBLANES, SUBLANES), SUBLANES)
                    return step(row, c, tiles, a_r, a_i)

                carry = lax.fori_loop(0, tt, body, carry, unroll=SCAN_UNROLL)
            else:
                for t in range(tt):
                    if t % SCAN_UNROLL == 0:
                        zero = _zero_after(after(t // SCAN_UNROLL, jg))
                        carry = tuple(c + zero for c in carry)
                    carry = step(pl.ds(t * ns + i * SUBLANES, SUBLANES), carry, tiles, a_r, a_i)
            for idx, j in enumerate(tiles):
                hcr[j, srow, :] = carry[idx]
                hci[j, srow, :] = carry[SCAN_TILE_GROUP + idx]


def _s5_readout(u_all, xr, xi, cre_ref, cim_ref, d_ref, wglu_ref):
    ys = []
    for ob in range(CH_TILES):
        h_re = jnp.concatenate([xr[ob * TILES_PER_CH + i] for i in range(TILES_PER_CH)], axis=1).astype(BF16)
        h_im = jnp.concatenate([xi[ob * TILES_PER_CH + i] for i in range(TILES_PER_CH)], axis=1).astype(BF16)
        ys.append(_dot(h_re, cre_ref[ob]) + _dot(h_im, cim_ref[ob]))
    y = jnp.concatenate(ys, axis=1) + d_ref[...] * u_all
    g = jax.nn.gelu(y)
    return g * jax.nn.sigmoid(_dot(g.astype(BF16), wglu_ref[...]))


def _project_delta(s_out, c_out, wout_ref):
    return _dot(jnp.concatenate([s_out, c_out], axis=1).astype(BF16), wout_ref[...])


def _finish_projection(delta, x1_ref, gx_ref, wq_ref, x2_ref, q_ref, dtm, xn, *, ns, tt):
    for j in range(MODEL_TILES):
        dtm[j] = delta[:, _lane_tile(j)]
    for s in range(ns):
        d_s = jnp.concatenate([dtm[j, pl.ds(s, tt, stride=ns), :] for j in range(MODEL_TILES)], axis=1)
        x2_s = x1_ref[s] + d_s
        x2_ref[s] = x2_s
        xn[s * tt:(s + 1) * tt, :] = _rms(x2_s, gx_ref[...])
    qv = _dot(xn[...].astype(BF16), wq_ref[...]).astype(BF16)
    for s in range(ns):
        q_ref[s] = qv[s * tt:(s + 1) * tt, :]


def _state_out(hre_out, him_out, hcr, hci):
    hre_out[...] = jnp.concatenate([hcr[j] for j in range(STATE_TILES)], axis=1)
    him_out[...] = jnp.concatenate([hci[j] for j in range(STATE_TILES)], axis=1)


_S5_WEIGHTS = ("bblk", "cre", "cim", "ab_re", "ab_im", "d", "w_glu")
_CONV_WEIGHTS = ("conv_w", "conv_b", "ln_g", "ln_b")
_OUT_WEIGHTS = ("w_out", "g_x", "w_q")


def _mixer_scratch(ns, rows):
    return [pltpu.VMEM((STATE_TILES, rows, LANES), F32),
            pltpu.VMEM((STATE_TILES, rows, LANES), F32),
            pltpu.VMEM((STATE_TILES, ns, LANES), F32),
            pltpu.VMEM((STATE_TILES, ns, LANES), F32),
            pltpu.VMEM((MODEL_TILES, rows, LANES), F32),
            pltpu.VMEM((rows, D_MODEL), F32)]


def _mixer_kernel(u_ref, val_ref, gate_ref, x1_ref, h0r_ref, h0i_ref, cache_ref,
                  bblk_ref, cre_ref, cim_ref, ar_ref, ai_ref, d_ref, wglu_ref,
                  cw_ref, cb_ref, lng_ref, lnb_ref, wout_ref, gx_ref, wq_ref,
                  x2_ref, q_ref, hre_out, him_out, hist_out,
                  xr, xi, hcr, hci, dtm, xn, utm, vext, ytm, h0r_s, h0i_s, hro_s, hio_s, *, ns, tt, steps):
    rows = ns * tt
    hist_rows = HIST * ns
    i = pl.program_id(0)
    seqs = pl.ds(pl.multiple_of(i * ns, ns), ns)

    @pl.when(i == 0)
    def _():
        for j in range(STATE_TILES):
            h0r_s[:, _lane_tile(j)] = h0r_ref[_lane_tile(j), :].T
            h0i_s[:, _lane_tile(j)] = h0i_ref[_lane_tile(j), :].T

    for j in range(STATE_TILES):
        hcr[j] = h0r_s[seqs, _lane_tile(j)]
        hci[j] = h0i_s[seqs, _lane_tile(j)]
    for j in range(CH_TILES):
        vext[j, FIRST_TAP * ns:hist_rows, :] = (
            cache_ref[:, :, _lane_tile(j)].reshape((CONV_WIDTH - 1) * ns, LANES))

    for s in range(ns):
        u_s = u_ref[s]
        v_s = val_ref[s] * jax.nn.sigmoid(gate_ref[s])
        for j in range(CH_TILES):
            utm[j, pl.ds(s, tt, stride=ns), :] = u_s[:, _lane_tile(j)]
            vext[j, pl.ds(hist_rows + s, tt, stride=ns), :] = v_s[:, _lane_tile(j)]

    _s5_input_and_conv(utm, vext, ytm, xr, xi, bblk_ref, cw_ref, ns=ns, rows=rows)
    _s5_scan(xr, xi, hcr, hci, ar_ref, ai_ref, ns=ns, tt=tt)
    u_all = jnp.concatenate([utm[j] for j in range(CH_TILES)], axis=1)
    s_out = _s5_readout(u_all, xr, xi, cre_ref, cim_ref, d_ref, wglu_ref)
    c_out = _ln_swish(ytm, cb_ref, lng_ref, lnb_ref)
    delta = _project_delta(s_out, c_out, wout_ref)
    _finish_projection(delta, x1_ref, gx_ref, wq_ref, x2_ref, q_ref, dtm, xn, ns=ns, tt=tt)
    for j in range(CH_TILES):
        hist_out[:, :, _lane_tile(j)] = (
            vext[j, (tt + FIRST_TAP) * ns:(tt + HIST) * ns, :].reshape(CONV_WIDTH - 1, ns, LANES))
    for j in range(STATE_TILES):
        hro_s[seqs, _lane_tile(j)] = hcr[j]
        hio_s[seqs, _lane_tile(j)] = hci[j]

    @pl.when(i == steps - 1)
    def _():
        for j in range(STATE_TILES):
            hre_out[_lane_tile(j), :] = hro_s[:, _lane_tile(j)].T
            him_out[_lane_tile(j), :] = hio_s[:, _lane_tile(j)].T


def _mixer(z3, x13, h0_re, h0_im, hist, w, *, ns):
    nseq, tt, _ = z3.shape
    rows = ns * tt
    assert nseq % ns == 0 and rows % CONV_ROWS == 0 and ns % SUBLANES == 0 and nseq % LANES == 0
    st_spec = pl.BlockSpec((N_STATE, nseq), lambda i: (0, 0))
    hist_spec = pl.BlockSpec((CONV_WIDTH - 1, ns, CONV_DIM), lambda i: (0, i, 0))
    row_spec = pl.BlockSpec((ns, tt, D_MODEL), lambda i: (i, 0, 0))
    z_specs = [pl.BlockSpec((ns, tt, SSM_DIM), (lambda c: (lambda i: (i, 0, c)))(c)) for c in range(3)]
    weights = [w[k] for k in _S5_WEIGHTS + _CONV_WEIGHTS + _OUT_WEIGHTS]
    return pl.pallas_call(
        functools.partial(_mixer_kernel, ns=ns, tt=tt, steps=nseq // ns),
        grid=(nseq // ns,),
        in_specs=z_specs + [row_spec, _const_spec((N_STATE, nseq)), _const_spec((N_STATE, nseq)), hist_spec]
                 + [_const_spec(a.shape) for a in weights],
        out_specs=(row_spec, row_spec, st_spec, st_spec, hist_spec),
        out_shape=(jax.ShapeDtypeStruct((nseq, tt, D_MODEL), F32),
                   jax.ShapeDtypeStruct((nseq, tt, D_MODEL), BF16),
                   jax.ShapeDtypeStruct((N_STATE, nseq), F32), jax.ShapeDtypeStruct((N_STATE, nseq), F32),
                   jax.ShapeDtypeStruct((CONV_WIDTH - 1, nseq, CONV_DIM), F32)),
        scratch_shapes=_mixer_scratch(ns, rows) + [
            pltpu.VMEM((CH_TILES, rows, LANES), F32),
            pltpu.VMEM((CH_TILES, rows + HIST * ns, LANES), F32),
            pltpu.VMEM((CH_TILES, rows, LANES), F32)]
        + [pltpu.VMEM((nseq, N_STATE), F32)] * 4,
        compiler_params=_params(("arbitrary",)),
        name="mixer",
    )(z3, z3, z3, x13, h0_re, h0_im, hist, *weights)


def _mixer_tm_kernel(u_ref, c_ref, x1_ref, bblk_ref, cre_ref, cim_ref, ar_ref, ai_ref, d_ref, wglu_ref,
                     wout_ref, gx_ref, wq_ref, x2_ref, q_ref, hre_out, him_out,
                     xr, xi, hcr, hci, dtm, xn, s_prev, *, ns, tt, steps):
    i = pl.program_id(0)

    @pl.when(i == 0)
    def _():
        hcr[...] = jnp.zeros_like(hcr)
        hci[...] = jnp.zeros_like(hci)
        s_prev[...] = jnp.zeros_like(s_prev)

    u = u_ref[...]
    for kb in range(CH_TILES):
        x = _dot(u[:, _lane_tile(kb)].astype(BF16), bblk_ref[kb])
        for t in range(TILES_PER_CH):
            xr[kb * TILES_PER_CH + t] = x[:, _lane_tile(t)]
            xi[kb * TILES_PER_CH + t] = x[:, _lane_tile(TILES_PER_CH + t)]

    delta = _project_delta(s_prev[...], c_ref[...], wout_ref)
    group_rows = SCAN_UNROLL * ns
    tiles_per_group = MODEL_TILES // (STATE_TILES // SCAN_TILE_GROUP)

    def after(group, tile_group):
        r0 = group * group_rows
        return delta[r0:r0 + SUBLANES, _lane_tile(tile_group * tiles_per_group)]

    _s5_scan(xr, xi, hcr, hci, ar_ref, ai_ref, ns=ns, tt=tt, after=after)
    _finish_projection(delta, x1_ref, gx_ref, wq_ref, x2_ref, q_ref, dtm, xn, ns=ns, tt=tt)
    s_prev[...] = _s5_readout(u, xr, xi, cre_ref, cim_ref, d_ref, wglu_ref)

    @pl.when(i == steps - 1)
    def _():
        _state_out(hre_out, him_out, hcr, hci)


def _mixer_tm(u_tm, c_tm, x13, w, *, tt):
    ns, seq_len, _ = x13.shape
    steps = seq_len // tt
    rows = ns * tt
    weights = [w[k] for k in _S5_WEIGHTS + _OUT_WEIGHTS]
    cur = lambda i: jnp.minimum(i, steps - 1)
    prev = lambda i: jnp.clip(i - 1, 0, steps - 1)
    tm_spec = lambda m: pl.BlockSpec((rows, SSM_DIM), lambda i: (m(i), 0))
    row_spec = pl.BlockSpec((ns, tt, D_MODEL), lambda i: (0, prev(i), 0))
    st_spec = pl.BlockSpec((ns, N_STATE), lambda i: (0, 0))
    return pl.pallas_call(
        functools.partial(_mixer_tm_kernel, ns=ns, tt=tt, steps=steps),
        grid=(steps + 1,),
        in_specs=[tm_spec(cur), tm_spec(prev), row_spec] + [_const_spec(a.shape) for a in weights],
        out_specs=(row_spec, row_spec, st_spec, st_spec),
        out_shape=(jax.ShapeDtypeStruct((ns, seq_len, D_MODEL), F32),
                   jax.ShapeDtypeStruct((ns, seq_len, D_MODEL), BF16),
                   jax.ShapeDtypeStruct((ns, N_STATE), F32), jax.ShapeDtypeStruct((ns, N_STATE), F32)),
        scratch_shapes=_mixer_scratch(ns, rows) + [pltpu.VMEM((rows, SSM_DIM), F32)],
        compiler_params=_params(("arbitrary",)),
        name="mixer_tm",
    )(u_tm, c_tm, x13, *weights)


def _row(x):
    return x.reshape(1, -1)


def kernel(x_prompt, x_sample, state_ssm_re, state_ssm_im, cache_conv, cache_mem_k, cache_mem_v, mem_prompt,
           g_mem, w_mem_k, w_mem_v, g_ffn1, w_ffn1_gate, w_ffn1_up, w_ffn1_down, g_mix, w_in,
           ssm_a_re, ssm_a_im, ssm_log_dt, ssm_b_re, ssm_b_im, ssm_c_re, ssm_c_im, ssm_d, w_ssm_glu,
           conv_w, conv_b, conv_ln_g, conv_ln_b, w_out, g_xattn, w_mem_q, w_mem_o,
           g_ffn2, w_ffn2_gate, w_ffn2_up, w_ffn2_down, g_final):
    depth = g_ffn1.shape[0]
    assert depth == 1
    l = 0
    bp, seq, _ = x_prompt.shape
    bs, dseq, _ = x_sample.shape

    ab_re, ab_im, bblk, cre, cim = _s5_params(ssm_a_re[l], ssm_a_im[l], ssm_log_dt[l:l + 1],
                                              ssm_b_re[l], ssm_b_im[l], ssm_c_re[l], ssm_c_im[l])
    mixer_w = dict(
        bblk=bblk, cre=cre, cim=cim, ab_re=ab_re, ab_im=ab_im,
        d=_row(ssm_d[l]),
        conv_w=jnp.swapaxes(conv_w[l].reshape(CONV_WIDTH, CH_TILES, LANES), 0, 1),
        conv_b=_row(conv_b[l]), ln_g=_row(conv_ln_g[l]), ln_b=_row(conv_ln_b[l]),
        g_x=_row(g_xattn[l]),
    )
    g_mix_r, g_final_r = _row(g_mix[l]), _row(g_final)

    first = [w_ffn1_gate[l], w_ffn1_up[l], w_ffn1_down[l], w_in[l], w_ssm_glu[l]]
    mk, mv, mk_hm, mv_hm, w1g_b, w1u_b, w1d_b, w_in_b, w_glu_b = _mem_kv(
        mem_prompt.reshape(bp * N_MEM, D_MODEL), _row(g_mem[l]), w_mem_k[l], w_mem_v[l], first)
    ffn1 = (_row(g_ffn1[l]), w1g_b, w1u_b, w1d_b)
    mixer_w["w_glu"] = w_glu_b

    p_rows, s_rows = bp * seq, bs * dseq
    prompt_tt = 64
    later = [w_out[l], w_mem_q[l], w_mem_o[l], w_ffn2_gate[l], w_ffn2_up[l], w_ffn2_down[l]]
    x1p, u_tm, c_tm, p_buf, w_out_b, w_q_b, w_o_b, w2g_b, w2u_b, w2d_b = _front(
        x_prompt, *ffn1, g_mix_r, w_in_b, *[mixer_w[k] for k in _CONV_WEIGHTS], later, tt=prompt_tt)
    mixer_w.update(w_out=w_out_b, w_q=w_q_b)
    ffn2 = (_row(g_ffn2[l]), w2g_b, w2u_b, w2d_b)
    x2p, qp, p_re, p_im = _mixer_tm(u_tm, c_tm, x1p, mixer_w, tt=prompt_tt)

    x1s, zs = _ffn_in(x_sample.reshape(s_rows, D_MODEL), *ffn1, g_mix_r, w_in_b)
    x2s, qs, s_re, s_im, s_buf = _mixer(
        zs.reshape(bs, dseq, -1), x1s.reshape(bs, dseq, D_MODEL),
        state_ssm_re[l].reshape(bs, N_STATE).T, state_ssm_im[l].reshape(bs, N_STATE).T,
        jnp.swapaxes(cache_conv[l], 0, 1), mixer_w, ns=32)

    kv_rows = lambda a: a.reshape(bs * N_MEM, MEM_HEADS, MEM_HEAD_DIM)
    yp, ys = _tail(x2p.reshape(p_rows, D_MODEL), qp.reshape(p_rows, D_MODEL), mk_hm, mv_hm,
                   x2s.reshape(s_rows, D_MODEL), qs.reshape(s_rows, D_MODEL),
                   kv_rows(cache_mem_k), kv_rows(cache_mem_v), w_o_b, *ffn2, g_final_r, tm=256)
    yp = yp.reshape(bp, seq, D_MODEL)
    ys = ys.reshape(bs, dseq, D_MODEL)

    st = lambda a, n: a.reshape(1, n, SSM_GROUPS, SSM_STATE)
    kv = lambda a: a.reshape(1, bp, N_MEM, MEM_HEADS, MEM_HEAD_DIM)
    buf = lambda a: jnp.swapaxes(a, 0, 1)[None]
    return (yp, ys, st(p_re, bp), st(p_im, bp), buf(p_buf), kv(mk), kv(mv),
            st(s_re.T, bs), st(s_im.T, bs), buf(s_buf))
```
